```python
import math
import jax, jax.numpy as jnp
from jax import lax
import numpy as np

D_MODEL = 1024
BATCH = 2
SEQ = 8192
DEPTH = 1
DEC_BATCH = 2
DEC_SEQ = 16384
PAST_LEN = 128

N_META = 16
MLA_H = 8
NOPE_DIM = 128
ROPE_DIM = 64
QK_DIM = NOPE_DIM + ROPE_DIM
V_DIM = 128
Q_LORA = 768
KV_LORA = 256
ROPE_THETA = 10000.0
Q_BLOCK = 128
GLA_H = 4
GLA_DK = 128
GLA_DV = 256
GATE_RANK = 16
GATE_TEMP = 16.0
GLA_CHUNK = 64
META_PAD = GLA_CHUNK - N_META
D_FF = 2816
NORM_EPS = 1e-6

IN_SIZES = (Q_LORA, KV_LORA, ROPE_DIM,
            GLA_H * GLA_DK, GLA_H * GLA_DK, GLA_H * GLA_DV, GLA_H * GLA_DV,
            GATE_RANK, GATE_RANK,
            D_MODEL, D_MODEL)
D_IN = Q_LORA + KV_LORA + ROPE_DIM + 2 * GLA_H * GLA_DK + 2 * GLA_H * GLA_DV + 2 * GATE_RANK + 2 * D_MODEL

kernel_name = "hybrid_mla_gla_gated_encoder"


def rmsnorm(x, g):
    xf = x.astype(jnp.float32)
    y = xf * lax.rsqrt(jnp.mean(xf * xf, axis=-1, keepdims=True) + NORM_EPS)
    return (y * g.astype(jnp.float32)).astype(x.dtype)


def split_cols(z, sizes):
    out = []
    off = 0
    for s in sizes:
        out.append(z[..., off:off + s])
        off += s
    return out


def rope_tables(length):
    inv = 1.0 / (ROPE_THETA ** (jnp.arange(0, ROPE_DIM, 2, dtype=jnp.float32) / ROPE_DIM))
    ang = jnp.arange(length, dtype=jnp.float32)[:, None] * inv[None, :]
    return jnp.cos(ang), jnp.sin(ang)


def apply_rope(x, cos, sin):
    xf = x.astype(jnp.float32)
    half = ROPE_DIM // 2
    x1, x2 = xf[..., :half], xf[..., half:]
    c = cos[None, :, None, :]
    s = sin[None, :, None, :]
    return jnp.concatenate([x1 * c - x2 * s, x2 * c + x1 * s], axis=-1).astype(x.dtype)


def block_attention(q, k, v):
    b, length, h, d = q.shape
    n_blk = -(-length // Q_BLOCK)
    pad = n_blk * Q_BLOCK - length
    qp = jnp.pad(q, ((0, 0), (0, pad), (0, 0), (0, 0)))
    qp = qp.reshape(b, n_blk, Q_BLOCK, h, d).transpose(1, 0, 2, 3, 4)
    scale = QK_DIM ** -0.5

    def one_block(qb):
        s = jnp.einsum('bqhd,bkhd->bhqk', qb, k, preferred_element_type=jnp.float32) * scale
        p = jax.nn.softmax(s, axis=-1)
        return jnp.einsum('bhqk,bkhv->bqhv', p.astype(v.dtype), v)

    o = lax.map(one_block, qp)
    o = o.transpose(1, 0, 2, 3, 4).reshape(b, n_blk * Q_BLOCK, h, v.shape[-1])
    return o[:, :length]


def mla_branch(c_q, c_kv, k_rope, q_a_norm, w_uq, kv_a_norm, w_ukv, q_norm, k_norm, w_o, cos, sin):
    b, length, _ = c_q.shape
    q = (rmsnorm(c_q, q_a_norm) @ w_uq).reshape(b, length, MLA_H, QK_DIM)
    kv = (rmsnorm(c_kv, kv_a_norm) @ w_ukv).reshape(b, length, MLA_H, NOPE_DIM + V_DIM)
    k_nope, v = kv[..., :NOPE_DIM], kv[..., NOPE_DIM:]
    k_r = jnp.broadcast_to(k_rope[:, :, None, :], (b, length, MLA_H, ROPE_DIM))
    k = jnp.concatenate([k_nope, k_r], axis=-1)
    q = rmsnorm(q, q_norm)
    k = rmsnorm(k, k_norm)
    q = jnp.concatenate([q[..., :NOPE_DIM], apply_rope(q[..., NOPE_DIM:], cos, sin)], axis=-1)
    k = jnp.concatenate([k[..., :NOPE_DIM], apply_rope(k[..., NOPE_DIM:], cos, sin)], axis=-1)
    o = block_attention(q, k, v)
    return o.reshape(b, length, MLA_H * V_DIM) @ w_o


def gla_chunked(q, k, v, log_g):
    b, t, h, dk = q.shape
    dv = v.shape[-1]
    n = t // GLA_CHUNK
    f32 = jnp.float32
    qc = q.astype(f32).reshape(b, n, GLA_CHUNK, h, dk)
    kc = k.astype(f32).reshape(b, n, GLA_CHUNK, h, dk)
    vc = v.astype(f32).reshape(b, n, GLA_CHUNK, h, dv)
    bcum = jnp.cumsum(log_g.astype(f32).reshape(b, n, GLA_CHUNK, h, dk), axis=2)
    b_last = bcum[:, :, -1:]
    b_ref = bcum[:, :, GLA_CHUNK // 2 - 1:GLA_CHUNK // 2]
    a = jnp.einsum('bnihd,bnjhd->bnhij', qc * jnp.exp(bcum - b_ref), kc * jnp.exp(b_ref - bcum))
    causal = jnp.tril(jnp.ones((GLA_CHUNK, GLA_CHUNK), dtype=bool))
    a = jnp.where(causal, a, 0.0)
    o_intra = jnp.einsum('bnhij,bnjhv->bnihv', a, vc)
    u = jnp.einsum('bnchd,bnchv->nbhdv', kc * jnp.exp(b_last - bcum), vc)
    decay = jnp.exp(b_last[:, :, 0]).transpose(1, 0, 2, 3)

    def step(s, inp):
        d, u_n = inp
        return d[..., None] * s + u_n, s

    _, s_prev = lax.scan(step, jnp.zeros((b, h, dk, dv), f32), (decay, u))
    o_inter = jnp.einsum('bnchd,nbhdv->bnchv', qc * jnp.exp(bcum), s_prev)
    return (o_intra + o_inter).reshape(b, t, h, dv).astype(v.dtype)


def gla_branch(q, k, v, g, a_f, a_b, w_a2_f, b_a2_f, w_a2_b, b_a2_b, o_norm, w_o):
    b, length, _ = q.shape
    q = q.reshape(b, length, GLA_H, GLA_DK) * (GLA_DK ** -0.5)
    k = k.reshape(b, length, GLA_H, GLA_DK)
    v = v.reshape(b, length, GLA_H, GLA_DV)
    lg_f = (jax.nn.log_sigmoid((a_f @ w_a2_f + b_a2_f).astype(jnp.float32)) / GATE_TEMP).reshape(b, length, GLA_H, GLA_DK)
    lg_b = (jax.nn.log_sigmoid((a_b @ w_a2_b + b_a2_b).astype(jnp.float32)) / GATE_TEMP).reshape(b, length, GLA_H, GLA_DK)
    pad = ((0, 0), (META_PAD, 0), (0, 0), (0, 0))
    qp, kp, vp = jnp.pad(q, pad), jnp.pad(k, pad), jnp.pad(v, pad)
    lfp, lbp = jnp.pad(lg_f, pad), jnp.pad(lg_b, pad)
    o_f = gla_chunked(qp, kp, vp, lfp)
    o_b = jnp.flip(gla_chunked(jnp.flip(qp, 1), jnp.flip(kp, 1), jnp.flip(vp, 1), jnp.flip(lbp, 1)), 1)
    o = (o_f + o_b)[:, META_PAD:]
    o = rmsnorm(o, o_norm) * jax.nn.silu(g).reshape(b, length, GLA_H, GLA_DV)
    return o.reshape(b, length, GLA_H * GLA_DV) @ w_o


def encoder_layer(x, attn_norm, w_in, q_a_norm, w_uq, kv_a_norm, w_ukv, q_norm, k_norm, w_o_mla,
                  w_a2_fwd, b_a2_fwd, w_a2_bwd, b_a2_bwd, gla_o_norm, w_o_gla, w_out,
                  ffn_norm, w_ffn_gate, w_ffn_up, w_ffn_down):
    length = x.shape[1]
    h = rmsnorm(x, attn_norm)
    z = h @ w_in
    (c_q, c_kv, k_rope, gq, gk, gv, gg, a_f, a_b, gate_a, gate_b) = split_cols(z, IN_SIZES)
    cos, sin = rope_tables(length)
    y_a = mla_branch(c_q, c_kv, k_rope, q_a_norm, w_uq, kv_a_norm, w_ukv, q_norm, k_norm, w_o_mla, cos, sin)
    y_b = gla_branch(gq, gk, gv, gg, a_f, a_b, w_a2_fwd, b_a2_fwd, w_a2_bwd, b_a2_bwd, gla_o_norm, w_o_gla)
    mixed = (jax.nn.sigmoid(gate_a) * y_a + jax.nn.sigmoid(gate_b) * y_b) @ w_out
    x = x + mixed
    h = rmsnorm(x, ffn_norm)
    x = x + (jax.nn.silu(h @ w_ffn_gate) * (h @ w_ffn_up)) @ w_ffn_down
    return x


def setup_inputs(seed: int = 0) -> dict:
    key = jax.random.key(seed)
    ks = jax.random.split(key, 24)
    f32 = jnp.float32

    def nrm(k, shape, scale):
        return jax.random.normal(k, shape, f32) * scale

    def gain(k, n):
        return 1.0 + 0.01 * jax.random.normal(k, (DEPTH, n), f32)

    return {
        "x_prompt": nrm(ks[0], (BATCH, SEQ, D_MODEL), 1.0),
        "x_sample": nrm(ks[1], (DEC_BATCH, DEC_SEQ, D_MODEL), 1.0),
        "meta_tokens": nrm(ks[2], (N_META, D_MODEL), 1.0),
        "attn_norm": gain(ks[3], D_MODEL),
        "w_in": nrm(ks[4], (DEPTH, D_MODEL, D_IN), D_MODEL ** -0.5),
        "q_a_norm": gain(ks[5], Q_LORA),
        "w_uq": nrm(ks[6], (DEPTH, Q_LORA, MLA_H * QK_DIM), Q_LORA ** -0.5),
        "kv_a_norm": gain(ks[7], KV_LORA),
        "w_ukv": nrm(ks[8], (DEPTH, KV_LORA, MLA_H * (NOPE_DIM + V_DIM)), KV_LORA ** -0.5),
        "q_norm": gain(ks[9], QK_DIM),
        "k_norm": gain(ks[10], QK_DIM),
        "w_o_mla": nrm(ks[11], (DEPTH, MLA_H * V_DIM, D_MODEL), (MLA_H * V_DIM) ** -0.5),
        "w_a2_fwd": nrm(ks[12], (DEPTH, GATE_RANK, GLA_H * GLA_DK), GATE_RANK ** -0.5),
        "b_a2_fwd": nrm(ks[13], (DEPTH, GLA_H * GLA_DK), 0.1),
        "w_a2_bwd": nrm(ks[14], (DEPTH, GATE_RANK, GLA_H * GLA_DK), GATE_RANK ** -0.5),
        "b_a2_bwd": nrm(ks[15], (DEPTH, GLA_H * GLA_DK), 0.1),
        "gla_o_norm": gain(ks[16], GLA_DV),
        "w_o_gla": nrm(ks[17], (DEPTH, GLA_H * GLA_DV, D_MODEL), (GLA_H * GLA_DV) ** -0.5),
        "w_out": nrm(ks[18], (DEPTH, D_MODEL, D_MODEL), D_MODEL ** -0.5),
        "ffn_norm": gain(ks[19], D_MODEL),
        "w_ffn_gate": nrm(ks[20], (DEPTH, D_MODEL, D_FF), D_MODEL ** -0.5),
        "w_ffn_up": nrm(ks[21], (DEPTH, D_MODEL, D_FF), D_MODEL ** -0.5),
        "w_ffn_down": nrm(ks[22], (DEPTH, D_FF, D_MODEL), D_FF ** -0.5),
    }


def reference(x_prompt, x_sample, meta_tokens, attn_norm, w_in, q_a_norm, w_uq, kv_a_norm, w_ukv,
              q_norm, k_norm, w_o_mla, w_a2_fwd, b_a2_fwd, w_a2_bwd, b_a2_bwd, gla_o_norm, w_o_gla,
              w_out, ffn_norm, w_ffn_gate, w_ffn_up, w_ffn_down):
    def encoder(x):
        b = x.shape[0]
        meta = jnp.broadcast_to(meta_tokens[None].astype(x.dtype), (b, N_META, D_MODEL))
        x = jnp.concatenate([meta, x], axis=1)
        for l in range(DEPTH):
            x = encoder_layer(x, attn_norm[l], w_in[l], q_a_norm[l], w_uq[l], kv_a_norm[l], w_ukv[l],
                              q_norm[l], k_norm[l], w_o_mla[l], w_a2_fwd[l], b_a2_fwd[l], w_a2_bwd[l],
                              b_a2_bwd[l], gla_o_norm[l], w_o_gla[l], w_out[l], ffn_norm[l],
                              w_ffn_gate[l], w_ffn_up[l], w_ffn_down[l])
        return x[:, N_META:]

    y_prompt = encoder(x_prompt)
    y_sample = encoder(x_sample)
    return (y_prompt, y_sample)
```

```python
import functools
import math

import jax
import jax.numpy as jnp
from jax import lax
from jax.experimental import pallas as pl
from jax.experimental.pallas import tpu as pltpu

D_MODEL = 1024
N_META = 16
MLA_H = 8
NOPE_DIM = 128
ROPE_DIM = 64
QK_DIM = NOPE_DIM + ROPE_DIM
V_DIM = 128
Q_LORA = 768
KV_LORA = 256
ROPE_THETA = 10000.0
GLA_H = 4
GLA_DK = 128
GLA_DV = 256
GATE_RANK = 16
GATE_TEMP = 16.0
GLA_CHUNK = 64
D_FF = 2816
NORM_EPS = 1e-6

QK_PAD = 256
LANES = 128
LOG2E = 1.4426950408889634
VMEM_LIMIT = 56 * 1024 * 1024

F32 = jnp.float32
BF16 = jnp.bfloat16


def _dot(a, b):
    return jnp.dot(a, b, preferred_element_type=F32)


def _dot_nt(a, b):
    return lax.dot_general(a, b, (((1,), (1,)), ((), ())), preferred_element_type=F32)


def _dot_tn(a, b):
    return lax.dot_general(a, b, (((0,), (0,)), ((), ())), preferred_element_type=F32)


def _rms(x):
    return x * lax.rsqrt(jnp.mean(x * x, axis=-1, keepdims=True) + NORM_EPS)


def _resident(shape):
    return pl.BlockSpec(shape, lambda *_: (0,) * len(shape), pipeline_mode=pl.Buffered(1))


def _params(*sem):
    return pltpu.CompilerParams(dimension_semantics=sem, vmem_limit_bytes=VMEM_LIMIT)


def _row_tile(n, want):
    t = min(n, want)
    assert n % t == 0, (n, t)
    return t


def _mla_proj_kernel(x_ref, an_ref, wcq_ref, wckv_ref, qan_ref, wuq_ref, kvan_ref, wukv_ref,
                     gq_ref, gk_ref, cos_ref, sin_ref, q_out, k_out, v_out):
    x = x_ref[...]
    h = (_rms(x) * an_ref[...]).astype(BF16)
    cq = _dot(h, wcq_ref[...])
    cqn = (_rms(cq) * qan_ref[...]).astype(BF16)
    qall = _dot(cqn, wuq_ref[...])
    ckvr = _dot(h, wckv_ref[...])
    ckvn = (_rms(ckvr[:, :KV_LORA]) * kvan_ref[...]).astype(BF16)
    kv = _dot(ckvn, wukv_ref[...])
    v_out[...] = kv[:, MLA_H * NOPE_DIM:].astype(BF16)

    cos = cos_ref[...]
    sin = sin_ref[...]
    lane = lax.broadcasted_iota(jnp.int32, (1, LANES), 1)
    ka = ckvr[:, KV_LORA:KV_LORA + LANES]
    kb = ckvr[:, KV_LORA + LANES:KV_LORA + 2 * LANES]
    head_lanes = [((lane >= p * ROPE_DIM) & (lane < (p + 1) * ROPE_DIM)).astype(F32) for p in range(2)]
    ssq_kr = jnp.sum(ka * ka * head_lanes[0], axis=-1, keepdims=True)
    k_rot = [(ka * gk_ref[1 + p:2 + p, :]) * cos + (kb * gk_ref[3 + p:4 + p, :]) * sin for p in range(2)]
    q_scale = QK_DIM ** -0.5 * LOG2E
    rope0 = MLA_H * NOPE_DIM
    for hd in range(MLA_H):
        j, p = divmod(hd, 2)
        qn = qall[:, hd * NOPE_DIM:(hd + 1) * NOPE_DIM]
        qa = qall[:, rope0 + j * LANES:rope0 + (j + 1) * LANES]
        qb = qall[:, rope0 + 4 * LANES + j * LANES:rope0 + 4 * LANES + (j + 1) * LANES]
        ssq = (jnp.sum(qn * qn, axis=-1, keepdims=True)
               + jnp.sum(qa * qa * head_lanes[p], axis=-1, keepdims=True))
        rs = lax.rsqrt(ssq * (1.0 / QK_DIM) + NORM_EPS) * q_scale
        q_rot = (qa * gq_ref[1 + p:2 + p, :]) * cos + (qb * gq_ref[3 + p:4 + p, :]) * sin
        q_out[:, hd * QK_PAD:hd * QK_PAD + LANES] = (qn * gq_ref[0:1, :] * rs).astype(BF16)
        q_out[:, hd * QK_PAD + LANES:(hd + 1) * QK_PAD] = (q_rot * rs).astype(BF16)
        kn = kv[:, hd * NOPE_DIM:(hd + 1) * NOPE_DIM]
        rsk = lax.rsqrt((jnp.sum(kn * kn, axis=-1, keepdims=True) + ssq_kr) * (1.0 / QK_DIM) + NORM_EPS)
        k_out[:, hd * QK_PAD:hd * QK_PAD + LANES] = (kn * gk_ref[0:1, :] * rsk).astype(BF16)
        k_out[:, hd * QK_PAD + LANES:(hd + 1) * QK_PAD] = (k_rot[p] * rsk).astype(BF16)


def _mla_proj(x2d, w, cos_t, sin_t, seq_len, tm):
    n = x2d.shape[0]
    nt = seq_len // tm
    row = lambda i: (i, 0)
    pos = lambda i: (i % nt, 0)
    return pl.pallas_call(
        _mla_proj_kernel,
        grid=(n // tm,),
        in_specs=[
            pl.BlockSpec((tm, D_MODEL), row),
            _resident((1, D_MODEL)),
            _resident(w["w_cq"].shape),
            _resident(w["w_ckvr"].shape),
            _resident((1, Q_LORA)),
            _resident(w["w_uq"].shape),
            _resident((1, KV_LORA)),
            _resident(w["w_ukv"].shape),
            _resident((8, LANES)),
            _resident((8, LANES)),
            pl.BlockSpec((tm, LANES), pos),
            pl.BlockSpec((tm, LANES), pos),
        ],
        out_specs=[
            pl.BlockSpec((tm, MLA_H * QK_PAD), row),
            pl.BlockSpec((tm, MLA_H * QK_PAD), row),
            pl.BlockSpec((tm, MLA_H * V_DIM), row),
        ],
        out_shape=[
            jax.ShapeDtypeStruct((n, MLA_H * QK_PAD), BF16),
            jax.ShapeDtypeStruct((n, MLA_H * QK_PAD), BF16),
            jax.ShapeDtypeStruct((n, MLA_H * V_DIM), BF16),
        ],
        compiler_params=_params("parallel"),
        name="mla_proj",
    )(x2d, w["attn_norm"], w["w_cq"], w["w_ckvr"], w["q_a_norm"], w["w_uq"], w["kv_a_norm"],
      w["w_ukv"], w["g_q"], w["g_k"], cos_t, sin_t)


G_Q0, G_K0, G_V0, G_G0, G_A0, G_S0, G_END = 0, 512, 1024, 2048, 3072, 3200, 5248


def _gla_proj_kernel(x_ref, an_ref, wg_ref, wa2_ref, ba2_ref,
                     q_out, k_out, v_out, og_out, lg_out, gate_out):
    x = x_ref[...]
    h = (_rms(x) * an_ref[...]).astype(BF16)
    q_out[...] = _dot(h, wg_ref[:, G_Q0:G_K0]) * (GLA_DK ** -0.5)
    k_out[...] = _dot(h, wg_ref[:, G_K0:G_V0])
    v_out[...] = _dot(h, wg_ref[:, G_V0:G_G0]).astype(BF16)
    g = _dot(h, wg_ref[:, G_G0:G_A0])
    og_out[...] = g * jax.nn.sigmoid(g)
    a = _dot(h, wg_ref[:, G_A0:G_S0]).astype(BF16)
    pre = _dot(a, wa2_ref[...]) + ba2_ref[...]
    lg_out[...] = jax.nn.log_sigmoid(pre) * (1.0 / GATE_TEMP)
    gate_out[...] = jax.nn.sigmoid(_dot(h, wg_ref[:, G_S0:G_END]))


def _gla_proj(x2d, w, tm):
    n = x2d.shape[0]
    row = lambda i: (i, 0)
    widths = (GLA_H * GLA_DK, GLA_H * GLA_DK, GLA_H * GLA_DV, GLA_H * GLA_DV, 2 * GLA_H * GLA_DK, 2 * D_MODEL)
    dtypes = (F32, F32, BF16, F32, F32, F32)
    return pl.pallas_call(
        _gla_proj_kernel,
        grid=(n // tm,),
        in_specs=[
            pl.BlockSpec((tm, D_MODEL), row),
            _resident((1, D_MODEL)),
            _resident(w["w_g"].shape),
            _resident(w["w_a2"].shape),
            _resident((1, 2 * GLA_H * GLA_DK)),
        ],
        out_specs=[pl.BlockSpec((tm, wd), row) for wd in widths],
        out_shape=[jax.ShapeDtypeStruct((n, wd), dt) for wd, dt in zip(widths, dtypes)],
        compiler_params=_params("parallel"),
        name="gla_proj",
    )(x2d, w["attn_norm"], w["w_g"], w["w_a2"], w["b_a2"])


def _attn_kernel(q_ref, k_ref, v_ref, km_ref, vm_ref, o_ref, *, tk):
    q = q_ref[0]
    s0 = _dot_nt(q, km_ref[...])
    m0 = jnp.max(s0, axis=-1, keepdims=True)
    p0 = jnp.exp2(s0 - m0)
    l0 = jnp.sum(p0, axis=-1, keepdims=True)
    acc0 = _dot(p0.astype(BF16), vm_ref[...])

    def body(i, carry):
        m, l, acc = carry
        off = pl.multiple_of(i * tk, tk)
        s = _dot_nt(q, k_ref[0, pl.ds(off, tk), :])
        m_new = jnp.maximum(m, jnp.max(s, axis=-1, keepdims=True))
        alpha = jnp.exp2(m - m_new)
        p = jnp.exp2(s - m_new)
        l = alpha * l + jnp.sum(p, axis=-1, keepdims=True)
        acc = alpha * acc + _dot(p.astype(BF16), v_ref[0, pl.ds(off, tk), :])
        return m_new, l, acc

    _, l, acc = lax.fori_loop(0, k_ref.shape[1] // tk, body, (m0, l0, acc0))
    o_ref[0] = (acc / l).astype(o_ref.dtype)


def _attention(q, k, v, k_meta, v_meta, tq, tk):
    b, seq, _ = q.shape
    return pl.pallas_call(
        functools.partial(_attn_kernel, tk=tk),
        grid=(b, MLA_H, seq // tq),
        in_specs=[
            pl.BlockSpec((1, tq, QK_PAD), lambda bi, hi, qi: (bi, qi, hi)),
            pl.BlockSpec((1, seq, QK_PAD), lambda bi, hi, qi: (bi, 0, hi)),
            pl.BlockSpec((1, seq, V_DIM), lambda bi, hi, qi: (bi, 0, hi)),
            pl.BlockSpec((N_META, QK_PAD), lambda bi, hi, qi: (0, hi)),
            pl.BlockSpec((N_META, V_DIM), lambda bi, hi, qi: (0, hi)),
        ],
        out_specs=pl.BlockSpec((1, tq, V_DIM), lambda bi, hi, qi: (bi, qi, hi)),
        out_shape=jax.ShapeDtypeStruct((b, seq, MLA_H * V_DIM), BF16),
        compiler_params=_params("parallel", "parallel", "arbitrary"),
        name="attention",
    )(q, k, v, k_meta, v_meta)


def _gla_scan_kernel(q_ref, k_ref, v_ref, lg_ref, s0_ref, o_ref, sfin_ref, state, *, reverse, n_chunks):
    blk = pl.program_id(1)

    @pl.when(blk == 0)
    def _():
        state[...] = s0_ref[...]

    ri = lax.broadcasted_iota(jnp.int32, (GLA_CHUNK, GLA_CHUNK), 0)
    ci = lax.broadcasted_iota(jnp.int32, (GLA_CHUNK, GLA_CHUNK), 1)
    keep = (ri <= ci) if reverse else (ri >= ci)
    csum = jnp.where(keep, 1.0, 0.0).astype(BF16)
    mid = GLA_CHUNK // 2 if reverse else GLA_CHUNK // 2 - 1
    end = 0 if reverse else GLA_CHUNK - 1

    def chunk(c, carry):
        cc = (n_chunks - 1 - c) if reverse else c
        r0 = pl.multiple_of(cc * GLA_CHUNK, GLA_CHUNK)
        rows = pl.ds(r0, GLA_CHUNK)
        for hd in range(GLA_H):
            kcols = slice(hd * GLA_DK, (hd + 1) * GLA_DK)
            vcols = slice(hd * GLA_DV, (hd + 1) * GLA_DV)
            lg = lg_ref[0, rows, kcols]
            lg_hi = lg.astype(BF16)
            lg_lo = (lg - lg_hi.astype(F32)).astype(BF16)
            bcum = _dot(csum, lg_hi) + _dot(csum, lg_lo)
            b_mid = bcum[mid:mid + 1, :]
            b_end = bcum[end:end + 1, :]
            q = q_ref[0, rows, kcols]
            k = k_ref[0, rows, kcols]
            v = v_ref[0, rows, vcols]
            qs = (q * jnp.exp(bcum - b_mid)).astype(BF16)
            ks = (k * jnp.exp(b_mid - bcum)).astype(BF16)
            a = jnp.where(keep, _dot_nt(qs, ks), 0.0).astype(BF16)
            st = state[hd]
            qe = (q * jnp.exp(bcum)).astype(BF16)
            o = _dot(a, v) + _dot_nt(qe, st.astype(BF16))
            o_ref[0, rows, vcols] = o
            kd = (k * jnp.exp(b_end - bcum)).astype(BF16)
            state[hd] = st * jnp.exp(b_end) + _dot_tn(v, kd)
        return carry

    lax.fori_loop(0, n_chunks, chunk, 0)

    @pl.when(blk == pl.num_programs(1) - 1)
    def _():
        sfin_ref[0] = state[...]


def _gla_scan(q, k, v, lg, s0, reverse, t_blk):
    b, t, _ = q.shape
    nb = t // t_blk
    lg_col = 1 if reverse else 0
    blk = (lambda bi, i: (bi, nb - 1 - i, 0)) if reverse else (lambda bi, i: (bi, i, 0))
    lg_blk = (lambda bi, i: (bi, nb - 1 - i, lg_col)) if reverse else (lambda bi, i: (bi, i, lg_col))
    st_shape = (GLA_H, GLA_DV, GLA_DK)
    return pl.pallas_call(
        functools.partial(_gla_scan_kernel, reverse=reverse, n_chunks=t_blk // GLA_CHUNK),
        grid=(b, nb),
        in_specs=[
            pl.BlockSpec((1, t_blk, GLA_H * GLA_DK), blk),
            pl.BlockSpec((1, t_blk, GLA_H * GLA_DK), blk),
            pl.BlockSpec((1, t_blk, GLA_H * GLA_DV), blk),
            pl.BlockSpec((1, t_blk, GLA_H * GLA_DK), lg_blk),
            pl.BlockSpec(st_shape, lambda bi, i: (0, 0, 0)),
        ],
        out_specs=[
            pl.BlockSpec((1, t_blk, GLA_H * GLA_DV), blk),
            pl.BlockSpec((1,) + st_shape, lambda bi, i: (bi, 0, 0, 0)),
        ],
        out_shape=[
            jax.ShapeDtypeStruct((b, t, GLA_H * GLA_DV), F32),
            jax.ShapeDtypeStruct((b,) + st_shape, F32),
        ],
        scratch_shapes=[pltpu.VMEM(st_shape, F32)],
        compiler_params=_params("parallel", "arbitrary"),
        name="gla_scan_bwd" if reverse else "gla_scan_fwd",
    )(q, k, v, lg, s0)


def _mix_ffn_kernel(x_ref, oa_ref, of_ref, ob_ref, og_ref, gate_ref, womla_ref, onorm_ref, wogla_ref,
                    wout_ref, fn_ref, wgate_ref, wup_ref, wdown_ref, y_ref):
    y_a = _dot(oa_ref[...], womla_ref[...])
    o = of_ref[...] + ob_ref[...]
    og = og_ref[...]
    parts = []
    for hd in range(GLA_H):
        cols = slice(hd * GLA_DV, (hd + 1) * GLA_DV)
        parts.append((_rms(o[:, cols]) * onorm_ref[...] * og[:, cols]).astype(BF16))
    y_b = _dot(jnp.concatenate(parts, axis=-1), wogla_ref[...])
    gates = gate_ref[...]
    mixed = (gates[:, :D_MODEL] * y_a + gates[:, D_MODEL:] * y_b).astype(BF16)
    x1 = x_ref[...] + _dot(mixed, wout_ref[...])
    h = (_rms(x1) * fn_ref[...]).astype(BF16)
    g = _dot(h, wgate_ref[...])
    u = _dot(h, wup_ref[...])
    act = (g * jax.nn.sigmoid(g) * u).astype(BF16)
    y_ref[...] = x1 + _dot(act, wdown_ref[...])


def _mix_ffn(x2d, o_attn, o_f, o_b, og, gates, w, tm):
    n = x2d.shape[0]
    row = lambda i: (i, 0)
    return pl.pallas_call(
        _mix_ffn_kernel,
        grid=(n // tm,),
        in_specs=[
            pl.BlockSpec((tm, D_MODEL), row),
            pl.BlockSpec((tm, MLA_H * V_DIM), row),
            pl.BlockSpec((tm, GLA_H * GLA_DV), row),
            pl.BlockSpec((tm, GLA_H * GLA_DV), row),
            pl.BlockSpec((tm, GLA_H * GLA_DV), row),
            pl.BlockSpec((tm, 2 * D_MODEL), row),
            _resident(w["w_o_mla"].shape),
            _resident((1, GLA_DV)),
            _resident(w["w_o_gla"].shape),
            _resident(w["w_out"].shape),
            _resident((1, D_MODEL)),
            _resident(w["w_ffn_gate"].shape),
            _resident(w["w_ffn_up"].shape),
            _resident(w["w_ffn_down"].shape),
        ],
        out_specs=pl.BlockSpec((tm, D_MODEL), row),
        out_shape=jax.ShapeDtypeStruct((n, D_MODEL), F32),
        compiler_params=_params("parallel"),
        name="mix_ffn",
    )(x2d, o_attn, o_f, o_b, og, gates, w["w_o_mla"], w["gla_o_norm"], w["w_o_gla"], w["w_out"],
      w["ffn_norm"], w["w_ffn_gate"], w["w_ffn_up"], w["w_ffn_down"])


def _prep_weights(attn_norm, w_in, q_a_norm, w_uq, kv_a_norm, w_ukv, q_norm, k_norm, w_o_mla,
                  w_a2_fwd, b_a2_fwd, w_a2_bwd, b_a2_bwd, gla_o_norm, w_o_gla, w_out, ffn_norm,
                  w_ffn_gate, w_ffn_up, w_ffn_down):
    half = ROPE_DIM // 2
    o = 0
    cols = {}
    for name, size in (("cq", Q_LORA), ("ckv", KV_LORA), ("kr", ROPE_DIM), ("gq", GLA_H * GLA_DK),
                       ("gk", GLA_H * GLA_DK), ("gv", GLA_H * GLA_DV), ("gg", GLA_H * GLA_DV),
                       ("af", GATE_RANK), ("ab", GATE_RANK), ("ga", D_MODEL), ("gb", D_MODEL)):
        cols[name] = w_in[:, o:o + size]
        o += size
    kr1, kr2 = cols["kr"][:, :half], cols["kr"][:, half:]
    w_ckvr = jnp.concatenate([cols["ckv"], kr1, kr2, kr1, kr2, kr2, kr1, kr2, kr1], axis=1)

    uq = w_uq.reshape(Q_LORA, MLA_H, QK_DIM)
    uq_nope = uq[:, :, :NOPE_DIM].reshape(Q_LORA, MLA_H * NOPE_DIM)
    uq_r = uq[:, :, NOPE_DIM:]
    uq_a = uq_r.reshape(Q_LORA, MLA_H * ROPE_DIM)
    uq_b = jnp.concatenate([uq_r[:, :, half:], uq_r[:, :, :half]], axis=-1).reshape(Q_LORA, MLA_H * ROPE_DIM)
    ukv = w_ukv.reshape(KV_LORA, MLA_H, NOPE_DIM + V_DIM)
    w_ukv2 = jnp.concatenate([ukv[:, :, :NOPE_DIM].reshape(KV_LORA, MLA_H * NOPE_DIM),
                              ukv[:, :, NOPE_DIM:].reshape(KV_LORA, MLA_H * V_DIM)], axis=1)

    def gain_rows(g):
        z = jnp.zeros((ROPE_DIM,), F32)
        ga = g[NOPE_DIM:]
        gb = jnp.concatenate([g[NOPE_DIM + half:], g[NOPE_DIM:NOPE_DIM + half]])
        rows = [g[:NOPE_DIM], jnp.concatenate([ga, z]), jnp.concatenate([z, ga]),
                jnp.concatenate([gb, z]), jnp.concatenate([z, gb])]
        rows += [jnp.zeros((LANES,), F32)] * 3
        return jnp.stack(rows)

    zpad = jnp.zeros((D_MODEL, LANES - 2 * GATE_RANK), F32)
    w_g = jnp.concatenate([cols["gq"], cols["gk"], cols["gv"], cols["gg"], cols["af"], cols["ab"], zpad,
                           cols["ga"], cols["gb"]], axis=1)
    nk = GLA_H * GLA_DK
    w_a2 = jnp.zeros((LANES, 2 * nk), F32)
    w_a2 = w_a2.at[:GATE_RANK, :nk].set(w_a2_fwd).at[GATE_RANK:2 * GATE_RANK, nk:].set(w_a2_bwd)
    return {
        "attn_norm": attn_norm[None], "w_cq": cols["cq"].astype(BF16), "w_ckvr": w_ckvr.astype(BF16),
        "q_a_norm": q_a_norm[None], "w_uq": jnp.concatenate([uq_nope, uq_a, uq_b], axis=1).astype(BF16),
        "kv_a_norm": kv_a_norm[None], "w_ukv": w_ukv2.astype(BF16),
        "g_q": gain_rows(q_norm), "g_k": gain_rows(k_norm),
        "w_g": w_g.astype(BF16), "w_a2": w_a2.astype(BF16),
        "b_a2": jnp.concatenate([b_a2_fwd, b_a2_bwd])[None],
        "w_o_mla": w_o_mla.astype(BF16), "gla_o_norm": gla_o_norm[None], "w_o_gla": w_o_gla.astype(BF16),
        "w_out": w_out.astype(BF16), "ffn_norm": ffn_norm[None], "w_ffn_gate": w_ffn_gate.astype(BF16),
        "w_ffn_up": w_ffn_up.astype(BF16), "w_ffn_down": w_ffn_down.astype(BF16),
    }


def _rope_tables(length):
    inv = 1.0 / (ROPE_THETA ** (jnp.arange(0, ROPE_DIM, 2, dtype=F32) / ROPE_DIM))
    ang = jnp.arange(length, dtype=F32)[:, None] * inv[None, :]
    cos, sin = jnp.cos(ang), jnp.sin(ang)
    return jnp.tile(cos, (1, 4)), jnp.concatenate([-sin, sin, -sin, sin], axis=1)


def _encode_group(x, w, meta, cos_t, sin_t):
    b, seq, _ = x.shape
    n = b * seq
    x2d = x.reshape(n, D_MODEL)
    tm = _row_tile(seq, 256)
    q, k, v = _mla_proj(x2d, w, cos_t[N_META:N_META + seq], sin_t[N_META:N_META + seq], seq, tm)
    gq, gk, gv, og, lg, gates = _gla_proj(x2d, w, tm)

    o_attn = _attention(q.reshape(b, seq, -1), k.reshape(b, seq, -1), v.reshape(b, seq, -1),
                        meta["k"], meta["v"], _row_tile(seq, 512), _row_tile(seq, 512))

    t_blk = _row_tile(seq, 512)
    g3 = lambda a: a.reshape(b, seq, -1)
    o_f, _ = _gla_scan(g3(gq), g3(gk), g3(gv), g3(lg), meta["state"], False, t_blk)
    o_b, _ = _gla_scan(g3(gq), g3(gk), g3(gv), g3(lg), jnp.zeros_like(meta["state"]), True, t_blk)

    y = _mix_ffn(x2d, o_attn.reshape(n, -1), o_f.reshape(n, -1), o_b.reshape(n, -1), og, gates, w, tm)
    return y.reshape(b, seq, D_MODEL)


def kernel(x_prompt, x_sample, meta_tokens, attn_norm, w_in, q_a_norm, w_uq, kv_a_norm, w_ukv, q_norm, k_norm, w_o_mla, w_a2_fwd, b_a2_fwd, w_a2_bwd, b_a2_bwd, gla_o_norm, w_o_gla, w_out, ffn_norm, w_ffn_gate, w_ffn_up, w_ffn_down):
    assert attn_norm.shape[0] == 1, "single-layer encoder"
    w = _prep_weights(attn_norm[0], w_in[0], q_a_norm[0], w_uq[0], kv_a_norm[0], w_ukv[0], q_norm[0],
                      k_norm[0], w_o_mla[0], w_a2_fwd[0], b_a2_fwd[0], w_a2_bwd[0], b_a2_bwd[0],
                      gla_o_norm[0], w_o_gla[0], w_out[0], ffn_norm[0], w_ffn_gate[0], w_ffn_up[0],
                      w_ffn_down[0])
    max_len = N_META + max(x_prompt.shape[1], x_sample.shape[1])
    cos_t, sin_t = _rope_tables(max_len)

    xm = meta_tokens.astype(F32)
    _, k_m, v_m = _mla_proj(xm, w, cos_t[:N_META], sin_t[:N_META], N_META, N_META)
    mq, mk, mv, _, mlg, _ = _gla_proj(xm, w, N_META)
    pad = lambda a: jnp.pad(a, ((GLA_CHUNK - N_META, 0), (0, 0)))[None]
    _, s_meta = _gla_scan(pad(mq), pad(mk), pad(mv), pad(mlg),
                          jnp.zeros((GLA_H, GLA_DV, GLA_DK), F32), False, GLA_CHUNK)
    meta = {"k": k_m, "v": v_m, "state": s_meta[0]}

    return (_encode_group(x_prompt, w, meta, cos_t, sin_t),
            _encode_group(x_sample, w, meta, cos_t, sin_t))
```

```python
import functools
import math

import jax
import jax.numpy as jnp
from jax import lax
from jax.experimental import pallas as pl
from jax.experimental.pallas import tpu as pltpu

D_MODEL = 1024
N_META = 16
MLA_H = 8
NOPE_DIM = 128
ROPE_DIM = 64
QK_DIM = NOPE_DIM + ROPE_DIM
V_DIM = 128
Q_LORA = 768
KV_LORA = 256
ROPE_THETA = 10000.0
GLA_H = 4
GLA_DK = 128
GLA_DV = 256
GATE_RANK = 16
GATE_TEMP = 16.0
GLA_CHUNK = 64
D_FF = 2816
NORM_EPS = 1e-6

QK_PAD = 256
ONES_ROWS = 16
VT_ROWS = V_DIM + ONES_ROWS
LANES = 128
LOG2E = 1.4426950408889634
VMEM_LIMIT = 56 * 1024 * 1024

F32 = jnp.float32
BF16 = jnp.bfloat16


def _dot(a, b):
    return jnp.dot(a, b, preferred_element_type=F32)


def _dot_nt(a, b):
    return lax.dot_general(a, b, (((1,), (1,)), ((), ())), preferred_element_type=F32)


def _dot_tn(a, b):
    return lax.dot_general(a, b, (((0,), (0,)), ((), ())), preferred_element_type=F32)


def _rms(x):
    return x * lax.rsqrt(jnp.mean(x * x, axis=-1, keepdims=True) + NORM_EPS)


def _resident(shape):
    return pl.BlockSpec(shape, lambda *_: (0,) * len(shape), pipeline_mode=pl.Buffered(1))


def _params(*sem):
    return pltpu.CompilerParams(dimension_semantics=sem, vmem_limit_bytes=VMEM_LIMIT)


def _row_tile(n, want):
    t = min(n, want)
    assert n % t == 0, (n, t)
    return t


def _mla_proj_kernel(x_ref, an_ref, wcq_ref, wckv_ref, qan_ref, wuq_ref, kvan_ref, wukv_ref,
                     gq_ref, gk_ref, cos_ref, sin_ref, q_out, k_out, v_out, *, transpose_v):
    x = x_ref[...]
    h = (_rms(x) * an_ref[...]).astype(BF16)
    cq = _dot(h, wcq_ref[...])
    cqn = (_rms(cq) * qan_ref[...]).astype(BF16)
    qall = _dot(cqn, wuq_ref[...])
    ckvr = _dot(h, wckv_ref[...])
    ckvn = (_rms(ckvr[:, :KV_LORA]) * kvan_ref[...]).astype(BF16)
    kv = _dot(ckvn, wukv_ref[...])
    v = kv[:, MLA_H * NOPE_DIM:]
    if transpose_v:
        vt = v.T.astype(BF16)
        ones = jnp.ones((ONES_ROWS, v.shape[0]), BF16)
        for hd in range(MLA_H):
            v_out[hd * VT_ROWS:hd * VT_ROWS + V_DIM, :] = vt[hd * V_DIM:(hd + 1) * V_DIM, :]
            v_out[hd * VT_ROWS + V_DIM:(hd + 1) * VT_ROWS, :] = ones
    else:
        v_out[...] = v.astype(BF16)

    cos = cos_ref[...]
    sin = sin_ref[...]
    lane = lax.broadcasted_iota(jnp.int32, (1, LANES), 1)
    ka = ckvr[:, KV_LORA:KV_LORA + LANES]
    kb = ckvr[:, KV_LORA + LANES:KV_LORA + 2 * LANES]
    head_lanes = [((lane >= p * ROPE_DIM) & (lane < (p + 1) * ROPE_DIM)).astype(F32) for p in range(2)]
    ssq_kr = jnp.sum(ka * ka * head_lanes[0], axis=-1, keepdims=True)
    k_rot = [(ka * gk_ref[1 + p:2 + p, :]) * cos + (kb * gk_ref[3 + p:4 + p, :]) * sin for p in range(2)]
    q_scale = QK_DIM ** -0.5 * LOG2E
    rope0 = MLA_H * NOPE_DIM
    for hd in range(MLA_H):
        j, p = divmod(hd, 2)
        qn = qall[:, hd * NOPE_DIM:(hd + 1) * NOPE_DIM]
        qa = qall[:, rope0 + j * LANES:rope0 + (j + 1) * LANES]
        qb = qall[:, rope0 + 4 * LANES + j * LANES:rope0 + 4 * LANES + (j + 1) * LANES]
        ssq = (jnp.sum(qn * qn, axis=-1, keepdims=True)
               + jnp.sum(qa * qa * head_lanes[p], axis=-1, keepdims=True))
        rs = lax.rsqrt(ssq * (1.0 / QK_DIM) + NORM_EPS) * q_scale
        q_rot = (qa * gq_ref[1 + p:2 + p, :]) * cos + (qb * gq_ref[3 + p:4 + p, :]) * sin
        q_out[:, hd * QK_PAD:hd * QK_PAD + LANES] = (qn * gq_ref[0:1, :] * rs).astype(BF16)
        q_out[:, hd * QK_PAD + LANES:(hd + 1) * QK_PAD] = (q_rot * rs).astype(BF16)
        kn = kv[:, hd * NOPE_DIM:(hd + 1) * NOPE_DIM]
        rsk = lax.rsqrt((jnp.sum(kn * kn, axis=-1, keepdims=True) + ssq_kr) * (1.0 / QK_DIM) + NORM_EPS)
        k_out[:, hd * QK_PAD:hd * QK_PAD + LANES] = (kn * gk_ref[0:1, :] * rsk).astype(BF16)
        k_out[:, hd * QK_PAD + LANES:(hd + 1) * QK_PAD] = (k_rot[p] * rsk).astype(BF16)


def _mla_proj(x2d, w, cos_t, sin_t, seq_len, tm, transpose_v):
    n = x2d.shape[0]
    nt = seq_len // tm
    row = lambda i: (i, 0)
    pos = lambda i: (i % nt, 0)
    v_dims = MLA_H * V_DIM
    vt_rows = MLA_H * VT_ROWS
    return pl.pallas_call(
        functools.partial(_mla_proj_kernel, transpose_v=transpose_v),
        grid=(n // tm,),
        in_specs=[
            pl.BlockSpec((tm, D_MODEL), row),
            _resident((1, D_MODEL)),
            _resident(w["w_cq"].shape),
            _resident(w["w_ckvr"].shape),
            _resident((1, Q_LORA)),
            _resident(w["w_uq"].shape),
            _resident((1, KV_LORA)),
            _resident(w["w_ukv"].shape),
            _resident((8, LANES)),
            _resident((8, LANES)),
            pl.BlockSpec((tm, LANES), pos),
            pl.BlockSpec((tm, LANES), pos),
        ],
        out_specs=[
            pl.BlockSpec((tm, MLA_H * QK_PAD), row),
            pl.BlockSpec((tm, MLA_H * QK_PAD), row),
            pl.BlockSpec((vt_rows, tm), lambda i: (0, i)) if transpose_v else pl.BlockSpec((tm, v_dims), row),
        ],
        out_shape=[
            jax.ShapeDtypeStruct((n, MLA_H * QK_PAD), BF16),
            jax.ShapeDtypeStruct((n, MLA_H * QK_PAD), BF16),
            jax.ShapeDtypeStruct((vt_rows, n) if transpose_v else (n, v_dims), BF16),
        ],
        compiler_params=_params("parallel"),
        name="mla_proj",
    )(x2d, w["attn_norm"], w["w_cq"], w["w_ckvr"], w["q_a_norm"], w["w_uq"], w["kv_a_norm"],
      w["w_ukv"], w["g_q"], w["g_k"], cos_t, sin_t)


G_Q0, G_K0, G_V0, G_G0, G_A0, G_S0, G_END = 0, 512, 1024, 2048, 3072, 3200, 5248


def _gla_proj_kernel(x_ref, an_ref, wg_ref, wa2_ref, ba2_ref,
                     q_out, k_out, v_out, og_out, lg_out, gate_out):
    x = x_ref[...]
    h = (_rms(x) * an_ref[...]).astype(BF16)
    q_out[...] = _dot(h, wg_ref[:, G_Q0:G_K0]) * (GLA_DK ** -0.5)
    k_out[...] = _dot(h, wg_ref[:, G_K0:G_V0])
    v_out[...] = _dot(h, wg_ref[:, G_V0:G_G0]).astype(BF16)
    g = _dot(h, wg_ref[:, G_G0:G_A0])
    og_out[...] = g * jax.nn.sigmoid(g)
    a = _dot(h, wg_ref[:, G_A0:G_S0]).astype(BF16)
    pre = _dot(a, wa2_ref[...]) + ba2_ref[...]
    lg_out[...] = jax.nn.log_sigmoid(pre) * (1.0 / GATE_TEMP)
    gate_out[...] = jax.nn.sigmoid(_dot(h, wg_ref[:, G_S0:G_END]))


def _gla_proj(x2d, w, tm):
    n = x2d.shape[0]
    row = lambda i: (i, 0)
    widths = (GLA_H * GLA_DK, GLA_H * GLA_DK, GLA_H * GLA_DV, GLA_H * GLA_DV, 2 * GLA_H * GLA_DK, 2 * D_MODEL)
    dtypes = (F32, F32, BF16, F32, F32, F32)
    return pl.pallas_call(
        _gla_proj_kernel,
        grid=(n // tm,),
        in_specs=[
            pl.BlockSpec((tm, D_MODEL), row),
            _resident((1, D_MODEL)),
            _resident(w["w_g"].shape),
            _resident(w["w_a2"].shape),
            _resident((1, 2 * GLA_H * GLA_DK)),
        ],
        out_specs=[pl.BlockSpec((tm, wd), row) for wd in widths],
        out_shape=[jax.ShapeDtypeStruct((n, wd), dt) for wd, dt in zip(widths, dtypes)],
        compiler_params=_params("parallel"),
        name="gla_proj",
    )(x2d, w["attn_norm"], w["w_g"], w["w_a2"], w["b_a2"])


def _attn_kernel(q_ref, k_ref, vt_ref, km_ref, vmt_ref, o_ref, s_scr, *, tk, n_pairs):
    q = q_ref[0]
    s0 = _dot_nt(km_ref[...], q)
    m0 = jnp.max(s0, axis=0, keepdims=True)
    acc0 = _dot(vmt_ref[...], jnp.exp2(s0 - m0).astype(BF16))

    def scores(i, slot):
        off = pl.multiple_of(i * tk, tk)
        s = _dot_nt(k_ref[0, pl.ds(off, tk), :], q)
        s_scr[slot] = s
        return jnp.max(s, axis=0, keepdims=True)

    def absorb(i, slot, s_max, carry):
        m, acc = carry
        off = pl.multiple_of(i * tk, tk)
        m_new = jnp.maximum(m, s_max)
        p = jnp.exp2(s_scr[slot] - m_new).astype(BF16)
        acc = jnp.exp2(m - m_new) * acc + _dot(vt_ref[:, pl.ds(off, tk)], p)
        return m_new, acc

    def pair(j, carry, last):
        max0, stats = carry[0], carry[1:]
        max1 = scores(2 * j + 1, 1)
        stats = absorb(2 * j, 0, max0, stats)
        max0 = max1 if last else scores(2 * j + 2, 0)
        stats = absorb(2 * j + 1, 1, max1, stats)
        return (max0,) + stats

    carry = (scores(0, 0), m0, acc0)
    carry = lax.fori_loop(0, n_pairs - 1, functools.partial(pair, last=False), carry)
    _, _, acc = pair(n_pairs - 1, carry, True)
    o_ref[0] = (acc[:V_DIM] / acc[V_DIM:V_DIM + 1]).T.astype(o_ref.dtype)


def _attention(q, k, vt, k_meta, vt_meta, tq, tk):
    b, seq, _ = q.shape
    assert seq % (2 * tk) == 0, (seq, tk)
    return pl.pallas_call(
        functools.partial(_attn_kernel, tk=tk, n_pairs=seq // (2 * tk)),
        grid=(b, MLA_H, seq // tq),
        in_specs=[
            pl.BlockSpec((1, tq, QK_PAD), lambda bi, hi, qi: (bi, qi, hi)),
            pl.BlockSpec((1, seq, QK_PAD), lambda bi, hi, qi: (bi, 0, hi)),
            pl.BlockSpec((VT_ROWS, seq), lambda bi, hi, qi: (hi, bi)),
            pl.BlockSpec((N_META, QK_PAD), lambda bi, hi, qi: (0, hi)),
            pl.BlockSpec((VT_ROWS, N_META), lambda bi, hi, qi: (hi, 0)),
        ],
        out_specs=pl.BlockSpec((1, tq, V_DIM), lambda bi, hi, qi: (bi, qi, hi)),
        out_shape=jax.ShapeDtypeStruct((b, seq, MLA_H * V_DIM), BF16),
        scratch_shapes=[pltpu.VMEM((2, tk, tq), F32)],
        compiler_params=_params("parallel", "parallel", "arbitrary"),
        name="attention",
    )(q, k, vt, k_meta, vt_meta)


def _gla_scan_kernel(q_ref, k_ref, v_ref, lg_ref, s0_ref, o_ref, sfin_ref, state, *, reverse, n_chunks):
    blk = pl.program_id(1)

    @pl.when(blk == 0)
    def _():
        state[...] = s0_ref[...]

    ri = lax.broadcasted_iota(jnp.int32, (GLA_CHUNK, GLA_CHUNK), 0)
    ci = lax.broadcasted_iota(jnp.int32, (GLA_CHUNK, GLA_CHUNK), 1)
    keep = (ri <= ci) if reverse else (ri >= ci)
    csum = jnp.where(keep, 1.0, 0.0).astype(BF16)
    mid = GLA_CHUNK // 2 if reverse else GLA_CHUNK // 2 - 1
    end = 0 if reverse else GLA_CHUNK - 1

    def chunk(c, carry):
        cc = (n_chunks - 1 - c) if reverse else c
        r0 = pl.multiple_of(cc * GLA_CHUNK, GLA_CHUNK)
        rows = pl.ds(r0, GLA_CHUNK)
        for hd in range(GLA_H):
            kcols = slice(hd * GLA_DK, (hd + 1) * GLA_DK)
            vcols = slice(hd * GLA_DV, (hd + 1) * GLA_DV)
            lg = lg_ref[0, rows, kcols]
            lg_hi = lg.astype(BF16)
            lg_lo = (lg - lg_hi.astype(F32)).astype(BF16)
            bcum = _dot(csum, lg_hi) + _dot(csum, lg_lo)
            b_mid = bcum[mid:mid + 1, :]
            b_end = bcum[end:end + 1, :]
            q = q_ref[0, rows, kcols]
            k = k_ref[0, rows, kcols]
            v = v_ref[0, rows, vcols]
            qs = (q * jnp.exp(bcum - b_mid)).astype(BF16)
            ks = (k * jnp.exp(b_mid - bcum)).astype(BF16)
            a = jnp.where(keep, _dot_nt(qs, ks), 0.0).astype(BF16)
            st = state[hd]
            qe = (q * jnp.exp(bcum)).astype(BF16)
            o = _dot(a, v) + _dot_nt(qe, st.astype(BF16))
            o_ref[0, rows, vcols] = o
            kd = (k * jnp.exp(b_end - bcum)).astype(BF16)
            state[hd] = st * jnp.exp(b_end) + _dot_tn(v, kd)
        return carry

    lax.fori_loop(0, n_chunks, chunk, 0)

    @pl.when(blk == pl.num_programs(1) - 1)
    def _():
        sfin_ref[0] = state[...]


def _gla_scan(q, k, v, lg, s0, reverse, t_blk):
    b, t, _ = q.shape
    nb = t // t_blk
    lg_col = 1 if reverse else 0
    blk = (lambda bi, i: (bi, nb - 1 - i, 0)) if reverse else (lambda bi, i: (bi, i, 0))
    lg_blk = (lambda bi, i: (bi, nb - 1 - i, lg_col)) if reverse else (lambda bi, i: (bi, i, lg_col))
    st_shape = (GLA_H, GLA_DV, GLA_DK)
    return pl.pallas_call(
        functools.partial(_gla_scan_kernel, reverse=reverse, n_chunks=t_blk // GLA_CHUNK),
        grid=(b, nb),
        in_specs=[
            pl.BlockSpec((1, t_blk, GLA_H * GLA_DK), blk),
            pl.BlockSpec((1, t_blk, GLA_H * GLA_DK), blk),
            pl.BlockSpec((1, t_blk, GLA_H * GLA_DV), blk),
            pl.BlockSpec((1, t_blk, GLA_H * GLA_DK), lg_blk),
            pl.BlockSpec(st_shape, lambda bi, i: (0, 0, 0)),
        ],
        out_specs=[
            pl.BlockSpec((1, t_blk, GLA_H * GLA_DV), blk),
            pl.BlockSpec((1,) + st_shape, lambda bi, i: (bi, 0, 0, 0)),
        ],
        out_shape=[
            jax.ShapeDtypeStruct((b, t, GLA_H * GLA_DV), F32),
            jax.ShapeDtypeStruct((b,) + st_shape, F32),
        ],
        scratch_shapes=[pltpu.VMEM(st_shape, F32)],
        compiler_params=_params("parallel", "arbitrary"),
        name="gla_scan_bwd" if reverse else "gla_scan_fwd",
    )(q, k, v, lg, s0)


def _mix_ffn_kernel(x_ref, oa_ref, of_ref, ob_ref, og_ref, gate_ref, womla_ref, onorm_ref, wogla_ref,
                    wout_ref, fn_ref, wgate_ref, wup_ref, wdown_ref, y_ref):
    y_a = _dot(oa_ref[...], womla_ref[...])
    o = of_ref[...] + ob_ref[...]
    og = og_ref[...]
    parts = []
    for hd in range(GLA_H):
        cols = slice(hd * GLA_DV, (hd + 1) * GLA_DV)
        parts.append((_rms(o[:, cols]) * onorm_ref[...] * og[:, cols]).astype(BF16))
    y_b = _dot(jnp.concatenate(parts, axis=-1), wogla_ref[...])
    gates = gate_ref[...]
    mixed = (gates[:, :D_MODEL] * y_a + gates[:, D_MODEL:] * y_b).astype(BF16)
    x1 = x_ref[...] + _dot(mixed, wout_ref[...])
    h = (_rms(x1) * fn_ref[...]).astype(BF16)
    g = _dot(h, wgate_ref[...])
    u = _dot(h, wup_ref[...])
    act = (g * jax.nn.sigmoid(g) * u).astype(BF16)
    y_ref[...] = x1 + _dot(act, wdown_ref[...])


def _mix_ffn(x2d, o_attn, o_f, o_b, og, gates, w, tm):
    n = x2d.shape[0]
    row = lambda i: (i, 0)
    return pl.pallas_call(
        _mix_ffn_kernel,
        grid=(n // tm,),
        in_specs=[
            pl.BlockSpec((tm, D_MODEL), row),
            pl.BlockSpec((tm, MLA_H * V_DIM), row),
            pl.BlockSpec((tm, GLA_H * GLA_DV), row),
            pl.BlockSpec((tm, GLA_H * GLA_DV), row),
            pl.BlockSpec((tm, GLA_H * GLA_DV), row),
            pl.BlockSpec((tm, 2 * D_MODEL), row),
            _resident(w["w_o_mla"].shape),
            _resident((1, GLA_DV)),
            _resident(w["w_o_gla"].shape),
            _resident(w["w_out"].shape),
            _resident((1, D_MODEL)),
            _resident(w["w_ffn_gate"].shape),
            _resident(w["w_ffn_up"].shape),
            _resident(w["w_ffn_down"].shape),
        ],
        out_specs=pl.BlockSpec((tm, D_MODEL), row),
        out_shape=jax.ShapeDtypeStruct((n, D_MODEL), F32),
        compiler_params=_params("parallel"),
        name="mix_ffn",
    )(x2d, o_attn, o_f, o_b, og, gates, w["w_o_mla"], w["gla_o_norm"], w["w_o_gla"], w["w_out"],
      w["ffn_norm"], w["w_ffn_gate"], w["w_ffn_up"], w["w_ffn_down"])


def _prep_weights(attn_norm, w_in, q_a_norm, w_uq, kv_a_norm, w_ukv, q_norm, k_norm, w_o_mla,
                  w_a2_fwd, b_a2_fwd, w_a2_bwd, b_a2_bwd, gla_o_norm, w_o_gla, w_out, ffn_norm,
                  w_ffn_gate, w_ffn_up, w_ffn_down):
    half = ROPE_DIM // 2
    o = 0
    cols = {}
    for name, size in (("cq", Q_LORA), ("ckv", KV_LORA), ("kr", ROPE_DIM), ("gq", GLA_H * GLA_DK),
                       ("gk", GLA_H * GLA_DK), ("gv", GLA_H * GLA_DV), ("gg", GLA_H * GLA_DV),
                       ("af", GATE_RANK), ("ab", GATE_RANK), ("ga", D_MODEL), ("gb", D_MODEL)):
        cols[name] = w_in[:, o:o + size]
        o += size
    kr1, kr2 = cols["kr"][:, :half], cols["kr"][:, half:]
    w_ckvr = jnp.concatenate([cols["ckv"], kr1, kr2, kr1, kr2, kr2, kr1, kr2, kr1], axis=1)

    uq = w_uq.reshape(Q_LORA, MLA_H, QK_DIM)
    uq_nope = uq[:, :, :NOPE_DIM].reshape(Q_LORA, MLA_H * NOPE_DIM)
    uq_r = uq[:, :, NOPE_DIM:]
    uq_a = uq_r.reshape(Q_LORA, MLA_H * ROPE_DIM)
    uq_b = jnp.concatenate([uq_r[:, :, half:], uq_r[:, :, :half]], axis=-1).reshape(Q_LORA, MLA_H * ROPE_DIM)
    ukv = w_ukv.reshape(KV_LORA, MLA_H, NOPE_DIM + V_DIM)
    w_ukv2 = jnp.concatenate([ukv[:, :, :NOPE_DIM].reshape(KV_LORA, MLA_H * NOPE_DIM),
                              ukv[:, :, NOPE_DIM:].reshape(KV_LORA, MLA_H * V_DIM)], axis=1)

    def gain_rows(g):
        z = jnp.zeros((ROPE_DIM,), F32)
        ga = g[NOPE_DIM:]
        gb = jnp.concatenate([g[NOPE_DIM + half:], g[NOPE_DIM:NOPE_DIM + half]])
        rows = [g[:NOPE_DIM], jnp.concatenate([ga, z]), jnp.concatenate([z, ga]),
                jnp.concatenate([gb, z]), jnp.concatenate([z, gb])]
        rows += [jnp.zeros((LANES,), F32)] * 3
        return jnp.stack(rows)

    zpad = jnp.zeros((D_MODEL, LANES - 2 * GATE_RANK), F32)
    w_g = jnp.concatenate([cols["gq"], cols["gk"], cols["gv"], cols["gg"], cols["af"], cols["ab"], zpad,
                           cols["ga"], cols["gb"]], axis=1)
    nk = GLA_H * GLA_DK
    w_a2 = jnp.zeros((LANES, 2 * nk), F32)
    w_a2 = w_a2.at[:GATE_RANK, :nk].set(w_a2_fwd).at[GATE_RANK:2 * GATE_RANK, nk:].set(w_a2_bwd)
    return {
        "attn_norm": attn_norm[None], "w_cq": cols["cq"].astype(BF16), "w_ckvr": w_ckvr.astype(BF16),
        "q_a_norm": q_a_norm[None], "w_uq": jnp.concatenate([uq_nope, uq_a, uq_b], axis=1).astype(BF16),
        "kv_a_norm": kv_a_norm[None], "w_ukv": w_ukv2.astype(BF16),
        "g_q": gain_rows(q_norm), "g_k": gain_rows(k_norm),
        "w_g": w_g.astype(BF16), "w_a2": w_a2.astype(BF16),
        "b_a2": jnp.concatenate([b_a2_fwd, b_a2_bwd])[None],
        "w_o_mla": w_o_mla.astype(BF16), "gla_o_norm": gla_o_norm[None], "w_o_gla": w_o_gla.astype(BF16),
        "w_out": w_out.astype(BF16), "ffn_norm": ffn_norm[None], "w_ffn_gate": w_ffn_gate.astype(BF16),
        "w_ffn_up": w_ffn_up.astype(BF16), "w_ffn_down": w_ffn_down.astype(BF16),
    }


def _rope_tables(length):
    inv = 1.0 / (ROPE_THETA ** (jnp.arange(0, ROPE_DIM, 2, dtype=F32) / ROPE_DIM))
    ang = jnp.arange(length, dtype=F32)[:, None] * inv[None, :]
    cos, sin = jnp.cos(ang), jnp.sin(ang)
    return jnp.tile(cos, (1, 4)), jnp.concatenate([-sin, sin, -sin, sin], axis=1)


def _encode_group(x, w, meta, cos_t, sin_t):
    b, seq, _ = x.shape
    n = b * seq
    x2d = x.reshape(n, D_MODEL)
    tm = _row_tile(seq, 256)
    q, k, vt = _mla_proj(x2d, w, cos_t[N_META:N_META + seq], sin_t[N_META:N_META + seq], seq, tm, True)
    gq, gk, gv, og, lg, gates = _gla_proj(x2d, w, tm)

    o_attn = _attention(q.reshape(b, seq, -1), k.reshape(b, seq, -1), vt,
                        meta["k"], meta["vt"], _row_tile(seq, 512), _row_tile(seq // 2, 1024))

    t_blk = _row_tile(seq, 512)
    g3 = lambda a: a.reshape(b, seq, -1)
    o_f, _ = _gla_scan(g3(gq), g3(gk), g3(gv), g3(lg), meta["state"], False, t_blk)
    o_b, _ = _gla_scan(g3(gq), g3(gk), g3(gv), g3(lg), jnp.zeros_like(meta["state"]), True, t_blk)

    y = _mix_ffn(x2d, o_attn.reshape(n, -1), o_f.reshape(n, -1), o_b.reshape(n, -1), og, gates, w, tm)
    return y.reshape(b, seq, D_MODEL)


def kernel(x_prompt, x_sample, meta_tokens, attn_norm, w_in, q_a_norm, w_uq, kv_a_norm, w_ukv, q_norm, k_norm, w_o_mla, w_a2_fwd, b_a2_fwd, w_a2_bwd, b_a2_bwd, gla_o_norm, w_o_gla, w_out, ffn_norm, w_ffn_gate, w_ffn_up, w_ffn_down):
    assert attn_norm.shape[0] == 1, "single-layer encoder"
    w = _prep_weights(attn_norm[0], w_in[0], q_a_norm[0], w_uq[0], kv_a_norm[0], w_ukv[0], q_norm[0],
                      k_norm[0], w_o_mla[0], w_a2_fwd[0], b_a2_fwd[0], w_a2_bwd[0], b_a2_bwd[0],
                      gla_o_norm[0], w_o_gla[0], w_out[0], ffn_norm[0], w_ffn_gate[0], w_ffn_up[0],
                      w_ffn_down[0])
    max_len = N_META + max(x_prompt.shape[1], x_sample.shape[1])
    cos_t, sin_t = _rope_tables(max_len)

    xm = meta_tokens.astype(F32)
    _, k_m, v_m = _mla_proj(xm, w, cos_t[:N_META], sin_t[:N_META], N_META, N_META, False)
    mq, mk, mv, _, mlg, _ = _gla_proj(xm, w, N_META)
    pad = lambda a: jnp.pad(a, ((GLA_CHUNK - N_META, 0), (0, 0)))[None]
    _, s_meta = _gla_scan(pad(mq), pad(mk), pad(mv), pad(mlg),
                          jnp.zeros((GLA_H, GLA_DV, GLA_DK), F32), False, GLA_CHUNK)
    vt_m = jnp.concatenate([v_m.T.reshape(MLA_H, V_DIM, N_META), jnp.ones((MLA_H, ONES_ROWS, N_META), BF16)],
                           axis=1).reshape(MLA_H * VT_ROWS, N_META)
    meta = {"k": k_m, "vt": vt_m, "state": s_meta[0]}

    return (_encode_group(x_prompt, w, meta, cos_t, sin_t),
            _encode_group(x_sample, w, meta, cos_t, sin_t))
```

```python
import functools
import math

import jax
import jax.numpy as jnp
from jax import lax
from jax.experimental import pallas as pl
from jax.experimental.pallas import tpu as pltpu

D_MODEL = 1024
N_META = 16
MLA_H = 8
NOPE_DIM = 128
ROPE_DIM = 64
QK_DIM = NOPE_DIM + ROPE_DIM
V_DIM = 128
Q_LORA = 768
KV_LORA = 256
ROPE_THETA = 10000.0
GLA_H = 4
GLA_DK = 128
GLA_DV = 256
GATE_RANK = 16
GATE_TEMP = 16.0
GLA_CHUNK = 64
D_FF = 2816
NORM_EPS = 1e-6

QK_PAD = 256
ONES_ROWS = 16
VT_ROWS = V_DIM + ONES_ROWS
LANES = 128
LOG2E = 1.4426950408889634
VMEM_LIMIT = 56 * 1024 * 1024

F32 = jnp.float32
BF16 = jnp.bfloat16


def _dot(a, b):
    return jnp.dot(a, b, preferred_element_type=F32)


def _dot_nt(a, b):
    return lax.dot_general(a, b, (((1,), (1,)), ((), ())), preferred_element_type=F32)


def _dot_tn(a, b):
    return lax.dot_general(a, b, (((0,), (0,)), ((), ())), preferred_element_type=F32)


def _rms(x):
    return x * lax.rsqrt(jnp.mean(x * x, axis=-1, keepdims=True) + NORM_EPS)


def _resident(shape):
    return pl.BlockSpec(shape, lambda *_: (0,) * len(shape), pipeline_mode=pl.Buffered(1))


def _params(*sem):
    return pltpu.CompilerParams(dimension_semantics=sem, vmem_limit_bytes=VMEM_LIMIT)


def _row_tile(n, want):
    t = min(n, want)
    assert n % t == 0, (n, t)
    return t


def _mla_proj_kernel(x_ref, an_ref, wcq_ref, wckv_ref, qan_ref, wuq_ref, kvan_ref, wukv_ref,
                     gq_ref, gk_ref, cos_ref, sin_ref, q_out, k_out, v_out, *, transpose_v):
    x = x_ref[...]
    h = (_rms(x) * an_ref[...]).astype(BF16)
    cq = _dot(h, wcq_ref[...])
    cqn = (_rms(cq) * qan_ref[...]).astype(BF16)
    qall = _dot(cqn, wuq_ref[...])
    ckvr = _dot(h, wckv_ref[...])
    ckvn = (_rms(ckvr[:, :KV_LORA]) * kvan_ref[...]).astype(BF16)
    kv = _dot(ckvn, wukv_ref[...])
    v = kv[:, MLA_H * NOPE_DIM:]
    if transpose_v:
        vt = v.T.astype(BF16)
        ones = jnp.ones((ONES_ROWS, v.shape[0]), BF16)
        for hd in range(MLA_H):
            v_out[hd * VT_ROWS:hd * VT_ROWS + V_DIM, :] = vt[hd * V_DIM:(hd + 1) * V_DIM, :]
            v_out[hd * VT_ROWS + V_DIM:(hd + 1) * VT_ROWS, :] = ones
    else:
        v_out[...] = v.astype(BF16)

    cos = cos_ref[...]
    sin = sin_ref[...]
    lane = lax.broadcasted_iota(jnp.int32, (1, LANES), 1)
    ka = ckvr[:, KV_LORA:KV_LORA + LANES]
    kb = ckvr[:, KV_LORA + LANES:KV_LORA + 2 * LANES]
    head_lanes = [((lane >= p * ROPE_DIM) & (lane < (p + 1) * ROPE_DIM)).astype(F32) for p in range(2)]
    ssq_kr = jnp.sum(ka * ka * head_lanes[0], axis=-1, keepdims=True)
    k_rot = [(ka * gk_ref[1 + p:2 + p, :]) * cos + (kb * gk_ref[3 + p:4 + p, :]) * sin for p in range(2)]
    q_scale = QK_DIM ** -0.5 * LOG2E
    rope0 = MLA_H * NOPE_DIM
    for hd in range(MLA_H):
        j, p = divmod(hd, 2)
        qn = qall[:, hd * NOPE_DIM:(hd + 1) * NOPE_DIM]
        qa = qall[:, rope0 + j * LANES:rope0 + (j + 1) * LANES]
        qb = qall[:, rope0 + 4 * LANES + j * LANES:rope0 + 4 * LANES + (j + 1) * LANES]
        ssq = (jnp.sum(qn * qn, axis=-1, keepdims=True)
               + jnp.sum(qa * qa * head_lanes[p], axis=-1, keepdims=True))
        rs = lax.rsqrt(ssq * (1.0 / QK_DIM) + NORM_EPS) * q_scale
        q_rot = (qa * gq_ref[1 + p:2 + p, :]) * cos + (qb * gq_ref[3 + p:4 + p, :]) * sin
        q_out[:, hd * QK_PAD:hd * QK_PAD + LANES] = (qn * gq_ref[0:1, :] * rs).astype(BF16)
        q_out[:, hd * QK_PAD + LANES:(hd + 1) * QK_PAD] = (q_rot * rs).astype(BF16)
        kn = kv[:, hd * NOPE_DIM:(hd + 1) * NOPE_DIM]
        rsk = lax.rsqrt((jnp.sum(kn * kn, axis=-1, keepdims=True) + ssq_kr) * (1.0 / QK_DIM) + NORM_EPS)
        k_out[:, hd * QK_PAD:hd * QK_PAD + LANES] = (kn * gk_ref[0:1, :] * rsk).astype(BF16)
        k_out[:, hd * QK_PAD + LANES:(hd + 1) * QK_PAD] = (k_rot[p] * rsk).astype(BF16)


def _mla_proj(x2d, w, cos_t, sin_t, seq_len, tm, transpose_v):
    n = x2d.shape[0]
    nt = seq_len // tm
    row = lambda i: (i, 0)
    pos = lambda i: (i % nt, 0)
    v_dims = MLA_H * V_DIM
    vt_rows = MLA_H * VT_ROWS
    return pl.pallas_call(
        functools.partial(_mla_proj_kernel, transpose_v=transpose_v),
        grid=(n // tm,),
        in_specs=[
            pl.BlockSpec((tm, D_MODEL), row),
            _resident((1, D_MODEL)),
            _resident(w["w_cq"].shape),
            _resident(w["w_ckvr"].shape),
            _resident((1, Q_LORA)),
            _resident(w["w_uq"].shape),
            _resident((1, KV_LORA)),
            _resident(w["w_ukv"].shape),
            _resident((8, LANES)),
            _resident((8, LANES)),
            pl.BlockSpec((tm, LANES), pos),
            pl.BlockSpec((tm, LANES), pos),
        ],
        out_specs=[
            pl.BlockSpec((tm, MLA_H * QK_PAD), row),
            pl.BlockSpec((tm, MLA_H * QK_PAD), row),
            pl.BlockSpec((vt_rows, tm), lambda i: (0, i)) if transpose_v else pl.BlockSpec((tm, v_dims), row),
        ],
        out_shape=[
            jax.ShapeDtypeStruct((n, MLA_H * QK_PAD), BF16),
            jax.ShapeDtypeStruct((n, MLA_H * QK_PAD), BF16),
            jax.ShapeDtypeStruct((vt_rows, n) if transpose_v else (n, v_dims), BF16),
        ],
        compiler_params=_params("parallel"),
        name="mla_proj",
    )(x2d, w["attn_norm"], w["w_cq"], w["w_ckvr"], w["q_a_norm"], w["w_uq"], w["kv_a_norm"],
      w["w_ukv"], w["g_q"], w["g_k"], cos_t, sin_t)


G_Q0, G_K0, G_V0, G_G0, G_A0, G_S0, G_END = 0, 512, 1024, 2048, 3072, 3200, 5248


def _gla_proj_kernel(x_ref, an_ref, wg_ref, wa2_ref, ba2_ref,
                     q_out, k_out, v_out, og_out, lg_out, gate_out):
    x = x_ref[...]
    h = (_rms(x) * an_ref[...]).astype(BF16)
    q_out[...] = _dot(h, wg_ref[:, G_Q0:G_K0]) * (GLA_DK ** -0.5)
    k_out[...] = _dot(h, wg_ref[:, G_K0:G_V0])
    v_out[...] = _dot(h, wg_ref[:, G_V0:G_G0]).astype(BF16)
    g = _dot(h, wg_ref[:, G_G0:G_A0])
    og_out[...] = g * jax.nn.sigmoid(g)
    a = _dot(h, wg_ref[:, G_A0:G_S0]).astype(BF16)
    pre = _dot(a, wa2_ref[...]) + ba2_ref[...]
    lg_out[...] = jax.nn.log_sigmoid(pre) * (1.0 / GATE_TEMP)
    gate_out[...] = jax.nn.sigmoid(_dot(h, wg_ref[:, G_S0:G_END]))


def _gla_proj(x2d, w, tm):
    n = x2d.shape[0]
    row = lambda i: (i, 0)
    widths = (GLA_H * GLA_DK, GLA_H * GLA_DK, GLA_H * GLA_DV, GLA_H * GLA_DV, 2 * GLA_H * GLA_DK, 2 * D_MODEL)
    dtypes = (F32, F32, BF16, F32, F32, F32)
    return pl.pallas_call(
        _gla_proj_kernel,
        grid=(n // tm,),
        in_specs=[
            pl.BlockSpec((tm, D_MODEL), row),
            _resident((1, D_MODEL)),
            _resident(w["w_g"].shape),
            _resident(w["w_a2"].shape),
            _resident((1, 2 * GLA_H * GLA_DK)),
        ],
        out_specs=[pl.BlockSpec((tm, wd), row) for wd in widths],
        out_shape=[jax.ShapeDtypeStruct((n, wd), dt) for wd, dt in zip(widths, dtypes)],
        compiler_params=_params("parallel"),
        name="gla_proj",
    )(x2d, w["attn_norm"], w["w_g"], w["w_a2"], w["b_a2"])


def _attn_kernel(q_ref, k_ref, vt_ref, km_ref, vmt_ref, o_ref, s_scr, *, tk, n_pairs):
    q = q_ref[0]
    s0 = _dot_nt(km_ref[...], q)
    m0 = jnp.max(s0, axis=0, keepdims=True)
    acc0 = _dot(vmt_ref[...], jnp.exp2(s0 - m0).astype(BF16))

    def scores(i, slot):
        off = pl.multiple_of(i * tk, tk)
        s = _dot_nt(k_ref[0, pl.ds(off, tk), :], q)
        s_scr[slot] = s
        return jnp.max(s, axis=0, keepdims=True)

    def absorb(i, slot, s_max, carry):
        m, acc = carry
        off = pl.multiple_of(i * tk, tk)
        m_new = jnp.maximum(m, s_max)
        p = jnp.exp2(s_scr[slot] - m_new).astype(BF16)
        acc = jnp.exp2(m - m_new) * acc + _dot(vt_ref[:, pl.ds(off, tk)], p)
        return m_new, acc

    def pair(j, carry, last):
        max0, stats = carry[0], carry[1:]
        max1 = scores(2 * j + 1, 1)
        stats = absorb(2 * j, 0, max0, stats)
        max0 = max1 if last else scores(2 * j + 2, 0)
        stats = absorb(2 * j + 1, 1, max1, stats)
        return (max0,) + stats

    carry = (scores(0, 0), m0, acc0)
    carry = lax.fori_loop(0, n_pairs - 1, functools.partial(pair, last=False), carry)
    _, _, acc = pair(n_pairs - 1, carry, True)
    o_ref[0] = (acc[:V_DIM] / acc[V_DIM:V_DIM + 1]).T.astype(o_ref.dtype)


def _attention(q, k, vt, k_meta, vt_meta, tq, tk):
    b, seq, _ = q.shape
    assert seq % (2 * tk) == 0, (seq, tk)
    return pl.pallas_call(
        functools.partial(_attn_kernel, tk=tk, n_pairs=seq // (2 * tk)),
        grid=(b, MLA_H, seq // tq),
        in_specs=[
            pl.BlockSpec((1, tq, QK_PAD), lambda bi, hi, qi: (bi, qi, hi)),
            pl.BlockSpec((1, seq, QK_PAD), lambda bi, hi, qi: (bi, 0, hi)),
            pl.BlockSpec((VT_ROWS, seq), lambda bi, hi, qi: (hi, bi)),
            pl.BlockSpec((N_META, QK_PAD), lambda bi, hi, qi: (0, hi)),
            pl.BlockSpec((VT_ROWS, N_META), lambda bi, hi, qi: (hi, 0)),
        ],
        out_specs=pl.BlockSpec((1, tq, V_DIM), lambda bi, hi, qi: (bi, qi, hi)),
        out_shape=jax.ShapeDtypeStruct((b, seq, MLA_H * V_DIM), BF16),
        scratch_shapes=[pltpu.VMEM((2, tk, tq), F32)],
        compiler_params=_params("parallel", "parallel", "arbitrary"),
        name="attention",
    )(q, k, vt, k_meta, vt_meta)


def _gla_scan_kernel(q_ref, k_ref, v_ref, lg_ref, s0_ref, o_ref, sfin_ref, state, u_scr, dec_scr, qe_scr,
                     *, reverse, n_chunks):
    blk = pl.program_id(1)

    @pl.when(blk == 0)
    def _():
        state[...] = s0_ref[...]

    ri = lax.broadcasted_iota(jnp.int32, (GLA_CHUNK, GLA_CHUNK), 0)
    ci = lax.broadcasted_iota(jnp.int32, (GLA_CHUNK, GLA_CHUNK), 1)
    keep = (ri <= ci) if reverse else (ri >= ci)
    csum = jnp.where(keep, 1.0, 0.0).astype(BF16)
    csum2 = jnp.concatenate([csum, csum], axis=1)
    mid = GLA_CHUNK // 2 if reverse else GLA_CHUNK // 2 - 1
    end = 0 if reverse else GLA_CHUNK - 1
    group = math.gcd(n_chunks, 4)

    heads = [(slice(hd * GLA_DK, (hd + 1) * GLA_DK), slice(hd * GLA_DV, (hd + 1) * GLA_DV)) for hd in range(GLA_H)]

    def local(g, carry):
        cs = [g * group + j for j in range(group)]
        rows = [pl.ds(pl.multiple_of(c * GLA_CHUNK, GLA_CHUNK), GLA_CHUNK) for c in cs]
        bcums = []
        for r in rows:
            lg = lg_ref[0, r, :]
            lg_hi = lg.astype(BF16)
            lg_lo = (lg - lg_hi.astype(F32)).astype(BF16)
            bcums.append(_dot(csum2, jnp.concatenate([lg_hi, lg_lo], axis=0)))
        qss, kss, kds = [], [], []
        for c, r, bcum in zip(cs, rows, bcums):
            b_mid = bcum[mid:mid + 1, :]
            b_end = bcum[end:end + 1, :]
            q = q_ref[0, r, :]
            k = k_ref[0, r, :]
            qss.append((q * jnp.exp(bcum - b_mid)).astype(BF16))
            kss.append((k * jnp.exp(b_mid - bcum)).astype(BF16))
            kds.append((k * jnp.exp(b_end - bcum)).astype(BF16))
            qe_scr[c] = (q * jnp.exp(bcum)).astype(BF16)
            dec_scr[c] = jnp.broadcast_to(jnp.exp(b_end), (8, GLA_H * GLA_DK))
        attn = [[_dot_nt(qs[:, kc], ks[:, kc]) for kc, _ in heads] for qs, ks in zip(qss, kss)]
        for c, r, kd in zip(cs, rows, kds):
            for hd, (kc, vc) in enumerate(heads):
                u_scr[c, hd] = _dot_tn(v_ref[0, r, vc], kd[:, kc])
        for r, a_c in zip(rows, attn):
            for (_, vc), a in zip(heads, a_c):
                o_ref[0, r, vc] = _dot(jnp.where(keep, a, 0.0).astype(BF16), v_ref[0, r, vc])
        return carry

    lax.fori_loop(0, n_chunks // group, local, 0)

    def recur(i, carry):
        c = (n_chunks - 1 - i) if reverse else i
        rows = pl.ds(pl.multiple_of(c * GLA_CHUNK, GLA_CHUNK), GLA_CHUNK)
        for hd, (kc, vc) in enumerate(heads):
            st = state[hd]
            o_ref[0, rows, vc] += _dot_nt(qe_scr[c, :, kc], st.astype(BF16))
            state[hd] = st * dec_scr[c, 0:1, kc] + u_scr[c, hd]
        return carry

    lax.fori_loop(0, n_chunks, recur, 0, unroll=group)

    @pl.when(blk == pl.num_programs(1) - 1)
    def _():
        sfin_ref[0] = state[...]


def _gla_scan(q, k, v, lg, s0, reverse, t_blk):
    b, t, _ = q.shape
    nb = t // t_blk
    lg_col = 1 if reverse else 0
    blk = (lambda bi, i: (bi, nb - 1 - i, 0)) if reverse else (lambda bi, i: (bi, i, 0))
    lg_blk = (lambda bi, i: (bi, nb - 1 - i, lg_col)) if reverse else (lambda bi, i: (bi, i, lg_col))
    st_shape = (GLA_H, GLA_DV, GLA_DK)
    n_chunks = t_blk // GLA_CHUNK
    return pl.pallas_call(
        functools.partial(_gla_scan_kernel, reverse=reverse, n_chunks=n_chunks),
        grid=(b, nb),
        in_specs=[
            pl.BlockSpec((1, t_blk, GLA_H * GLA_DK), blk),
            pl.BlockSpec((1, t_blk, GLA_H * GLA_DK), blk),
            pl.BlockSpec((1, t_blk, GLA_H * GLA_DV), blk),
            pl.BlockSpec((1, t_blk, GLA_H * GLA_DK), lg_blk),
            pl.BlockSpec(st_shape, lambda bi, i: (0, 0, 0)),
        ],
        out_specs=[
            pl.BlockSpec((1, t_blk, GLA_H * GLA_DV), blk),
            pl.BlockSpec((1,) + st_shape, lambda bi, i: (bi, 0, 0, 0)),
        ],
        out_shape=[
            jax.ShapeDtypeStruct((b, t, GLA_H * GLA_DV), F32),
            jax.ShapeDtypeStruct((b,) + st_shape, F32),
        ],
        scratch_shapes=[
            pltpu.VMEM(st_shape, F32),
            pltpu.VMEM((n_chunks,) + st_shape, F32),
            pltpu.VMEM((n_chunks, 8, GLA_H * GLA_DK), F32),
            pltpu.VMEM((n_chunks, GLA_CHUNK, GLA_H * GLA_DK), BF16),
        ],
        compiler_params=_params("parallel", "arbitrary"),
        name="gla_scan_bwd" if reverse else "gla_scan_fwd",
    )(q, k, v, lg, s0)


def _mix_ffn_kernel(x_ref, oa_ref, of_ref, ob_ref, og_ref, gate_ref, womla_ref, onorm_ref, wogla_ref,
                    wout_ref, fn_ref, wgate_ref, wup_ref, wdown_ref, y_ref):
    y_a = _dot(oa_ref[...], womla_ref[...])
    o = of_ref[...] + ob_ref[...]
    og = og_ref[...]
    parts = []
    for hd in range(GLA_H):
        cols = slice(hd * GLA_DV, (hd + 1) * GLA_DV)
        parts.append((_rms(o[:, cols]) * onorm_ref[...] * og[:, cols]).astype(BF16))
    y_b = _dot(jnp.concatenate(parts, axis=-1), wogla_ref[...])
    gates = gate_ref[...]
    mixed = (gates[:, :D_MODEL] * y_a + gates[:, D_MODEL:] * y_b).astype(BF16)
    x1 = x_ref[...] + _dot(mixed, wout_ref[...])
    h = (_rms(x1) * fn_ref[...]).astype(BF16)
    g = _dot(h, wgate_ref[...])
    u = _dot(h, wup_ref[...])
    act = (g * jax.nn.sigmoid(g) * u).astype(BF16)
    y_ref[...] = x1 + _dot(act, wdown_ref[...])


def _mix_ffn(x2d, o_attn, o_f, o_b, og, gates, w, tm):
    n = x2d.shape[0]
    row = lambda i: (i, 0)
    return pl.pallas_call(
        _mix_ffn_kernel,
        grid=(n // tm,),
        in_specs=[
            pl.BlockSpec((tm, D_MODEL), row),
            pl.BlockSpec((tm, MLA_H * V_DIM), row),
            pl.BlockSpec((tm, GLA_H * GLA_DV), row),
            pl.BlockSpec((tm, GLA_H * GLA_DV), row),
            pl.BlockSpec((tm, GLA_H * GLA_DV), row),
            pl.BlockSpec((tm, 2 * D_MODEL), row),
            _resident(w["w_o_mla"].shape),
            _resident((1, GLA_DV)),
            _resident(w["w_o_gla"].shape),
            _resident(w["w_out"].shape),
            _resident((1, D_MODEL)),
            _resident(w["w_ffn_gate"].shape),
            _resident(w["w_ffn_up"].shape),
            _resident(w["w_ffn_down"].shape),
        ],
        out_specs=pl.BlockSpec((tm, D_MODEL), row),
        out_shape=jax.ShapeDtypeStruct((n, D_MODEL), F32),
        compiler_params=_params("parallel"),
        name="mix_ffn",
    )(x2d, o_attn, o_f, o_b, og, gates, w["w_o_mla"], w["gla_o_norm"], w["w_o_gla"], w["w_out"],
      w["ffn_norm"], w["w_ffn_gate"], w["w_ffn_up"], w["w_ffn_down"])


def _prep_weights(attn_norm, w_in, q_a_norm, w_uq, kv_a_norm, w_ukv, q_norm, k_norm, w_o_mla,
                  w_a2_fwd, b_a2_fwd, w_a2_bwd, b_a2_bwd, gla_o_norm, w_o_gla, w_out, ffn_norm,
                  w_ffn_gate, w_ffn_up, w_ffn_down):
    half = ROPE_DIM // 2
    o = 0
    cols = {}
    for name, size in (("cq", Q_LORA), ("ckv", KV_LORA), ("kr", ROPE_DIM), ("gq", GLA_H * GLA_DK),
                       ("gk", GLA_H * GLA_DK), ("gv", GLA_H * GLA_DV), ("gg", GLA_H * GLA_DV),
                       ("af", GATE_RANK), ("ab", GATE_RANK), ("ga", D_MODEL), ("gb", D_MODEL)):
        cols[name] = w_in[:, o:o + size]
        o += size
    kr1, kr2 = cols["kr"][:, :half], cols["kr"][:, half:]
    w_ckvr = jnp.concatenate([cols["ckv"], kr1, kr2, kr1, kr2, kr2, kr1, kr2, kr1], axis=1)

    uq = w_uq.reshape(Q_LORA, MLA_H, QK_DIM)
    uq_nope = uq[:, :, :NOPE_DIM].reshape(Q_LORA, MLA_H * NOPE_DIM)
    uq_r = uq[:, :, NOPE_DIM:]
    uq_a = uq_r.reshape(Q_LORA, MLA_H * ROPE_DIM)
    uq_b = jnp.concatenate([uq_r[:, :, half:], uq_r[:, :, :half]], axis=-1).reshape(Q_LORA, MLA_H * ROPE_DIM)
    ukv = w_ukv.reshape(KV_LORA, MLA_H, NOPE_DIM + V_DIM)
    w_ukv2 = jnp.concatenate([ukv[:, :, :NOPE_DIM].reshape(KV_LORA, MLA_H * NOPE_DIM),
                              ukv[:, :, NOPE_DIM:].reshape(KV_LORA, MLA_H * V_DIM)], axis=1)

    def gain_rows(g):
        z = jnp.zeros((ROPE_DIM,), F32)
        ga = g[NOPE_DIM:]
        gb = jnp.concatenate([g[NOPE_DIM + half:], g[NOPE_DIM:NOPE_DIM + half]])
        rows = [g[:NOPE_DIM], jnp.concatenate([ga, z]), jnp.concatenate([z, ga]),
                jnp.concatenate([gb, z]), jnp.concatenate([z, gb])]
        rows += [jnp.zeros((LANES,), F32)] * 3
        return jnp.stack(rows)

    zpad = jnp.zeros((D_MODEL, LANES - 2 * GATE_RANK), F32)
    w_g = jnp.concatenate([cols["gq"], cols["gk"], cols["gv"], cols["gg"], cols["af"], cols["ab"], zpad,
                           cols["ga"], cols["gb"]], axis=1)
    nk = GLA_H * GLA_DK
    w_a2 = jnp.zeros((LANES, 2 * nk), F32)
    w_a2 = w_a2.at[:GATE_RANK, :nk].set(w_a2_fwd).at[GATE_RANK:2 * GATE_RANK, nk:].set(w_a2_bwd)
    return {
        "attn_norm": attn_norm[None], "w_cq": cols["cq"].astype(BF16), "w_ckvr": w_ckvr.astype(BF16),
        "q_a_norm": q_a_norm[None], "w_uq": jnp.concatenate([uq_nope, uq_a, uq_b], axis=1).astype(BF16),
        "kv_a_norm": kv_a_norm[None], "w_ukv": w_ukv2.astype(BF16),
        "g_q": gain_rows(q_norm), "g_k": gain_rows(k_norm),
        "w_g": w_g.astype(BF16), "w_a2": w_a2.astype(BF16),
        "b_a2": jnp.concatenate([b_a2_fwd, b_a2_bwd])[None],
        "w_o_mla": w_o_mla.astype(BF16), "gla_o_norm": gla_o_norm[None], "w_o_gla": w_o_gla.astype(BF16),
        "w_out": w_out.astype(BF16), "ffn_norm": ffn_norm[None], "w_ffn_gate": w_ffn_gate.astype(BF16),
        "w_ffn_up": w_ffn_up.astype(BF16), "w_ffn_down": w_ffn_down.astype(BF16),
    }


def _rope_tables(length):
    inv = 1.0 / (ROPE_THETA ** (jnp.arange(0, ROPE_DIM, 2, dtype=F32) / ROPE_DIM))
    ang = jnp.arange(length, dtype=F32)[:, None] * inv[None, :]
    cos, sin = jnp.cos(ang), jnp.sin(ang)
    return jnp.tile(cos, (1, 4)), jnp.concatenate([-sin, sin, -sin, sin], axis=1)


def _encode_group(x, w, meta, cos_t, sin_t):
    b, seq, _ = x.shape
    n = b * seq
    x2d = x.reshape(n, D_MODEL)
    tm = _row_tile(seq, 256)
    q, k, vt = _mla_proj(x2d, w, cos_t[N_META:N_META + seq], sin_t[N_META:N_META + seq], seq, tm, True)
    gq, gk, gv, og, lg, gates = _gla_proj(x2d, w, tm)

    o_attn = _attention(q.reshape(b, seq, -1), k.reshape(b, seq, -1), vt,
                        meta["k"], meta["vt"], _row_tile(seq, 1024), _row_tile(seq // 2, 1024))

    t_blk = _row_tile(seq, 512)
    g3 = lambda a: a.reshape(b, seq, -1)
    o_f, _ = _gla_scan(g3(gq), g3(gk), g3(gv), g3(lg), meta["state"], False, t_blk)
    o_b, _ = _gla_scan(g3(gq), g3(gk), g3(gv), g3(lg), jnp.zeros_like(meta["state"]), True, t_blk)

    y = _mix_ffn(x2d, o_attn.reshape(n, -1), o_f.reshape(n, -1), o_b.reshape(n, -1), og, gates, w, tm)
    return y.reshape(b, seq, D_MODEL)


def kernel(x_prompt, x_sample, meta_tokens, attn_norm, w_in, q_a_norm, w_uq, kv_a_norm, w_ukv, q_norm, k_norm, w_o_mla, w_a2_fwd, b_a2_fwd, w_a2_bwd, b_a2_bwd, gla_o_norm, w_o_gla, w_out, ffn_norm, w_ffn_gate, w_ffn_up, w_ffn_down):
    assert attn_norm.shape[0] == 1, "single-layer encoder"
    w = _prep_weights(attn_norm[0], w_in[0], q_a_norm[0], w_uq[0], kv_a_norm[0], w_ukv[0], q_norm[0],
                      k_norm[0], w_o_mla[0], w_a2_fwd[0], b_a2_fwd[0], w_a2_bwd[0], b_a2_bwd[0],
                      gla_o_norm[0], w_o_gla[0], w_out[0], ffn_norm[0], w_ffn_gate[0], w_ffn_up[0],
                      w_ffn_down[0])
    max_len = N_META + max(x_prompt.shape[1], x_sample.shape[1])
    cos_t, sin_t = _rope_tables(max_len)

    xm = meta_tokens.astype(F32)
    _, k_m, v_m = _mla_proj(xm, w, cos_t[:N_META], sin_t[:N_META], N_META, N_META, False)
    mq, mk, mv, _, mlg, _ = _gla_proj(xm, w, N_META)
    pad = lambda a: jnp.pad(a, ((GLA_CHUNK - N_META, 0), (0, 0)))[None]
    _, s_meta = _gla_scan(pad(mq), pad(mk), pad(mv), pad(mlg),
                          jnp.zeros((GLA_H, GLA_DV, GLA_DK), F32), False, GLA_CHUNK)
    vt_m = jnp.concatenate([v_m.T.reshape(MLA_H, V_DIM, N_META), jnp.ones((MLA_H, ONES_ROWS, N_META), BF16)],
                           axis=1).reshape(MLA_H * VT_ROWS, N_META)
    meta = {"k": k_m, "vt": vt_m, "state": s_meta[0]}

    return (_encode_group(x_prompt, w, meta, cos_t, sin_t),
            _encode_group(x_sample, w, meta, cos_t, sin_t))
```

```python
import functools
import math

import jax
import jax.numpy as jnp
from jax import lax
from jax.experimental import pallas as pl
from jax.experimental.pallas import tpu as pltpu

D_MODEL = 1024
N_META = 16
MLA_H = 8
NOPE_DIM = 128
ROPE_DIM = 64
QK_DIM = NOPE_DIM + ROPE_DIM
V_DIM = 128
Q_LORA = 768
KV_LORA = 256
ROPE_THETA = 10000.0
GLA_H = 4
GLA_DK = 128
GLA_DV = 256
GATE_RANK = 16
GATE_TEMP = 16.0
GLA_CHUNK = 64
D_FF = 2816
NORM_EPS = 1e-6

QK_PAD = 256
ONES_ROWS = 16
VT_ROWS = V_DIM + ONES_ROWS
LANES = 128
MXU_DIM = 256
LOG2E = 1.4426950408889634
VMEM_LIMIT = 56 * 1024 * 1024

F32 = jnp.float32
BF16 = jnp.bfloat16


def _dot(a, b):
    return jnp.dot(a, b, preferred_element_type=F32)


def _dot_nt(a, b):
    return lax.dot_general(a, b, (((1,), (1,)), ((), ())), preferred_element_type=F32)


def _dot_tn(a, b):
    return lax.dot_general(a, b, (((0,), (0,)), ((), ())), preferred_element_type=F32)


def _rms(x):
    return x * lax.rsqrt(jnp.mean(x * x, axis=-1, keepdims=True) + NORM_EPS)


def _resident(shape):
    return pl.BlockSpec(shape, lambda *_: (0,) * len(shape), pipeline_mode=pl.Buffered(1))


def _params(*sem):
    return pltpu.CompilerParams(dimension_semantics=sem, vmem_limit_bytes=VMEM_LIMIT)


def _row_tile(n, want):
    t = min(n, want)
    assert n % t == 0, (n, t)
    return t


def _mla_proj_kernel(x_ref, an_ref, wcq_ref, wckv_ref, qan_ref, wuq_ref, kvan_ref, wukv_ref,
                     gq_ref, gk_ref, cos_ref, sin_ref, q_out, k_out, v_out, *, transpose_v):
    x = x_ref[...]
    h = (_rms(x) * an_ref[...]).astype(BF16)
    cq = _dot(h, wcq_ref[...])
    cqn = (_rms(cq) * qan_ref[...]).astype(BF16)
    qall = _dot(cqn, wuq_ref[...])
    ckvr = _dot(h, wckv_ref[...])
    ckvn = (_rms(ckvr[:, :KV_LORA]) * kvan_ref[...]).astype(BF16)
    kv = _dot(ckvn, wukv_ref[...])
    v = kv[:, MLA_H * NOPE_DIM:]
    if transpose_v:
        vt = v.T.astype(BF16)
        ones = jnp.ones((ONES_ROWS, v.shape[0]), BF16)
        for hd in range(MLA_H):
            v_out[hd * VT_ROWS:hd * VT_ROWS + V_DIM, :] = vt[hd * V_DIM:(hd + 1) * V_DIM, :]
            v_out[hd * VT_ROWS + V_DIM:(hd + 1) * VT_ROWS, :] = ones
    else:
        v_out[...] = v.astype(BF16)

    cos = cos_ref[...]
    sin = sin_ref[...]
    lane = lax.broadcasted_iota(jnp.int32, (1, LANES), 1)
    ka = ckvr[:, KV_LORA:KV_LORA + LANES]
    kb = ckvr[:, KV_LORA + LANES:KV_LORA + 2 * LANES]
    head_lanes = [((lane >= p * ROPE_DIM) & (lane < (p + 1) * ROPE_DIM)).astype(F32) for p in range(2)]
    ssq_kr = jnp.sum(ka * ka * head_lanes[0], axis=-1, keepdims=True)
    k_rot = [(ka * gk_ref[1 + p:2 + p, :]) * cos + (kb * gk_ref[3 + p:4 + p, :]) * sin for p in range(2)]
    q_scale = QK_DIM ** -0.5 * LOG2E
    rope0 = MLA_H * NOPE_DIM
    for hd in range(MLA_H):
        j, p = divmod(hd, 2)
        qn = qall[:, hd * NOPE_DIM:(hd + 1) * NOPE_DIM]
        qa = qall[:, rope0 + j * LANES:rope0 + (j + 1) * LANES]
        qb = qall[:, rope0 + 4 * LANES + j * LANES:rope0 + 4 * LANES + (j + 1) * LANES]
        ssq = (jnp.sum(qn * qn, axis=-1, keepdims=True)
               + jnp.sum(qa * qa * head_lanes[p], axis=-1, keepdims=True))
        rs = lax.rsqrt(ssq * (1.0 / QK_DIM) + NORM_EPS) * q_scale
        q_rot = (qa * gq_ref[1 + p:2 + p, :]) * cos + (qb * gq_ref[3 + p:4 + p, :]) * sin
        q_out[:, hd * QK_PAD:hd * QK_PAD + LANES] = (qn * gq_ref[0:1, :] * rs).astype(BF16)
        q_out[:, hd * QK_PAD + LANES:(hd + 1) * QK_PAD] = (q_rot * rs).astype(BF16)
        kn = kv[:, hd * NOPE_DIM:(hd + 1) * NOPE_DIM]
        rsk = lax.rsqrt((jnp.sum(kn * kn, axis=-1, keepdims=True) + ssq_kr) * (1.0 / QK_DIM) + NORM_EPS)
        k_out[:, hd * QK_PAD:hd * QK_PAD + LANES] = (kn * gk_ref[0:1, :] * rsk).astype(BF16)
        k_out[:, hd * QK_PAD + LANES:(hd + 1) * QK_PAD] = (k_rot[p] * rsk).astype(BF16)


def _mla_proj(x2d, w, cos_t, sin_t, seq_len, tm, transpose_v):
    n = x2d.shape[0]
    nt = seq_len // tm
    row = lambda i: (i, 0)
    pos = lambda i: (i % nt, 0)
    v_dims = MLA_H * V_DIM
    vt_rows = MLA_H * VT_ROWS
    return pl.pallas_call(
        functools.partial(_mla_proj_kernel, transpose_v=transpose_v),
        grid=(n // tm,),
        in_specs=[
            pl.BlockSpec((tm, D_MODEL), row),
            _resident((1, D_MODEL)),
            _resident(w["w_cq"].shape),
            _resident(w["w_ckvr"].shape),
            _resident((1, Q_LORA)),
            _resident(w["w_uq"].shape),
            _resident((1, KV_LORA)),
            _resident(w["w_ukv"].shape),
            _resident((8, LANES)),
            _resident((8, LANES)),
            pl.BlockSpec((tm, LANES), pos),
            pl.BlockSpec((tm, LANES), pos),
        ],
        out_specs=[
            pl.BlockSpec((tm, MLA_H * QK_PAD), row),
            pl.BlockSpec((tm, MLA_H * QK_PAD), row),
            pl.BlockSpec((vt_rows, tm), lambda i: (0, i)) if transpose_v else pl.BlockSpec((tm, v_dims), row),
        ],
        out_shape=[
            jax.ShapeDtypeStruct((n, MLA_H * QK_PAD), BF16),
            jax.ShapeDtypeStruct((n, MLA_H * QK_PAD), BF16),
            jax.ShapeDtypeStruct((vt_rows, n) if transpose_v else (n, v_dims), BF16),
        ],
        compiler_params=_params("parallel"),
        name="mla_proj",
    )(x2d, w["attn_norm"], w["w_cq"], w["w_ckvr"], w["q_a_norm"], w["w_uq"], w["kv_a_norm"],
      w["w_ukv"], w["g_q"], w["g_k"], cos_t, sin_t)


G_Q0, G_K0, G_V0, G_G0, G_A0, G_S0, G_END = 0, 512, 1024, 2048, 3072, 3200, 5248


def _gla_proj_kernel(x_ref, an_ref, wg_ref, wa2_ref, ba2_ref,
                     q_out, k_out, v_out, og_out, lg_out, gate_out):
    x = x_ref[...]
    h = (_rms(x) * an_ref[...]).astype(BF16)
    q_out[...] = _dot(h, wg_ref[:, G_Q0:G_K0]) * (GLA_DK ** -0.5)
    k_out[...] = _dot(h, wg_ref[:, G_K0:G_V0])
    v_out[...] = _dot(h, wg_ref[:, G_V0:G_G0]).astype(BF16)
    g = _dot(h, wg_ref[:, G_G0:G_A0])
    og_out[...] = g * jax.nn.sigmoid(g)
    a = _dot(h, wg_ref[:, G_A0:G_S0]).astype(BF16)
    pre = _dot(a, wa2_ref[...]) + ba2_ref[...]
    lg_out[...] = jax.nn.log_sigmoid(pre) * (1.0 / GATE_TEMP)
    gate_out[...] = jax.nn.sigmoid(_dot(h, wg_ref[:, G_S0:G_END]))


def _gla_proj(x2d, w, tm):
    n = x2d.shape[0]
    row = lambda i: (i, 0)
    widths = (GLA_H * GLA_DK, GLA_H * GLA_DK, GLA_H * GLA_DV, GLA_H * GLA_DV, 2 * GLA_H * GLA_DK, 2 * D_MODEL)
    dtypes = (F32, F32, BF16, F32, F32, F32)
    return pl.pallas_call(
        _gla_proj_kernel,
        grid=(n // tm,),
        in_specs=[
            pl.BlockSpec((tm, D_MODEL), row),
            _resident((1, D_MODEL)),
            _resident(w["w_g"].shape),
            _resident(w["w_a2"].shape),
            _resident((1, 2 * GLA_H * GLA_DK)),
        ],
        out_specs=[pl.BlockSpec((tm, wd), row) for wd in widths],
        out_shape=[jax.ShapeDtypeStruct((n, wd), dt) for wd, dt in zip(widths, dtypes)],
        compiler_params=_params("parallel"),
        name="gla_proj",
    )(x2d, w["attn_norm"], w["w_g"], w["w_a2"], w["b_a2"])


def _attn_kernel(q_ref, k_ref, vt_ref, km_ref, vmt_ref, o_ref, s_even, s_odd, acc_scr, *, tk, ck, n_tiles):
    s_scr = (s_even, s_odd)
    n_chunks = tk // ck
    q = q_ref[0]
    qt = q.astype(F32).T.astype(BF16)
    s_meta = _dot(km_ref[...], qt)
    m_meta = jnp.max(s_meta, axis=0, keepdims=True)
    acc_scr[...] = _dot(vmt_ref[...], jnp.exp2(s_meta - m_meta).astype(BF16))

    def score_chunk(i, c, slot, running_max):
        off = pl.multiple_of(i * tk + c * ck, ck)
        s = _dot(k_ref[0, pl.ds(off, ck), :], qt)
        s_scr[slot][c * ck:(c + 1) * ck, :] = s
        s_max = jnp.max(s, axis=0, keepdims=True)
        return s_max if running_max is None else jnp.maximum(running_max, s_max)

    def gather_chunk(i, c, slot, m, partial):
        off = pl.multiple_of(i * tk + c * ck, ck)
        p = jnp.exp2(s_scr[slot][c * ck:(c + 1) * ck, :] - m).astype(BF16)
        d = _dot(vt_ref[:, pl.ds(off, ck)], p)
        return d if partial is None else partial + d

    def step(i, slot, s_max, m, last=False):
        m_new = jnp.maximum(m, s_max)
        next_max, partial = None, None
        for c in range(n_chunks):
            if not last:
                next_max = score_chunk(i + 1, c, 1 - slot, next_max)
            partial = gather_chunk(i, c, slot, m_new, partial)
        acc_scr[...] = jnp.exp2(m - m_new) * acc_scr[...] + partial
        return next_max, m_new

    def two_steps(j, carry):
        carry = step(2 * j, 0, *carry)
        return step(2 * j + 1, 1, *carry)

    max0 = None
    for c in range(n_chunks):
        max0 = score_chunk(0, c, 0, max0)
    carry = lax.fori_loop(0, n_tiles // 2 - 1, two_steps, (max0, m_meta))
    carry = step(n_tiles - 2, 0, *carry)
    step(n_tiles - 1, 1, *carry, last=True)
    acc = acc_scr[...]
    o_ref[0] = (acc[:V_DIM] / acc[V_DIM:V_DIM + 1]).T.astype(o_ref.dtype)


def _attention(q, k, vt, k_meta, vt_meta, tq, tk):
    b, seq, _ = q.shape
    assert seq % (2 * tk) == 0, (seq, tk)
    return pl.pallas_call(
        functools.partial(_attn_kernel, tk=tk, ck=min(tk, MXU_DIM), n_tiles=seq // tk),
        grid=(b, MLA_H, seq // tq),
        in_specs=[
            pl.BlockSpec((1, tq, QK_PAD), lambda bi, hi, qi: (bi, qi, hi)),
            pl.BlockSpec((1, seq, QK_PAD), lambda bi, hi, qi: (bi, 0, hi)),
            pl.BlockSpec((VT_ROWS, seq), lambda bi, hi, qi: (hi, bi)),
            pl.BlockSpec((N_META, QK_PAD), lambda bi, hi, qi: (0, hi)),
            pl.BlockSpec((VT_ROWS, N_META), lambda bi, hi, qi: (hi, 0)),
        ],
        out_specs=pl.BlockSpec((1, tq, V_DIM), lambda bi, hi, qi: (bi, qi, hi)),
        out_shape=jax.ShapeDtypeStruct((b, seq, MLA_H * V_DIM), BF16),
        scratch_shapes=[
            pltpu.VMEM((tk, tq), F32),
            pltpu.VMEM((tk, tq), F32),
            pltpu.VMEM((VT_ROWS, tq), F32),
        ],
        compiler_params=_params("parallel", "parallel", "arbitrary"),
        name="attention",
    )(q, k, vt, k_meta, vt_meta)


def _gla_scan_kernel(q_ref, k_ref, v_ref, lg_ref, s0_ref, o_ref, sfin_ref, state, u_scr, dec_scr, qe_scr,
                     *, reverse, n_chunks):
    blk = pl.program_id(1)

    @pl.when(blk == 0)
    def _():
        state[...] = s0_ref[...]

    ri = lax.broadcasted_iota(jnp.int32, (GLA_CHUNK, GLA_CHUNK), 0)
    ci = lax.broadcasted_iota(jnp.int32, (GLA_CHUNK, GLA_CHUNK), 1)
    keep = (ri <= ci) if reverse else (ri >= ci)
    csum = jnp.where(keep, 1.0, 0.0).astype(BF16)
    csum2 = jnp.concatenate([csum, csum], axis=1)
    mid = GLA_CHUNK // 2 if reverse else GLA_CHUNK // 2 - 1
    end = 0 if reverse else GLA_CHUNK - 1
    group = math.gcd(n_chunks, 4)

    heads = [(slice(hd * GLA_DK, (hd + 1) * GLA_DK), slice(hd * GLA_DV, (hd + 1) * GLA_DV)) for hd in range(GLA_H)]

    def local(g, carry):
        cs = [g * group + j for j in range(group)]
        rows = [pl.ds(pl.multiple_of(c * GLA_CHUNK, GLA_CHUNK), GLA_CHUNK) for c in cs]
        bcums = []
        for r in rows:
            lg = lg_ref[0, r, :]
            lg_hi = lg.astype(BF16)
            lg_lo = (lg - lg_hi.astype(F32)).astype(BF16)
            bcums.append(_dot(csum2, jnp.concatenate([lg_hi, lg_lo], axis=0)))
        qss, kss, kds = [], [], []
        for c, r, bcum in zip(cs, rows, bcums):
            b_mid = bcum[mid:mid + 1, :]
            b_end = bcum[end:end + 1, :]
            q = q_ref[0, r, :]
            k = k_ref[0, r, :]
            qss.append((q * jnp.exp(bcum - b_mid)).astype(BF16))
            kss.append((k * jnp.exp(b_mid - bcum)).astype(BF16))
            kds.append((k * jnp.exp(b_end - bcum)).astype(BF16))
            qe_scr[c] = (q * jnp.exp(bcum)).astype(BF16)
            dec_scr[c] = jnp.broadcast_to(jnp.exp(b_end), (8, GLA_H * GLA_DK))
        attn = [[_dot_nt(qs[:, kc], ks[:, kc]) for kc, _ in heads] for qs, ks in zip(qss, kss)]
        for c, r, kd in zip(cs, rows, kds):
            for hd, (kc, vc) in enumerate(heads):
                u_scr[c, hd] = _dot_tn(v_ref[0, r, vc], kd[:, kc])
        for r, a_c in zip(rows, attn):
            for (_, vc), a in zip(heads, a_c):
                o_ref[0, r, vc] = _dot(jnp.where(keep, a, 0.0).astype(BF16), v_ref[0, r, vc])
        return carry

    lax.fori_loop(0, n_chunks // group, local, 0)

    def recur(i, carry):
        c = (n_chunks - 1 - i) if reverse else i
        rows = pl.ds(pl.multiple_of(c * GLA_CHUNK, GLA_CHUNK), GLA_CHUNK)
        for hd, (kc, vc) in enumerate(heads):
            st = state[hd]
            o_ref[0, rows, vc] += _dot_nt(qe_scr[c, :, kc], st.astype(BF16))
            state[hd] = st * dec_scr[c, 0:1, kc] + u_scr[c, hd]
        return carry

    lax.fori_loop(0, n_chunks, recur, 0, unroll=group)

    @pl.when(blk == pl.num_programs(1) - 1)
    def _():
        sfin_ref[0] = state[...]


def _gla_scan(q, k, v, lg, s0, reverse, t_blk):
    b, t, _ = q.shape
    nb = t // t_blk
    lg_col = 1 if reverse else 0
    blk = (lambda bi, i: (bi, nb - 1 - i, 0)) if reverse else (lambda bi, i: (bi, i, 0))
    lg_blk = (lambda bi, i: (bi, nb - 1 - i, lg_col)) if reverse else (lambda bi, i: (bi, i, lg_col))
    st_shape = (GLA_H, GLA_DV, GLA_DK)
    n_chunks = t_blk // GLA_CHUNK
    return pl.pallas_call(
        functools.partial(_gla_scan_kernel, reverse=reverse, n_chunks=n_chunks),
        grid=(b, nb),
        in_specs=[
            pl.BlockSpec((1, t_blk, GLA_H * GLA_DK), blk),
            pl.BlockSpec((1, t_blk, GLA_H * GLA_DK), blk),
            pl.BlockSpec((1, t_blk, GLA_H * GLA_DV), blk),
            pl.BlockSpec((1, t_blk, GLA_H * GLA_DK), lg_blk),
            pl.BlockSpec(st_shape, lambda bi, i: (0, 0, 0)),
        ],
        out_specs=[
            pl.BlockSpec((1, t_blk, GLA_H * GLA_DV), blk),
            pl.BlockSpec((1,) + st_shape, lambda bi, i: (bi, 0, 0, 0)),
        ],
        out_shape=[
            jax.ShapeDtypeStruct((b, t, GLA_H * GLA_DV), F32),
            jax.ShapeDtypeStruct((b,) + st_shape, F32),
        ],
        scratch_shapes=[
            pltpu.VMEM(st_shape, F32),
            pltpu.VMEM((n_chunks,) + st_shape, F32),
            pltpu.VMEM((n_chunks, 8, GLA_H * GLA_DK), F32),
            pltpu.VMEM((n_chunks, GLA_CHUNK, GLA_H * GLA_DK), BF16),
        ],
        compiler_params=_params("parallel", "arbitrary"),
        name="gla_scan_bwd" if reverse else "gla_scan_fwd",
    )(q, k, v, lg, s0)


def _mix_ffn_kernel(x_ref, oa_ref, of_ref, ob_ref, og_ref, gate_ref, womla_ref, onorm_ref, wogla_ref,
                    wout_ref, fn_ref, wgate_ref, wup_ref, wdown_ref, y_ref):
    y_a = _dot(oa_ref[...], womla_ref[...])
    o = of_ref[...] + ob_ref[...]
    og = og_ref[...]
    parts = []
    for hd in range(GLA_H):
        cols = slice(hd * GLA_DV, (hd + 1) * GLA_DV)
        parts.append((_rms(o[:, cols]) * onorm_ref[...] * og[:, cols]).astype(BF16))
    y_b = _dot(jnp.concatenate(parts, axis=-1), wogla_ref[...])
    gates = gate_ref[...]
    mixed = (gates[:, :D_MODEL] * y_a + gates[:, D_MODEL:] * y_b).astype(BF16)
    x1 = x_ref[...] + _dot(mixed, wout_ref[...])
    h = (_rms(x1) * fn_ref[...]).astype(BF16)
    g = _dot(h, wgate_ref[...])
    u = _dot(h, wup_ref[...])
    act = (g * jax.nn.sigmoid(g) * u).astype(BF16)
    y_ref[...] = x1 + _dot(act, wdown_ref[...])


def _mix_ffn(x2d, o_attn, o_f, o_b, og, gates, w, tm):
    n = x2d.shape[0]
    row = lambda i: (i, 0)
    return pl.pallas_call(
        _mix_ffn_kernel,
        grid=(n // tm,),
        in_specs=[
            pl.BlockSpec((tm, D_MODEL), row),
            pl.BlockSpec((tm, MLA_H * V_DIM), row),
            pl.BlockSpec((tm, GLA_H * GLA_DV), row),
            pl.BlockSpec((tm, GLA_H * GLA_DV), row),
            pl.BlockSpec((tm, GLA_H * GLA_DV), row),
            pl.BlockSpec((tm, 2 * D_MODEL), row),
            _resident(w["w_o_mla"].shape),
            _resident((1, GLA_DV)),
            _resident(w["w_o_gla"].shape),
            _resident(w["w_out"].shape),
            _resident((1, D_MODEL)),
            _resident(w["w_ffn_gate"].shape),
            _resident(w["w_ffn_up"].shape),
            _resident(w["w_ffn_down"].shape),
        ],
        out_specs=pl.BlockSpec((tm, D_MODEL), row),
        out_shape=jax.ShapeDtypeStruct((n, D_MODEL), F32),
        compiler_params=_params("parallel"),
        name="mix_ffn",
    )(x2d, o_attn, o_f, o_b, og, gates, w["w_o_mla"], w["gla_o_norm"], w["w_o_gla"], w["w_out"],
      w["ffn_norm"], w["w_ffn_gate"], w["w_ffn_up"], w["w_ffn_down"])


def _prep_weights(attn_norm, w_in, q_a_norm, w_uq, kv_a_norm, w_ukv, q_norm, k_norm, w_o_mla,
                  w_a2_fwd, b_a2_fwd, w_a2_bwd, b_a2_bwd, gla_o_norm, w_o_gla, w_out, ffn_norm,
                  w_ffn_gate, w_ffn_up, w_ffn_down):
    half = ROPE_DIM // 2
    o = 0
    cols = {}
    for name, size in (("cq", Q_LORA), ("ckv", KV_LORA), ("kr", ROPE_DIM), ("gq", GLA_H * GLA_DK),
                       ("gk", GLA_H * GLA_DK), ("gv", GLA_H * GLA_DV), ("gg", GLA_H * GLA_DV),
                       ("af", GATE_RANK), ("ab", GATE_RANK), ("ga", D_MODEL), ("gb", D_MODEL)):
        cols[name] = w_in[:, o:o + size]
        o += size
    kr1, kr2 = cols["kr"][:, :half], cols["kr"][:, half:]
    w_ckvr = jnp.concatenate([cols["ckv"], kr1, kr2, kr1, kr2, kr2, kr1, kr2, kr1], axis=1)

    uq = w_uq.reshape(Q_LORA, MLA_H, QK_DIM)
    uq_nope = uq[:, :, :NOPE_DIM].reshape(Q_LORA, MLA_H * NOPE_DIM)
    uq_r = uq[:, :, NOPE_DIM:]
    uq_a = uq_r.reshape(Q_LORA, MLA_H * ROPE_DIM)
    uq_b = jnp.concatenate([uq_r[:, :, half:], uq_r[:, :, :half]], axis=-1).reshape(Q_LORA, MLA_H * ROPE_DIM)
    ukv = w_ukv.reshape(KV_LORA, MLA_H, NOPE_DIM + V_DIM)
    w_ukv2 = jnp.concatenate([ukv[:, :, :NOPE_DIM].reshape(KV_LORA, MLA_H * NOPE_DIM),
                              ukv[:, :, NOPE_DIM:].reshape(KV_LORA, MLA_H * V_DIM)], axis=1)

    def gain_rows(g):
        z = jnp.zeros((ROPE_DIM,), F32)
        ga = g[NOPE_DIM:]
        gb = jnp.concatenate([g[NOPE_DIM + half:], g[NOPE_DIM:NOPE_DIM + half]])
        rows = [g[:NOPE_DIM], jnp.concatenate([ga, z]), jnp.concatenate([z, ga]),
                jnp.concatenate([gb, z]), jnp.concatenate([z, gb])]
        rows += [jnp.zeros((LANES,), F32)] * 3
        return jnp.stack(rows)

    zpad = jnp.zeros((D_MODEL, LANES - 2 * GATE_RANK), F32)
    w_g = jnp.concatenate([cols["gq"], cols["gk"], cols["gv"], cols["gg"], cols["af"], cols["ab"], zpad,
                           cols["ga"], cols["gb"]], axis=1)
    nk = GLA_H * GLA_DK
    w_a2 = jnp.zeros((LANES, 2 * nk), F32)
    w_a2 = w_a2.at[:GATE_RANK, :nk].set(w_a2_fwd).at[GATE_RANK:2 * GATE_RANK, nk:].set(w_a2_bwd)
    return {
        "attn_norm": attn_norm[None], "w_cq": cols["cq"].astype(BF16), "w_ckvr": w_ckvr.astype(BF16),
        "q_a_norm": q_a_norm[None], "w_uq": jnp.concatenate([uq_nope, uq_a, uq_b], axis=1).astype(BF16),
        "kv_a_norm": kv_a_norm[None], "w_ukv": w_ukv2.astype(BF16),
        "g_q": gain_rows(q_norm), "g_k": gain_rows(k_norm),
        "w_g": w_g.astype(BF16), "w_a2": w_a2.astype(BF16),
        "b_a2": jnp.concatenate([b_a2_fwd, b_a2_bwd])[None],
        "w_o_mla": w_o_mla.astype(BF16), "gla_o_norm": gla_o_norm[None], "w_o_gla": w_o_gla.astype(BF16),
        "w_out": w_out.astype(BF16), "ffn_norm": ffn_norm[None], "w_ffn_gate": w_ffn_gate.astype(BF16),
        "w_ffn_up": w_ffn_up.astype(BF16), "w_ffn_down": w_ffn_down.astype(BF16),
    }


def _rope_tables(length):
    inv = 1.0 / (ROPE_THETA ** (jnp.arange(0, ROPE_DIM, 2, dtype=F32) / ROPE_DIM))
    ang = jnp.arange(length, dtype=F32)[:, None] * inv[None, :]
    cos, sin = jnp.cos(ang), jnp.sin(ang)
    return jnp.tile(cos, (1, 4)), jnp.concatenate([-sin, sin, -sin, sin], axis=1)


def _encode_group(x, w, meta, cos_t, sin_t):
    b, seq, _ = x.shape
    n = b * seq
    x2d = x.reshape(n, D_MODEL)
    tm = _row_tile(seq, 256)
    q, k, vt = _mla_proj(x2d, w, cos_t[N_META:N_META + seq], sin_t[N_META:N_META + seq], seq, tm, True)
    gq, gk, gv, og, lg, gates = _gla_proj(x2d, w, tm)

    o_attn = _attention(q.reshape(b, seq, -1), k.reshape(b, seq, -1), vt,
                        meta["k"], meta["vt"], _row_tile(seq, 1024), _row_tile(seq // 2, 1024))

    t_blk = _row_tile(seq, 512)
    g3 = lambda a: a.reshape(b, seq, -1)
    o_f, _ = _gla_scan(g3(gq), g3(gk), g3(gv), g3(lg), meta["state"], False, t_blk)
    o_b, _ = _gla_scan(g3(gq), g3(gk), g3(gv), g3(lg), jnp.zeros_like(meta["state"]), True, t_blk)

    y = _mix_ffn(x2d, o_attn.reshape(n, -1), o_f.reshape(n, -1), o_b.reshape(n, -1), og, gates, w, tm)
    return y.reshape(b, seq, D_MODEL)


def kernel(x_prompt, x_sample, meta_tokens, attn_norm, w_in, q_a_norm, w_uq, kv_a_norm, w_ukv, q_norm, k_norm, w_o_mla, w_a2_fwd, b_a2_fwd, w_a2_bwd, b_a2_bwd, gla_o_norm, w_o_gla, w_out, ffn_norm, w_ffn_gate, w_ffn_up, w_ffn_down):
    assert attn_norm.shape[0] == 1, "single-layer encoder"
    w = _prep_weights(attn_norm[0], w_in[0], q_a_norm[0], w_uq[0], kv_a_norm[0], w_ukv[0], q_norm[0],
                      k_norm[0], w_o_mla[0], w_a2_fwd[0], b_a2_fwd[0], w_a2_bwd[0], b_a2_bwd[0],
                      gla_o_norm[0], w_o_gla[0], w_out[0], ffn_norm[0], w_ffn_gate[0], w_ffn_up[0],
                      w_ffn_down[0])
    max_len = N_META + max(x_prompt.shape[1], x_sample.shape[1])
    cos_t, sin_t = _rope_tables(max_len)

    xm = meta_tokens.astype(F32)
    _, k_m, v_m = _mla_proj(xm, w, cos_t[:N_META], sin_t[:N_META], N_META, N_META, False)
    mq, mk, mv, _, mlg, _ = _gla_proj(xm, w, N_META)
    pad = lambda a: jnp.pad(a, ((GLA_CHUNK - N_META, 0), (0, 0)))[None]
    _, s_meta = _gla_scan(pad(mq), pad(mk), pad(mv), pad(mlg),
                          jnp.zeros((GLA_H, GLA_DV, GLA_DK), F32), False, GLA_CHUNK)
    vt_m = jnp.concatenate([v_m.T.reshape(MLA_H, V_DIM, N_META), jnp.ones((MLA_H, ONES_ROWS, N_META), BF16)],
                           axis=1).reshape(MLA_H * VT_ROWS, N_META)
    meta = {"k": k_m, "vt": vt_m, "state": s_meta[0]}

    return (_encode_group(x_prompt, w, meta, cos_t, sin_t),
            _encode_group(x_sample, w, meta, cos_t, sin_t))
```

```python
import functools
import math

import jax
import jax.numpy as jnp
from jax import lax
from jax.experimental import pallas as pl
from jax.experimental.pallas import tpu as pltpu

D_MODEL = 1024
N_META = 16
MLA_H = 8
NOPE_DIM = 128
ROPE_DIM = 64
QK_DIM = NOPE_DIM + ROPE_DIM
V_DIM = 128
Q_LORA = 768
KV_LORA = 256
ROPE_THETA = 10000.0
GLA_H = 4
GLA_DK = 128
GLA_DV = 256
GATE_RANK = 16
GATE_TEMP = 16.0
GLA_CHUNK = 64
D_FF = 2816
NORM_EPS = 1e-6

QK_PAD = 256
ONES_ROWS = 16
VT_ROWS = V_DIM + ONES_ROWS
LANES = 128
MXU_DIM = 256
LOG2E = 1.4426950408889634
VMEM_LIMIT = 56 * 1024 * 1024

F32 = jnp.float32
BF16 = jnp.bfloat16


def _dot(a, b):
    return jnp.dot(a, b, preferred_element_type=F32)


def _dot_nt(a, b):
    return lax.dot_general(a, b, (((1,), (1,)), ((), ())), preferred_element_type=F32)


def _dot_tn(a, b):
    return lax.dot_general(a, b, (((0,), (0,)), ((), ())), preferred_element_type=F32)


def _rms(x):
    return x * lax.rsqrt(jnp.mean(x * x, axis=-1, keepdims=True) + NORM_EPS)


def _resident(shape):
    return pl.BlockSpec(shape, lambda *_: (0,) * len(shape), pipeline_mode=pl.Buffered(1))


def _params(*sem):
    return pltpu.CompilerParams(dimension_semantics=sem, vmem_limit_bytes=VMEM_LIMIT)


def _row_tile(n, want):
    t = min(n, want)
    assert n % t == 0, (n, t)
    return t


G_Q0, G_K0, G_V0, G_G0, G_A0, G_S0, G_END = 0, 512, 1024, 2048, 3072, 3200, 5248


def _proj_kernel(x_ref, an_ref, wcq_ref, wckv_ref, qan_ref, wuq_ref, kvan_ref, wukv_ref, gq_ref, gk_ref,
                 cos_ref, sin_ref, wg_ref, wa2_ref, ba2_ref,
                 q_out, k_out, v_out, gla_q_out, gla_k_out, gla_v_out, og_out, lg_out, gate_out, *, transpose_v):
    h = (_rms(x_ref[...]) * an_ref[...]).astype(BF16)
    cq = _dot(h, wcq_ref[...])
    ckvr = _dot(h, wckv_ref[...])
    gla_q_out[...] = _dot(h, wg_ref[:, G_Q0:G_K0]) * (GLA_DK ** -0.5)
    gla_k_out[...] = _dot(h, wg_ref[:, G_K0:G_V0])
    cqn = (_rms(cq) * qan_ref[...]).astype(BF16)
    ckvn = (_rms(ckvr[:, :KV_LORA]) * kvan_ref[...]).astype(BF16)
    qall = _dot(cqn, wuq_ref[...])
    kv = _dot(ckvn, wukv_ref[...])
    gla_v_out[...] = _dot(h, wg_ref[:, G_V0:G_G0]).astype(BF16)
    g = _dot(h, wg_ref[:, G_G0:G_A0])
    og_out[...] = (g * jax.nn.sigmoid(g)).astype(og_out.dtype)
    a = _dot(h, wg_ref[:, G_A0:G_S0]).astype(BF16)
    pre = _dot(a, wa2_ref[...]) + ba2_ref[...]
    lg_out[...] = jax.nn.log_sigmoid(pre) * (1.0 / GATE_TEMP)
    gate_out[...] = jax.nn.sigmoid(_dot(h, wg_ref[:, G_S0:G_END])).astype(gate_out.dtype)

    v = kv[:, MLA_H * NOPE_DIM:]
    if transpose_v:
        vt = v.T.astype(BF16)
        ones = jnp.ones((ONES_ROWS, v.shape[0]), BF16)
        for hd in range(MLA_H):
            v_out[hd * VT_ROWS:hd * VT_ROWS + V_DIM, :] = vt[hd * V_DIM:(hd + 1) * V_DIM, :]
            v_out[hd * VT_ROWS + V_DIM:(hd + 1) * VT_ROWS, :] = ones
    else:
        v_out[...] = v.astype(BF16)

    cos = cos_ref[...]
    sin = sin_ref[...]
    lane = lax.broadcasted_iota(jnp.int32, (1, LANES), 1)
    ka = ckvr[:, KV_LORA:KV_LORA + LANES]
    kb = ckvr[:, KV_LORA + LANES:KV_LORA + 2 * LANES]
    head_lanes = [((lane >= p * ROPE_DIM) & (lane < (p + 1) * ROPE_DIM)).astype(F32) for p in range(2)]
    ssq_kr = jnp.sum(ka * ka * head_lanes[0], axis=-1, keepdims=True)
    k_rot = [(ka * gk_ref[1 + p:2 + p, :]) * cos + (kb * gk_ref[3 + p:4 + p, :]) * sin for p in range(2)]
    q_scale = QK_DIM ** -0.5 * LOG2E
    rope0 = MLA_H * NOPE_DIM
    for hd in range(MLA_H):
        j, p = divmod(hd, 2)
        qn = qall[:, hd * NOPE_DIM:(hd + 1) * NOPE_DIM]
        qa = qall[:, rope0 + j * LANES:rope0 + (j + 1) * LANES]
        qb = qall[:, rope0 + 4 * LANES + j * LANES:rope0 + 4 * LANES + (j + 1) * LANES]
        ssq = (jnp.sum(qn * qn, axis=-1, keepdims=True)
               + jnp.sum(qa * qa * head_lanes[p], axis=-1, keepdims=True))
        rs = lax.rsqrt(ssq * (1.0 / QK_DIM) + NORM_EPS) * q_scale
        q_rot = (qa * gq_ref[1 + p:2 + p, :]) * cos + (qb * gq_ref[3 + p:4 + p, :]) * sin
        q_out[:, hd * QK_PAD:hd * QK_PAD + LANES] = (qn * gq_ref[0:1, :] * rs).astype(BF16)
        q_out[:, hd * QK_PAD + LANES:(hd + 1) * QK_PAD] = (q_rot * rs).astype(BF16)
        kn = kv[:, hd * NOPE_DIM:(hd + 1) * NOPE_DIM]
        rsk = lax.rsqrt((jnp.sum(kn * kn, axis=-1, keepdims=True) + ssq_kr) * (1.0 / QK_DIM) + NORM_EPS)
        k_out[:, hd * QK_PAD:hd * QK_PAD + LANES] = (kn * gk_ref[0:1, :] * rsk).astype(BF16)
        k_out[:, hd * QK_PAD + LANES:(hd + 1) * QK_PAD] = (k_rot[p] * rsk).astype(BF16)


def _project(x2d, w, cos_t, sin_t, seq_len, tm, transpose_v):
    n = x2d.shape[0]
    nt = seq_len // tm
    row = lambda i: (i, 0)
    pos = lambda i: (i % nt, 0)
    v_dims = MLA_H * V_DIM
    vt_rows = MLA_H * VT_ROWS
    nk, nv = GLA_H * GLA_DK, GLA_H * GLA_DV
    gla_outs = ((nk, F32), (nk, F32), (nv, BF16), (nv, BF16), (2 * nk, F32), (2 * D_MODEL, BF16))
    return pl.pallas_call(
        functools.partial(_proj_kernel, transpose_v=transpose_v),
        grid=(n // tm,),
        in_specs=[
            pl.BlockSpec((tm, D_MODEL), row),
            _resident((1, D_MODEL)),
            _resident(w["w_cq"].shape),
            _resident(w["w_ckvr"].shape),
            _resident((1, Q_LORA)),
            _resident(w["w_uq"].shape),
            _resident((1, KV_LORA)),
            _resident(w["w_ukv"].shape),
            _resident((8, LANES)),
            _resident((8, LANES)),
            pl.BlockSpec((tm, LANES), pos),
            pl.BlockSpec((tm, LANES), pos),
            _resident(w["w_g"].shape),
            _resident(w["w_a2"].shape),
            _resident((1, 2 * nk)),
        ],
        out_specs=[
            pl.BlockSpec((tm, MLA_H * QK_PAD), row),
            pl.BlockSpec((tm, MLA_H * QK_PAD), row),
            pl.BlockSpec((vt_rows, tm), lambda i: (0, i)) if transpose_v else pl.BlockSpec((tm, v_dims), row),
        ] + [pl.BlockSpec((tm, width), row) for width, _ in gla_outs],
        out_shape=[
            jax.ShapeDtypeStruct((n, MLA_H * QK_PAD), BF16),
            jax.ShapeDtypeStruct((n, MLA_H * QK_PAD), BF16),
            jax.ShapeDtypeStruct((vt_rows, n) if transpose_v else (n, v_dims), BF16),
        ] + [jax.ShapeDtypeStruct((n, width), dtype) for width, dtype in gla_outs],
        compiler_params=_params("parallel"),
        name="project",
    )(x2d, w["attn_norm"], w["w_cq"], w["w_ckvr"], w["q_a_norm"], w["w_uq"], w["kv_a_norm"],
      w["w_ukv"], w["g_q"], w["g_k"], cos_t, sin_t, w["w_g"], w["w_a2"], w["b_a2"])


def _attn_kernel(q_ref, k_ref, vt_ref, km_ref, vmt_ref, o_ref, s_even, s_odd, acc_scr, *, tk, ck, n_tiles):
    s_scr = (s_even, s_odd)
    n_chunks = tk // ck
    q = q_ref[0]
    qt = q.astype(F32).T.astype(BF16)
    s_meta = _dot(km_ref[...], qt)
    m_meta = jnp.max(s_meta, axis=0, keepdims=True)
    acc_scr[...] = _dot(vmt_ref[...], jnp.exp2(s_meta - m_meta).astype(BF16))

    def score_chunk(i, c, slot, running_max):
        off = pl.multiple_of(i * tk + c * ck, ck)
        s = _dot(k_ref[0, pl.ds(off, ck), :], qt)
        s_scr[slot][c * ck:(c + 1) * ck, :] = s
        s_max = jnp.max(s, axis=0, keepdims=True)
        return s_max if running_max is None else jnp.maximum(running_max, s_max)

    def gather_chunk(i, c, slot, m, partial):
        off = pl.multiple_of(i * tk + c * ck, ck)
        p = jnp.exp2(s_scr[slot][c * ck:(c + 1) * ck, :] - m).astype(BF16)
        d = _dot(vt_ref[:, pl.ds(off, ck)], p)
        return d if partial is None else partial + d

    def step(i, slot, s_max, m, last=False):
        m_new = jnp.maximum(m, s_max)
        next_max, partial = None, None
        for c in range(n_chunks):
            if not last:
                next_max = score_chunk(i + 1, c, 1 - slot, next_max)
            partial = gather_chunk(i, c, slot, m_new, partial)
        acc_scr[...] = jnp.exp2(m - m_new) * acc_scr[...] + partial
        return next_max, m_new

    def two_steps(j, carry):
        carry = step(2 * j, 0, *carry)
        return step(2 * j + 1, 1, *carry)

    max0 = None
    for c in range(n_chunks):
        max0 = score_chunk(0, c, 0, max0)
    carry = lax.fori_loop(0, n_tiles // 2 - 1, two_steps, (max0, m_meta))
    carry = step(n_tiles - 2, 0, *carry)
    step(n_tiles - 1, 1, *carry, last=True)
    acc = acc_scr[...]
    o_ref[0] = (acc[:V_DIM] / acc[V_DIM:V_DIM + 1]).T.astype(o_ref.dtype)


def _attention(q, k, vt, k_meta, vt_meta, tq, tk):
    b, seq, _ = q.shape
    assert seq % (2 * tk) == 0, (seq, tk)
    return pl.pallas_call(
        functools.partial(_attn_kernel, tk=tk, ck=min(tk, MXU_DIM), n_tiles=seq // tk),
        grid=(b, MLA_H, seq // tq),
        in_specs=[
            pl.BlockSpec((1, tq, QK_PAD), lambda bi, hi, qi: (bi, qi, hi)),
            pl.BlockSpec((1, seq, QK_PAD), lambda bi, hi, qi: (bi, 0, hi)),
            pl.BlockSpec((VT_ROWS, seq), lambda bi, hi, qi: (hi, bi)),
            pl.BlockSpec((N_META, QK_PAD), lambda bi, hi, qi: (0, hi)),
            pl.BlockSpec((VT_ROWS, N_META), lambda bi, hi, qi: (hi, 0)),
        ],
        out_specs=pl.BlockSpec((1, tq, V_DIM), lambda bi, hi, qi: (bi, qi, hi)),
        out_shape=jax.ShapeDtypeStruct((b, seq, MLA_H * V_DIM), BF16),
        scratch_shapes=[
            pltpu.VMEM((tk, tq), F32),
            pltpu.VMEM((tk, tq), F32),
            pltpu.VMEM((VT_ROWS, tq), F32),
        ],
        compiler_params=_params("parallel", "parallel", "arbitrary"),
        name="attention",
    )(q, k, vt, k_meta, vt_meta)


def _gla_scan_kernel(q_ref, k_ref, v_ref, lg_ref, s0_ref, o_ref, sfin_ref, state, u_scr, dec_scr, qe_scr,
                     *, reverse, n_chunks):
    blk = pl.program_id(1)

    @pl.when(blk == 0)
    def _():
        state[...] = s0_ref[...]

    ri = lax.broadcasted_iota(jnp.int32, (GLA_CHUNK, GLA_CHUNK), 0)
    ci = lax.broadcasted_iota(jnp.int32, (GLA_CHUNK, GLA_CHUNK), 1)
    keep = (ri <= ci) if reverse else (ri >= ci)
    csum = jnp.where(keep, 1.0, 0.0).astype(BF16)
    csum2 = jnp.concatenate([csum, csum], axis=1)
    mid = GLA_CHUNK // 2 if reverse else GLA_CHUNK // 2 - 1
    end = 0 if reverse else GLA_CHUNK - 1
    group = math.gcd(n_chunks, 4)

    heads = [(slice(hd * GLA_DK, (hd + 1) * GLA_DK), slice(hd * GLA_DV, (hd + 1) * GLA_DV)) for hd in range(GLA_H)]

    def local(g, carry):
        cs = [g * group + j for j in range(group)]
        rows = [pl.ds(pl.multiple_of(c * GLA_CHUNK, GLA_CHUNK), GLA_CHUNK) for c in cs]
        bcums = []
        for r in rows:
            lg = lg_ref[0, r, :]
            lg_hi = lg.astype(BF16)
            lg_lo = (lg - lg_hi.astype(F32)).astype(BF16)
            bcums.append(_dot(csum2, jnp.concatenate([lg_hi, lg_lo], axis=0)))
        qss, kss, kds = [], [], []
        for c, r, bcum in zip(cs, rows, bcums):
            b_mid = bcum[mid:mid + 1, :]
            b_end = bcum[end:end + 1, :]
            q = q_ref[0, r, :]
            k = k_ref[0, r, :]
            qss.append((q * jnp.exp(bcum - b_mid)).astype(BF16))
            kss.append((k * jnp.exp(b_mid - bcum)).astype(BF16))
            kds.append((k * jnp.exp(b_end - bcum)).astype(BF16))
            qe_scr[c] = (q * jnp.exp(bcum)).astype(BF16)
            dec_scr[c] = jnp.broadcast_to(jnp.exp(b_end), (8, GLA_H * GLA_DK))
        attn = [[_dot_nt(qs[:, kc], ks[:, kc]) for kc, _ in heads] for qs, ks in zip(qss, kss)]
        for c, r, kd in zip(cs, rows, kds):
            for hd, (kc, vc) in enumerate(heads):
                u_scr[c, hd] = _dot_tn(v_ref[0, r, vc], kd[:, kc])
        for r, a_c in zip(rows, attn):
            for (_, vc), a in zip(heads, a_c):
                o_ref[0, r, vc] = _dot(jnp.where(keep, a, 0.0).astype(BF16), v_ref[0, r, vc])
        return carry

    lax.fori_loop(0, n_chunks // group, local, 0)

    def recur(i, carry):
        c = (n_chunks - 1 - i) if reverse else i
        rows = pl.ds(pl.multiple_of(c * GLA_CHUNK, GLA_CHUNK), GLA_CHUNK)
        for hd, (kc, vc) in enumerate(heads):
            st = state[hd]
            o_ref[0, rows, vc] += _dot_nt(qe_scr[c, :, kc], st.astype(BF16))
            state[hd] = st * dec_scr[c, 0:1, kc] + u_scr[c, hd]
        return carry

    lax.fori_loop(0, n_chunks, recur, 0, unroll=group)

    @pl.when(blk == pl.num_programs(1) - 1)
    def _():
        sfin_ref[0] = state[...]


def _gla_scan(q, k, v, lg, s0, reverse, t_blk):
    b, t, _ = q.shape
    nb = t // t_blk
    lg_col = 1 if reverse else 0
    blk = (lambda bi, i: (bi, nb - 1 - i, 0)) if reverse else (lambda bi, i: (bi, i, 0))
    lg_blk = (lambda bi, i: (bi, nb - 1 - i, lg_col)) if reverse else (lambda bi, i: (bi, i, lg_col))
    st_shape = (GLA_H, GLA_DV, GLA_DK)
    n_chunks = t_blk // GLA_CHUNK
    return pl.pallas_call(
        functools.partial(_gla_scan_kernel, reverse=reverse, n_chunks=n_chunks),
        grid=(b, nb),
        in_specs=[
            pl.BlockSpec((1, t_blk, GLA_H * GLA_DK), blk),
            pl.BlockSpec((1, t_blk, GLA_H * GLA_DK), blk),
            pl.BlockSpec((1, t_blk, GLA_H * GLA_DV), blk),
            pl.BlockSpec((1, t_blk, GLA_H * GLA_DK), lg_blk),
            pl.BlockSpec(st_shape, lambda bi, i: (0, 0, 0)),
        ],
        out_specs=[
            pl.BlockSpec((1, t_blk, GLA_H * GLA_DV), blk),
            pl.BlockSpec((1,) + st_shape, lambda bi, i: (bi, 0, 0, 0)),
        ],
        out_shape=[
            jax.ShapeDtypeStruct((b, t, GLA_H * GLA_DV), F32),
            jax.ShapeDtypeStruct((b,) + st_shape, F32),
        ],
        scratch_shapes=[
            pltpu.VMEM(st_shape, F32),
            pltpu.VMEM((n_chunks,) + st_shape, F32),
            pltpu.VMEM((n_chunks, 8, GLA_H * GLA_DK), F32),
            pltpu.VMEM((n_chunks, GLA_CHUNK, GLA_H * GLA_DK), BF16),
        ],
        compiler_params=_params("parallel", "arbitrary"),
        name="gla_scan_bwd" if reverse else "gla_scan_fwd",
    )(q, k, v, lg, s0)


def _mix_ffn_kernel(x_ref, oa_ref, of_ref, ob_ref, og_ref, gate_ref, womla_ref, onorm_ref, wogla_ref,
                    wout_ref, fn_ref, wgate_ref, wup_ref, wdown_ref, y_ref):
    y_a = _dot(oa_ref[...], womla_ref[...])
    o = of_ref[...] + ob_ref[...]
    og = og_ref[...]
    parts = []
    for hd in range(GLA_H):
        cols = slice(hd * GLA_DV, (hd + 1) * GLA_DV)
        parts.append((_rms(o[:, cols]) * onorm_ref[...] * og[:, cols]).astype(BF16))
    y_b = _dot(jnp.concatenate(parts, axis=-1), wogla_ref[...])
    gates = gate_ref[...]
    mixed = (gates[:, :D_MODEL] * y_a + gates[:, D_MODEL:] * y_b).astype(BF16)
    x1 = x_ref[...] + _dot(mixed, wout_ref[...])
    h = (_rms(x1) * fn_ref[...]).astype(BF16)
    g = _dot(h, wgate_ref[...])
    u = _dot(h, wup_ref[...])
    act = (g * jax.nn.sigmoid(g) * u).astype(BF16)
    y_ref[...] = x1 + _dot(act, wdown_ref[...])


def _mix_ffn(x2d, o_attn, o_f, o_b, og, gates, w, tm):
    n = x2d.shape[0]
    row = lambda i: (i, 0)
    return pl.pallas_call(
        _mix_ffn_kernel,
        grid=(n // tm,),
        in_specs=[
            pl.BlockSpec((tm, D_MODEL), row),
            pl.BlockSpec((tm, MLA_H * V_DIM), row),
            pl.BlockSpec((tm, GLA_H * GLA_DV), row),
            pl.BlockSpec((tm, GLA_H * GLA_DV), row),
            pl.BlockSpec((tm, GLA_H * GLA_DV), row),
            pl.BlockSpec((tm, 2 * D_MODEL), row),
            _resident(w["w_o_mla"].shape),
            _resident((1, GLA_DV)),
            _resident(w["w_o_gla"].shape),
            _resident(w["w_out"].shape),
            _resident((1, D_MODEL)),
            _resident(w["w_ffn_gate"].shape),
            _resident(w["w_ffn_up"].shape),
            _resident(w["w_ffn_down"].shape),
        ],
        out_specs=pl.BlockSpec((tm, D_MODEL), row),
        out_shape=jax.ShapeDtypeStruct((n, D_MODEL), F32),
        compiler_params=_params("parallel"),
        name="mix_ffn",
    )(x2d, o_attn, o_f, o_b, og, gates, w["w_o_mla"], w["gla_o_norm"], w["w_o_gla"], w["w_out"],
      w["ffn_norm"], w["w_ffn_gate"], w["w_ffn_up"], w["w_ffn_down"])


def _prep_weights(attn_norm, w_in, q_a_norm, w_uq, kv_a_norm, w_ukv, q_norm, k_norm, w_o_mla,
                  w_a2_fwd, b_a2_fwd, w_a2_bwd, b_a2_bwd, gla_o_norm, w_o_gla, w_out, ffn_norm,
                  w_ffn_gate, w_ffn_up, w_ffn_down):
    half = ROPE_DIM // 2
    o = 0
    cols = {}
    for name, size in (("cq", Q_LORA), ("ckv", KV_LORA), ("kr", ROPE_DIM), ("gq", GLA_H * GLA_DK),
                       ("gk", GLA_H * GLA_DK), ("gv", GLA_H * GLA_DV), ("gg", GLA_H * GLA_DV),
                       ("af", GATE_RANK), ("ab", GATE_RANK), ("ga", D_MODEL), ("gb", D_MODEL)):
        cols[name] = w_in[:, o:o + size]
        o += size
    kr1, kr2 = cols["kr"][:, :half], cols["kr"][:, half:]
    w_ckvr = jnp.concatenate([cols["ckv"], kr1, kr2, kr1, kr2, kr2, kr1, kr2, kr1], axis=1)

    uq = w_uq.reshape(Q_LORA, MLA_H, QK_DIM)
    uq_nope = uq[:, :, :NOPE_DIM].reshape(Q_LORA, MLA_H * NOPE_DIM)
    uq_r = uq[:, :, NOPE_DIM:]
    uq_a = uq_r.reshape(Q_LORA, MLA_H * ROPE_DIM)
    uq_b = jnp.concatenate([uq_r[:, :, half:], uq_r[:, :, :half]], axis=-1).reshape(Q_LORA, MLA_H * ROPE_DIM)
    ukv = w_ukv.reshape(KV_LORA, MLA_H, NOPE_DIM + V_DIM)
    w_ukv2 = jnp.concatenate([ukv[:, :, :NOPE_DIM].reshape(KV_LORA, MLA_H * NOPE_DIM),
                              ukv[:, :, NOPE_DIM:].reshape(KV_LORA, MLA_H * V_DIM)], axis=1)

    def gain_rows(g):
        z = jnp.zeros((ROPE_DIM,), F32)
        ga = g[NOPE_DIM:]
        gb = jnp.concatenate([g[NOPE_DIM + half:], g[NOPE_DIM:NOPE_DIM + half]])
        rows = [g[:NOPE_DIM], jnp.concatenate([ga, z]), jnp.concatenate([z, ga]),
                jnp.concatenate([gb, z]), jnp.concatenate([z, gb])]
        rows += [jnp.zeros((LANES,), F32)] * 3
        return jnp.stack(rows)

    zpad = jnp.zeros((D_MODEL, LANES - 2 * GATE_RANK), F32)
    w_g = jnp.concatenate([cols["gq"], cols["gk"], cols["gv"], cols["gg"], cols["af"], cols["ab"], zpad,
                           cols["ga"], cols["gb"]], axis=1)
    nk = GLA_H * GLA_DK
    w_a2 = jnp.zeros((LANES, 2 * nk), F32)
    w_a2 = w_a2.at[:GATE_RANK, :nk].set(w_a2_fwd).at[GATE_RANK:2 * GATE_RANK, nk:].set(w_a2_bwd)
    return {
        "attn_norm": attn_norm[None], "w_cq": cols["cq"].astype(BF16), "w_ckvr": w_ckvr.astype(BF16),
        "q_a_norm": q_a_norm[None], "w_uq": jnp.concatenate([uq_nope, uq_a, uq_b], axis=1).astype(BF16),
        "kv_a_norm": kv_a_norm[None], "w_ukv": w_ukv2.astype(BF16),
        "g_q": gain_rows(q_norm), "g_k": gain_rows(k_norm),
        "w_g": w_g.astype(BF16), "w_a2": w_a2.astype(BF16),
        "b_a2": jnp.concatenate([b_a2_fwd, b_a2_bwd])[None],
        "w_o_mla": w_o_mla.astype(BF16), "gla_o_norm": gla_o_norm[None], "w_o_gla": w_o_gla.astype(BF16),
        "w_out": w_out.astype(BF16), "ffn_norm": ffn_norm[None], "w_ffn_gate": w_ffn_gate.astype(BF16),
        "w_ffn_up": w_ffn_up.astype(BF16), "w_ffn_down": w_ffn_down.astype(BF16),
    }


def _rope_tables(length):
    inv = 1.0 / (ROPE_THETA ** (jnp.arange(0, ROPE_DIM, 2, dtype=F32) / ROPE_DIM))
    ang = jnp.arange(length, dtype=F32)[:, None] * inv[None, :]
    cos, sin = jnp.cos(ang), jnp.sin(ang)
    return jnp.tile(cos, (1, 4)), jnp.concatenate([-sin, sin, -sin, sin], axis=1)


def _encode_group(x, w, meta, cos_t, sin_t):
    b, seq, _ = x.shape
    n = b * seq
    x2d = x.reshape(n, D_MODEL)
    tm = _row_tile(seq, 256)
    q, k, vt, gq, gk, gv, og, lg, gates = _project(
        x2d, w, cos_t[N_META:N_META + seq], sin_t[N_META:N_META + seq], seq, tm, True)

    o_attn = _attention(q.reshape(b, seq, -1), k.reshape(b, seq, -1), vt,
                        meta["k"], meta["vt"], _row_tile(seq, 1024), _row_tile(seq // 8, 2048))

    t_blk = _row_tile(seq, 512)
    g3 = lambda a: a.reshape(b, seq, -1)
    o_f, _ = _gla_scan(g3(gq), g3(gk), g3(gv), g3(lg), meta["state"], False, t_blk)
    o_b, _ = _gla_scan(g3(gq), g3(gk), g3(gv), g3(lg), jnp.zeros_like(meta["state"]), True, t_blk)

    y = _mix_ffn(x2d, o_attn.reshape(n, -1), o_f.reshape(n, -1), o_b.reshape(n, -1), og, gates, w, tm)
    return y.reshape(b, seq, D_MODEL)


def kernel(x_prompt, x_sample, meta_tokens, attn_norm, w_in, q_a_norm, w_uq, kv_a_norm, w_ukv, q_norm, k_norm, w_o_mla, w_a2_fwd, b_a2_fwd, w_a2_bwd, b_a2_bwd, gla_o_norm, w_o_gla, w_out, ffn_norm, w_ffn_gate, w_ffn_up, w_ffn_down):
    assert attn_norm.shape[0] == 1, "single-layer encoder"
    w = _prep_weights(attn_norm[0], w_in[0], q_a_norm[0], w_uq[0], kv_a_norm[0], w_ukv[0], q_norm[0],
                      k_norm[0], w_o_mla[0], w_a2_fwd[0], b_a2_fwd[0], w_a2_bwd[0], b_a2_bwd[0],
                      gla_o_norm[0], w_o_gla[0], w_out[0], ffn_norm[0], w_ffn_gate[0], w_ffn_up[0],
                      w_ffn_down[0])
    max_len = N_META + max(x_prompt.shape[1], x_sample.shape[1])
    cos_t, sin_t = _rope_tables(max_len)

    xm = meta_tokens.astype(F32)
    _, k_m, v_m, mq, mk, mv, _, mlg, _ = _project(xm, w, cos_t[:N_META], sin_t[:N_META], N_META, N_META, False)
    pad = lambda a: jnp.pad(a, ((GLA_CHUNK - N_META, 0), (0, 0)))[None]
    _, s_meta = _gla_scan(pad(mq), pad(mk), pad(mv), pad(mlg),
                          jnp.zeros((GLA_H, GLA_DV, GLA_DK), F32), False, GLA_CHUNK)
    vt_m = jnp.concatenate([v_m.T.reshape(MLA_H, V_DIM, N_META), jnp.ones((MLA_H, ONES_ROWS, N_META), BF16)],
                           axis=1).reshape(MLA_H * VT_ROWS, N_META)
    meta = {"k": k_m, "vt": vt_m, "state": s_meta[0]}

    return (_encode_group(x_prompt, w, meta, cos_t, sin_t),
            _encode_group(x_sample, w, meta, cos_t, sin_t))
```

```python
import functools
import math

import jax
import jax.numpy as jnp
from jax import lax
from jax.experimental import pallas as pl
from jax.experimental.pallas import tpu as pltpu

D_MODEL = 1024
N_META = 16
MLA_H = 8
NOPE_DIM = 128
ROPE_DIM = 64
QK_DIM = NOPE_DIM + ROPE_DIM
V_DIM = 128
Q_LORA = 768
KV_LORA = 256
ROPE_THETA = 10000.0
GLA_H = 4
GLA_DK = 128
GLA_DV = 256
GATE_RANK = 16
GATE_TEMP = 16.0
GLA_CHUNK = 64
D_FF = 2816
NORM_EPS = 1e-6

QK_PAD = 256
ONES_ROWS = 16
VT_ROWS = V_DIM + ONES_ROWS
LANES = 128
MXU_DIM = 256
LOG2E = 1.4426950408889634
VMEM_LIMIT = 56 * 1024 * 1024

F32 = jnp.float32
BF16 = jnp.bfloat16


def _dot(a, b):
    return jnp.dot(a, b, preferred_element_type=F32)


def _dot_nt(a, b):
    return lax.dot_general(a, b, (((1,), (1,)), ((), ())), preferred_element_type=F32)


def _dot_tn(a, b):
    return lax.dot_general(a, b, (((0,), (0,)), ((), ())), preferred_element_type=F32)


def _rms(x):
    return x * lax.rsqrt(jnp.mean(x * x, axis=-1, keepdims=True) + NORM_EPS)


def _resident(shape):
    return pl.BlockSpec(shape, lambda *_: (0,) * len(shape), pipeline_mode=pl.Buffered(1))


def _params(*sem):
    return pltpu.CompilerParams(dimension_semantics=sem, vmem_limit_bytes=VMEM_LIMIT)


def _row_tile(n, want):
    t = min(n, want)
    assert n % t == 0, (n, t)
    return t


G_Q0, G_K0, G_V0, G_G0, G_A0, G_S0, G_END = 0, 512, 1024, 2048, 3072, 3200, 5248


def _proj_kernel(x_ref, an_ref, wcq_ref, wckv_ref, qan_ref, wuq_ref, kvan_ref, wukv_ref, gq_ref, gk_ref,
                 cos_ref, sin_ref, wg_ref, wa2_ref, ba2_ref,
                 q_out, k_out, v_out, gla_q_out, gla_k_out, gla_v_out, og_out, lg_out, gate_out, *, transpose_v):
    h = (_rms(x_ref[...]) * an_ref[...]).astype(BF16)
    cq = _dot(h, wcq_ref[...])
    ckvr = _dot(h, wckv_ref[...])
    gla_q_out[...] = _dot(h, wg_ref[:, G_Q0:G_K0]) * (GLA_DK ** -0.5)
    gla_k_out[...] = _dot(h, wg_ref[:, G_K0:G_V0])
    cqn = (_rms(cq) * qan_ref[...]).astype(BF16)
    ckvn = (_rms(ckvr[:, :KV_LORA]) * kvan_ref[...]).astype(BF16)
    qall = _dot(cqn, wuq_ref[...])
    kv = _dot(ckvn, wukv_ref[...])
    gla_v_out[...] = _dot(h, wg_ref[:, G_V0:G_G0]).astype(BF16)
    g = _dot(h, wg_ref[:, G_G0:G_A0])
    og_out[...] = (g * jax.nn.sigmoid(g)).astype(og_out.dtype)
    a = _dot(h, wg_ref[:, G_A0:G_S0]).astype(BF16)
    pre = _dot(a, wa2_ref[...]) + ba2_ref[...]
    lg_out[...] = jax.nn.log_sigmoid(pre) * (1.0 / GATE_TEMP)
    gate_out[...] = jax.nn.sigmoid(_dot(h, wg_ref[:, G_S0:G_END])).astype(gate_out.dtype)

    v = kv[:, MLA_H * NOPE_DIM:]
    if transpose_v:
        vt = v.T.astype(BF16)
        ones = jnp.ones((ONES_ROWS, v.shape[0]), BF16)
        for hd in range(MLA_H):
            v_out[hd * VT_ROWS:hd * VT_ROWS + V_DIM, :] = vt[hd * V_DIM:(hd + 1) * V_DIM, :]
            v_out[hd * VT_ROWS + V_DIM:(hd + 1) * VT_ROWS, :] = ones
    else:
        v_out[...] = v.astype(BF16)

    cos = cos_ref[...]
    sin = sin_ref[...]
    lane = lax.broadcasted_iota(jnp.int32, (1, LANES), 1)
    ka = ckvr[:, KV_LORA:KV_LORA + LANES]
    kb = ckvr[:, KV_LORA + LANES:KV_LORA + 2 * LANES]
    head_lanes = [((lane >= p * ROPE_DIM) & (lane < (p + 1) * ROPE_DIM)).astype(F32) for p in range(2)]
    ssq_kr = jnp.sum(ka * ka * head_lanes[0], axis=-1, keepdims=True)
    k_rot = [(ka * gk_ref[1 + p:2 + p, :]) * cos + (kb * gk_ref[3 + p:4 + p, :]) * sin for p in range(2)]
    q_scale = QK_DIM ** -0.5 * LOG2E
    rope0 = MLA_H * NOPE_DIM
    for hd in range(MLA_H):
        j, p = divmod(hd, 2)
        qn = qall[:, hd * NOPE_DIM:(hd + 1) * NOPE_DIM]
        qa = qall[:, rope0 + j * LANES:rope0 + (j + 1) * LANES]
        qb = qall[:, rope0 + 4 * LANES + j * LANES:rope0 + 4 * LANES + (j + 1) * LANES]
        ssq = (jnp.sum(qn * qn, axis=-1, keepdims=True)
               + jnp.sum(qa * qa * head_lanes[p], axis=-1, keepdims=True))
        rs = lax.rsqrt(ssq * (1.0 / QK_DIM) + NORM_EPS) * q_scale
        q_rot = (qa * gq_ref[1 + p:2 + p, :]) * cos + (qb * gq_ref[3 + p:4 + p, :]) * sin
        q_out[:, hd * QK_PAD:hd * QK_PAD + LANES] = (qn * gq_ref[0:1, :] * rs).astype(BF16)
        q_out[:, hd * QK_PAD + LANES:(hd + 1) * QK_PAD] = (q_rot * rs).astype(BF16)
        kn = kv[:, hd * NOPE_DIM:(hd + 1) * NOPE_DIM]
        rsk = lax.rsqrt((jnp.sum(kn * kn, axis=-1, keepdims=True) + ssq_kr) * (1.0 / QK_DIM) + NORM_EPS)
        k_out[:, hd * QK_PAD:hd * QK_PAD + LANES] = (kn * gk_ref[0:1, :] * rsk).astype(BF16)
        k_out[:, hd * QK_PAD + LANES:(hd + 1) * QK_PAD] = (k_rot[p] * rsk).astype(BF16)


def _project(x2d, w, cos_t, sin_t, seq_len, tm, transpose_v):
    n = x2d.shape[0]
    nt = seq_len // tm
    row = lambda i: (i, 0)
    pos = lambda i: (i % nt, 0)
    v_dims = MLA_H * V_DIM
    vt_rows = MLA_H * VT_ROWS
    nk, nv = GLA_H * GLA_DK, GLA_H * GLA_DV
    gla_outs = ((nk, F32), (nk, F32), (nv, BF16), (nv, BF16), (2 * nk, F32), (2 * D_MODEL, BF16))
    return pl.pallas_call(
        functools.partial(_proj_kernel, transpose_v=transpose_v),
        grid=(n // tm,),
        in_specs=[
            pl.BlockSpec((tm, D_MODEL), row),
            _resident((1, D_MODEL)),
            _resident(w["w_cq"].shape),
            _resident(w["w_ckvr"].shape),
            _resident((1, Q_LORA)),
            _resident(w["w_uq"].shape),
            _resident((1, KV_LORA)),
            _resident(w["w_ukv"].shape),
            _resident((8, LANES)),
            _resident((8, LANES)),
            pl.BlockSpec((tm, LANES), pos),
            pl.BlockSpec((tm, LANES), pos),
            _resident(w["w_g"].shape),
            _resident(w["w_a2"].shape),
            _resident((1, 2 * nk)),
        ],
        out_specs=[
            pl.BlockSpec((tm, MLA_H * QK_PAD), row),
            pl.BlockSpec((tm, MLA_H * QK_PAD), row),
            pl.BlockSpec((vt_rows, tm), lambda i: (0, i)) if transpose_v else pl.BlockSpec((tm, v_dims), row),
        ] + [pl.BlockSpec((tm, width), row) for width, _ in gla_outs],
        out_shape=[
            jax.ShapeDtypeStruct((n, MLA_H * QK_PAD), BF16),
            jax.ShapeDtypeStruct((n, MLA_H * QK_PAD), BF16),
            jax.ShapeDtypeStruct((vt_rows, n) if transpose_v else (n, v_dims), BF16),
        ] + [jax.ShapeDtypeStruct((n, width), dtype) for width, dtype in gla_outs],
        compiler_params=_params("parallel"),
        name="project",
    )(x2d, w["attn_norm"], w["w_cq"], w["w_ckvr"], w["q_a_norm"], w["w_uq"], w["kv_a_norm"],
      w["w_ukv"], w["g_q"], w["g_k"], cos_t, sin_t, w["w_g"], w["w_a2"], w["b_a2"])


def _attn_kernel(q_ref, qn_ref, k_ref, vt_ref, km_ref, vmt_ref, o_ref,
                 s_even, s_odd, acc_scr, qt_scr, qtn_scr, max_scr, *, tk, ck, n_tiles):
    qi = pl.program_id(2)
    n_q = pl.num_programs(2)
    s_scr = (s_even, s_odd)
    n_chunks = tk // ck

    def transposed(ref):
        return ref[0].astype(F32).T.astype(BF16)

    def score_chunk(i, c, slot, running_max, qt_ref):
        off = pl.multiple_of(i * tk + c * ck, ck)
        s = _dot(k_ref[0, pl.ds(off, ck), :], qt_ref[...])
        s_scr[slot][c * ck:(c + 1) * ck, :] = s
        s_max = jnp.max(s, axis=0, keepdims=True)
        return s_max if running_max is None else jnp.maximum(running_max, s_max)

    def gather_chunk(i, c, slot, m, partial):
        off = pl.multiple_of(i * tk + c * ck, ck)
        p = jnp.exp2(s_scr[slot][c * ck:(c + 1) * ck, :] - m).astype(BF16)
        d = _dot(vt_ref[:, pl.ds(off, ck)], p)
        return d if partial is None else partial + d

    def step(i, slot, s_max, m, following="tile"):
        m_new = jnp.maximum(m, s_max)
        next_max, partial = None, None
        for c in range(n_chunks):
            if following == "tile":
                next_max = score_chunk(i + 1, c, 1 - slot, next_max, qt_scr)
            elif following == "block":
                next_max = score_chunk(0, c, 1 - slot, next_max, qtn_scr)
            partial = gather_chunk(i, c, slot, m_new, partial)
        acc_scr[...] = jnp.exp2(m - m_new) * acc_scr[...] + partial
        return next_max, m_new

    def two_steps(j, carry):
        carry = step(2 * j, 0, *carry)
        return step(2 * j + 1, 1, *carry)

    @pl.when(qi == 0)
    def _():
        qt_scr[...] = transposed(q_ref)
        max0 = None
        for c in range(n_chunks):
            max0 = score_chunk(0, c, 0, max0, qt_scr)
        max_scr[...] = max0

    s_meta = _dot(km_ref[...], qt_scr[...])
    m_meta = jnp.max(s_meta, axis=0, keepdims=True)
    acc_scr[...] = _dot(vmt_ref[...], jnp.exp2(s_meta - m_meta).astype(BF16))
    carry = lax.fori_loop(0, n_tiles // 2 - 1, two_steps, (max_scr[...], m_meta))
    max_last, m = step(n_tiles - 2, 0, *carry)

    @pl.when(qi < n_q - 1)
    def _():
        qtn_scr[...] = transposed(qn_ref)
        max_scr[...] = step(n_tiles - 1, 1, max_last, m, following="block")[0]
        qt_scr[...] = qtn_scr[...]

    @pl.when(qi == n_q - 1)
    def _():
        step(n_tiles - 1, 1, max_last, m, following=None)

    acc = acc_scr[...]
    o_ref[0] = (acc[:V_DIM] / acc[V_DIM:V_DIM + 1]).T.astype(o_ref.dtype)


def _attention(q, k, vt, k_meta, vt_meta, tq, tk):
    b, seq, _ = q.shape
    assert seq % (2 * tk) == 0, (seq, tk)
    n_q = seq // tq
    return pl.pallas_call(
        functools.partial(_attn_kernel, tk=tk, ck=min(tk, MXU_DIM), n_tiles=seq // tk),
        grid=(b, MLA_H, n_q),
        in_specs=[
            pl.BlockSpec((1, tq, QK_PAD), lambda bi, hi, qi: (bi, qi, hi)),
            pl.BlockSpec((1, tq, QK_PAD), lambda bi, hi, qi: (bi, jnp.minimum(qi + 1, n_q - 1), hi)),
            pl.BlockSpec((1, seq, QK_PAD), lambda bi, hi, qi: (bi, 0, hi)),
            pl.BlockSpec((VT_ROWS, seq), lambda bi, hi, qi: (hi, bi)),
            pl.BlockSpec((N_META, QK_PAD), lambda bi, hi, qi: (0, hi)),
            pl.BlockSpec((VT_ROWS, N_META), lambda bi, hi, qi: (hi, 0)),
        ],
        out_specs=pl.BlockSpec((1, tq, V_DIM), lambda bi, hi, qi: (bi, qi, hi)),
        out_shape=jax.ShapeDtypeStruct((b, seq, MLA_H * V_DIM), BF16),
        scratch_shapes=[
            pltpu.VMEM((tk, tq), F32),
            pltpu.VMEM((tk, tq), F32),
            pltpu.VMEM((VT_ROWS, tq), F32),
            pltpu.VMEM((QK_PAD, tq), BF16),
            pltpu.VMEM((QK_PAD, tq), BF16),
            pltpu.VMEM((1, tq), F32),
        ],
        compiler_params=_params("arbitrary", "arbitrary", "arbitrary"),
        name="attention",
    )(q, q, k, vt, k_meta, vt_meta)


def _gla_scan_kernel(q_ref, k_ref, v_ref, lg_ref, s0_ref, o_ref, sfin_ref, state, u_scr, dec_scr, qe_scr,
                     *, reverse, n_chunks):
    blk = pl.program_id(1)

    @pl.when(blk == 0)
    def _():
        state[...] = s0_ref[...]

    ri = lax.broadcasted_iota(jnp.int32, (GLA_CHUNK, GLA_CHUNK), 0)
    ci = lax.broadcasted_iota(jnp.int32, (GLA_CHUNK, GLA_CHUNK), 1)
    keep = (ri <= ci) if reverse else (ri >= ci)
    csum = jnp.where(keep, 1.0, 0.0).astype(BF16)
    csum2 = jnp.concatenate([csum, csum], axis=1)
    mid = GLA_CHUNK // 2 if reverse else GLA_CHUNK // 2 - 1
    end = 0 if reverse else GLA_CHUNK - 1
    group = math.gcd(n_chunks, 4)

    heads = [(slice(hd * GLA_DK, (hd + 1) * GLA_DK), slice(hd * GLA_DV, (hd + 1) * GLA_DV)) for hd in range(GLA_H)]

    def local(g, carry):
        cs = [g * group + j for j in range(group)]
        rows = [pl.ds(pl.multiple_of(c * GLA_CHUNK, GLA_CHUNK), GLA_CHUNK) for c in cs]
        bcums = []
        for r in rows:
            lg = lg_ref[0, r, :]
            lg_hi = lg.astype(BF16)
            lg_lo = (lg - lg_hi.astype(F32)).astype(BF16)
            bcums.append(_dot(csum2, jnp.concatenate([lg_hi, lg_lo], axis=0)))
        qss, kss, kds = [], [], []
        for c, r, bcum in zip(cs, rows, bcums):
            b_mid = bcum[mid:mid + 1, :]
            b_end = bcum[end:end + 1, :]
            q = q_ref[0, r, :]
            k = k_ref[0, r, :]
            qss.append((q * jnp.exp(bcum - b_mid)).astype(BF16))
            kss.append((k * jnp.exp(b_mid - bcum)).astype(BF16))
            kds.append((k * jnp.exp(b_end - bcum)).astype(BF16))
            qe_scr[c] = (q * jnp.exp(bcum)).astype(BF16)
            dec_scr[c] = jnp.broadcast_to(jnp.exp(b_end), (8, GLA_H * GLA_DK))
        attn = [[_dot_nt(qs[:, kc], ks[:, kc]) for kc, _ in heads] for qs, ks in zip(qss, kss)]
        for c, r, kd in zip(cs, rows, kds):
            for hd, (kc, vc) in enumerate(heads):
                u_scr[c, hd] = _dot_tn(v_ref[0, r, vc], kd[:, kc])
        for r, a_c in zip(rows, attn):
            for (_, vc), a in zip(heads, a_c):
                o_ref[0, r, vc] = _dot(jnp.where(keep, a, 0.0).astype(BF16), v_ref[0, r, vc])
        return carry

    lax.fori_loop(0, n_chunks // group, local, 0)

    def recur(i, carry):
        c = (n_chunks - 1 - i) if reverse else i
        rows = pl.ds(pl.multiple_of(c * GLA_CHUNK, GLA_CHUNK), GLA_CHUNK)
        for hd, (kc, vc) in enumerate(heads):
            st = state[hd]
            o_ref[0, rows, vc] += _dot_nt(qe_scr[c, :, kc], st.astype(BF16))
            state[hd] = st * dec_scr[c, 0:1, kc] + u_scr[c, hd]
        return carry

    lax.fori_loop(0, n_chunks, recur, 0, unroll=group)

    @pl.when(blk == pl.num_programs(1) - 1)
    def _():
        sfin_ref[0] = state[...]


def _gla_scan(q, k, v, lg, s0, reverse, t_blk):
    b, t, _ = q.shape
    nb = t // t_blk
    lg_col = 1 if reverse else 0
    blk = (lambda bi, i: (bi, nb - 1 - i, 0)) if reverse else (lambda bi, i: (bi, i, 0))
    lg_blk = (lambda bi, i: (bi, nb - 1 - i, lg_col)) if reverse else (lambda bi, i: (bi, i, lg_col))
    st_shape = (GLA_H, GLA_DV, GLA_DK)
    n_chunks = t_blk // GLA_CHUNK
    return pl.pallas_call(
        functools.partial(_gla_scan_kernel, reverse=reverse, n_chunks=n_chunks),
        grid=(b, nb),
        in_specs=[
            pl.BlockSpec((1, t_blk, GLA_H * GLA_DK), blk),
            pl.BlockSpec((1, t_blk, GLA_H * GLA_DK), blk),
            pl.BlockSpec((1, t_blk, GLA_H * GLA_DV), blk),
            pl.BlockSpec((1, t_blk, GLA_H * GLA_DK), lg_blk),
            pl.BlockSpec(st_shape, lambda bi, i: (0, 0, 0)),
        ],
        out_specs=[
            pl.BlockSpec((1, t_blk, GLA_H * GLA_DV), blk),
            pl.BlockSpec((1,) + st_shape, lambda bi, i: (bi, 0, 0, 0)),
        ],
        out_shape=[
            jax.ShapeDtypeStruct((b, t, GLA_H * GLA_DV), F32),
            jax.ShapeDtypeStruct((b,) + st_shape, F32),
        ],
        scratch_shapes=[
            pltpu.VMEM(st_shape, F32),
            pltpu.VMEM((n_chunks,) + st_shape, F32),
            pltpu.VMEM((n_chunks, 8, GLA_H * GLA_DK), F32),
            pltpu.VMEM((n_chunks, GLA_CHUNK, GLA_H * GLA_DK), BF16),
        ],
        compiler_params=_params("parallel", "arbitrary"),
        name="gla_scan_bwd" if reverse else "gla_scan_fwd",
    )(q, k, v, lg, s0)


def _mix_ffn_kernel(x_ref, oa_ref, of_ref, ob_ref, og_ref, gate_ref, womla_ref, onorm_ref, wogla_ref,
                    wout_ref, fn_ref, wgate_ref, wup_ref, wdown_ref, y_ref):
    y_a = _dot(oa_ref[...], womla_ref[...])
    o = of_ref[...] + ob_ref[...]
    og = og_ref[...]
    parts = []
    for hd in range(GLA_H):
        cols = slice(hd * GLA_DV, (hd + 1) * GLA_DV)
        parts.append((_rms(o[:, cols]) * onorm_ref[...] * og[:, cols]).astype(BF16))
    y_b = _dot(jnp.concatenate(parts, axis=-1), wogla_ref[...])
    gates = gate_ref[...]
    mixed = (gates[:, :D_MODEL] * y_a + gates[:, D_MODEL:] * y_b).astype(BF16)
    x1 = x_ref[...] + _dot(mixed, wout_ref[...])
    h = (_rms(x1) * fn_ref[...]).astype(BF16)
    g = _dot(h, wgate_ref[...])
    u = _dot(h, wup_ref[...])
    act = (g * jax.nn.sigmoid(g) * u).astype(BF16)
    y_ref[...] = x1 + _dot(act, wdown_ref[...])


def _mix_ffn(x2d, o_attn, o_f, o_b, og, gates, w, tm):
    n = x2d.shape[0]
    row = lambda i: (i, 0)
    return pl.pallas_call(
        _mix_ffn_kernel,
        grid=(n // tm,),
        in_specs=[
            pl.BlockSpec((tm, D_MODEL), row),
            pl.BlockSpec((tm, MLA_H * V_DIM), row),
            pl.BlockSpec((tm, GLA_H * GLA_DV), row),
            pl.BlockSpec((tm, GLA_H * GLA_DV), row),
            pl.BlockSpec((tm, GLA_H * GLA_DV), row),
            pl.BlockSpec((tm, 2 * D_MODEL), row),
            _resident(w["w_o_mla"].shape),
            _resident((1, GLA_DV)),
            _resident(w["w_o_gla"].shape),
            _resident(w["w_out"].shape),
            _resident((1, D_MODEL)),
            _resident(w["w_ffn_gate"].shape),
            _resident(w["w_ffn_up"].shape),
            _resident(w["w_ffn_down"].shape),
        ],
        out_specs=pl.BlockSpec((tm, D_MODEL), row),
        out_shape=jax.ShapeDtypeStruct((n, D_MODEL), F32),
        compiler_params=_params("parallel"),
        name="mix_ffn",
    )(x2d, o_attn, o_f, o_b, og, gates, w["w_o_mla"], w["gla_o_norm"], w["w_o_gla"], w["w_out"],
      w["ffn_norm"], w["w_ffn_gate"], w["w_ffn_up"], w["w_ffn_down"])


def _prep_weights(attn_norm, w_in, q_a_norm, w_uq, kv_a_norm, w_ukv, q_norm, k_norm, w_o_mla,
                  w_a2_fwd, b_a2_fwd, w_a2_bwd, b_a2_bwd, gla_o_norm, w_o_gla, w_out, ffn_norm,
                  w_ffn_gate, w_ffn_up, w_ffn_down):
    half = ROPE_DIM // 2
    o = 0
    cols = {}
    for name, size in (("cq", Q_LORA), ("ckv", KV_LORA), ("kr", ROPE_DIM), ("gq", GLA_H * GLA_DK),
                       ("gk", GLA_H * GLA_DK), ("gv", GLA_H * GLA_DV), ("gg", GLA_H * GLA_DV),
                       ("af", GATE_RANK), ("ab", GATE_RANK), ("ga", D_MODEL), ("gb", D_MODEL)):
        cols[name] = w_in[:, o:o + size]
        o += size
    kr1, kr2 = cols["kr"][:, :half], cols["kr"][:, half:]
    w_ckvr = jnp.concatenate([cols["ckv"], kr1, kr2, kr1, kr2, kr2, kr1, kr2, kr1], axis=1)

    uq = w_uq.reshape(Q_LORA, MLA_H, QK_DIM)
    uq_nope = uq[:, :, :NOPE_DIM].reshape(Q_LORA, MLA_H * NOPE_DIM)
    uq_r = uq[:, :, NOPE_DIM:]
    uq_a = uq_r.reshape(Q_LORA, MLA_H * ROPE_DIM)
    uq_b = jnp.concatenate([uq_r[:, :, half:], uq_r[:, :, :half]], axis=-1).reshape(Q_LORA, MLA_H * ROPE_DIM)
    ukv = w_ukv.reshape(KV_LORA, MLA_H, NOPE_DIM + V_DIM)
    w_ukv2 = jnp.concatenate([ukv[:, :, :NOPE_DIM].reshape(KV_LORA, MLA_H * NOPE_DIM),
                              ukv[:, :, NOPE_DIM:].reshape(KV_LORA, MLA_H * V_DIM)], axis=1)

    def gain_rows(g):
        z = jnp.zeros((ROPE_DIM,), F32)
        ga = g[NOPE_DIM:]
        gb = jnp.concatenate([g[NOPE_DIM + half:], g[NOPE_DIM:NOPE_DIM + half]])
        rows = [g[:NOPE_DIM], jnp.concatenate([ga, z]), jnp.concatenate([z, ga]),
                jnp.concatenate([gb, z]), jnp.concatenate([z, gb])]
        rows += [jnp.zeros((LANES,), F32)] * 3
        return jnp.stack(rows)

    zpad = jnp.zeros((D_MODEL, LANES - 2 * GATE_RANK), F32)
    w_g = jnp.concatenate([cols["gq"], cols["gk"], cols["gv"], cols["gg"], cols["af"], cols["ab"], zpad,
                           cols["ga"], cols["gb"]], axis=1)
    nk = GLA_H * GLA_DK
    w_a2 = jnp.zeros((LANES, 2 * nk), F32)
    w_a2 = w_a2.at[:GATE_RANK, :nk].set(w_a2_fwd).at[GATE_RANK:2 * GATE_RANK, nk:].set(w_a2_bwd)
    return {
        "attn_norm": attn_norm[None], "w_cq": cols["cq"].astype(BF16), "w_ckvr": w_ckvr.astype(BF16),
        "q_a_norm": q_a_norm[None], "w_uq": jnp.concatenate([uq_nope, uq_a, uq_b], axis=1).astype(BF16),
        "kv_a_norm": kv_a_norm[None], "w_ukv": w_ukv2.astype(BF16),
        "g_q": gain_rows(q_norm), "g_k": gain_rows(k_norm),
        "w_g": w_g.astype(BF16), "w_a2": w_a2.astype(BF16),
        "b_a2": jnp.concatenate([b_a2_fwd, b_a2_bwd])[None],
        "w_o_mla": w_o_mla.astype(BF16), "gla_o_norm": gla_o_norm[None], "w_o_gla": w_o_gla.astype(BF16),
        "w_out": w_out.astype(BF16), "ffn_norm": ffn_norm[None], "w_ffn_gate": w_ffn_gate.astype(BF16),
        "w_ffn_up": w_ffn_up.astype(BF16), "w_ffn_down": w_ffn_down.astype(BF16),
    }


def _rope_tables(length):
    inv = 1.0 / (ROPE_THETA ** (jnp.arange(0, ROPE_DIM, 2, dtype=F32) / ROPE_DIM))
    ang = jnp.arange(length, dtype=F32)[:, None] * inv[None, :]
    cos, sin = jnp.cos(ang), jnp.sin(ang)
    return jnp.tile(cos, (1, 4)), jnp.concatenate([-sin, sin, -sin, sin], axis=1)


def _encode_group(x, w, meta, cos_t, sin_t):
    b, seq, _ = x.shape
    n = b * seq
    x2d = x.reshape(n, D_MODEL)
    tm = _row_tile(seq, 256)
    q, k, vt, gq, gk, gv, og, lg, gates = _project(
        x2d, w, cos_t[N_META:N_META + seq], sin_t[N_META:N_META + seq], seq, tm, True)

    o_attn = _attention(q.reshape(b, seq, -1), k.reshape(b, seq, -1), vt,
                        meta["k"], meta["vt"], _row_tile(seq // 2, 1024), _row_tile(seq // 8, 2048))

    t_blk = _row_tile(seq, 512)
    g3 = lambda a: a.reshape(b, seq, -1)
    o_f, _ = _gla_scan(g3(gq), g3(gk), g3(gv), g3(lg), meta["state"], False, t_blk)
    o_b, _ = _gla_scan(g3(gq), g3(gk), g3(gv), g3(lg), jnp.zeros_like(meta["state"]), True, t_blk)

    y = _mix_ffn(x2d, o_attn.reshape(n, -1), o_f.reshape(n, -1), o_b.reshape(n, -1), og, gates, w, tm)
    return y.reshape(b, seq, D_MODEL)


def kernel(x_prompt, x_sample, meta_tokens, attn_norm, w_in, q_a_norm, w_uq, kv_a_norm, w_ukv, q_norm, k_norm, w_o_mla, w_a2_fwd, b_a2_fwd, w_a2_bwd, b_a2_bwd, gla_o_norm, w_o_gla, w_out, ffn_norm, w_ffn_gate, w_ffn_up, w_ffn_down):
    assert attn_norm.shape[0] == 1, "single-layer encoder"
    w = _prep_weights(attn_norm[0], w_in[0], q_a_norm[0], w_uq[0], kv_a_norm[0], w_ukv[0], q_norm[0],
                      k_norm[0], w_o_mla[0], w_a2_fwd[0], b_a2_fwd[0], w_a2_bwd[0], b_a2_bwd[0],
                      gla_o_norm[0], w_o_gla[0], w_out[0], ffn_norm[0], w_ffn_gate[0], w_ffn_up[0],
                      w_ffn_down[0])
    max_len = N_META + max(x_prompt.shape[1], x_sample.shape[1])
    cos_t, sin_t = _rope_tables(max_len)

    xm = meta_tokens.astype(F32)
    _, k_m, v_m, mq, mk, mv, _, mlg, _ = _project(xm, w, cos_t[:N_META], sin_t[:N_META], N_META, N_META, False)
    pad = lambda a: jnp.pad(a, ((GLA_CHUNK - N_META, 0), (0, 0)))[None]
    _, s_meta = _gla_scan(pad(mq), pad(mk), pad(mv), pad(mlg),
                          jnp.zeros((GLA_H, GLA_DV, GLA_DK), F32), False, GLA_CHUNK)
    vt_m = jnp.concatenate([v_m.T.reshape(MLA_H, V_DIM, N_META), jnp.ones((MLA_H, ONES_ROWS, N_META), BF16)],
                           axis=1).reshape(MLA_H * VT_ROWS, N_META)
    meta = {"k": k_m, "vt": vt_m, "state": s_meta[0]}

    return (_encode_group(x_prompt, w, meta, cos_t, sin_t),
            _encode_group(x_sample, w, meta, cos_t, sin_t))
```

```python
import functools
import math

import jax
import jax.numpy as jnp
import numpy as np
from jax import lax
from jax.experimental import pallas as pl
from jax.experimental.pallas import tpu as pltpu

D_MODEL = 1024
N_META = 16
MLA_H = 8
NOPE_DIM = 128
ROPE_DIM = 64
QK_DIM = NOPE_DIM + ROPE_DIM
V_DIM = 128
Q_LORA = 768
KV_LORA = 256
ROPE_THETA = 10000.0
GLA_H = 4
GLA_DK = 128
GLA_DV = 256
GATE_RANK = 16
GATE_TEMP = 16.0
GLA_CHUNK = 64
D_FF = 2816
NORM_EPS = 1e-6

QK_PAD = 256
ONES_ROWS = 16
VT_ROWS = V_DIM + ONES_ROWS
LANES = 128
MXU_DIM = 256
LOG2E = 1.4426950408889634
VMEM_LIMIT = 56 * 1024 * 1024

F32 = jnp.float32
BF16 = jnp.bfloat16


def _dot(a, b):
    return jnp.dot(a, b, preferred_element_type=F32)


def _dot_nt(a, b):
    return lax.dot_general(a, b, (((1,), (1,)), ((), ())), preferred_element_type=F32)


def _dot_tn(a, b):
    return lax.dot_general(a, b, (((0,), (0,)), ((), ())), preferred_element_type=F32)


def _rms(x):
    return x * lax.rsqrt(jnp.mean(x * x, axis=-1, keepdims=True) + NORM_EPS)


def _resident(shape):
    return pl.BlockSpec(shape, lambda *_: (0,) * len(shape), pipeline_mode=pl.Buffered(1))


def _params(*sem):
    return pltpu.CompilerParams(dimension_semantics=sem, vmem_limit_bytes=VMEM_LIMIT)


def _row_tile(n, want):
    t = min(n, want)
    assert n % t == 0, (n, t)
    return t


G_Q0, G_K0, G_V0, G_G0, G_A0, G_S0, G_END = 0, 512, 1024, 2048, 3072, 3200, 5248


def _proj_kernel(x_ref, an_ref, wcq_ref, wckv_ref, qan_ref, wuq_ref, kvan_ref, wukv_ref, gq_ref, gk_ref,
                 cos_ref, sin_ref, wg_ref, wa2_ref, ba2_ref,
                 q_out, k_out, v_out, gla_q_out, gla_k_out, gla_v_out, og_out, lg_out, gate_out, *, transpose_v):
    h = (_rms(x_ref[...]) * an_ref[...]).astype(BF16)
    cq = _dot(h, wcq_ref[...])
    ckvr = _dot(h, wckv_ref[...])
    gla_q_out[...] = _dot(h, wg_ref[:, G_Q0:G_K0]) * (GLA_DK ** -0.5)
    gla_k_out[...] = _dot(h, wg_ref[:, G_K0:G_V0])
    cqn = (_rms(cq) * qan_ref[...]).astype(BF16)
    ckvn = (_rms(ckvr[:, :KV_LORA]) * kvan_ref[...]).astype(BF16)
    qall = _dot(cqn, wuq_ref[...])
    kv = _dot(ckvn, wukv_ref[...])
    gla_v_out[...] = _dot(h, wg_ref[:, G_V0:G_G0]).astype(BF16)
    g = _dot(h, wg_ref[:, G_G0:G_A0])
    og_out[...] = (g * jax.nn.sigmoid(g)).astype(og_out.dtype)
    a = _dot(h, wg_ref[:, G_A0:G_S0]).astype(BF16)
    pre = _dot(a, wa2_ref[...]) + ba2_ref[...]
    lg_out[...] = jax.nn.log_sigmoid(pre) * (1.0 / GATE_TEMP)
    gate_out[...] = jax.nn.sigmoid(_dot(h, wg_ref[:, G_S0:G_END])).astype(gate_out.dtype)

    v = kv[:, MLA_H * NOPE_DIM:]
    if transpose_v:
        vt = v.T.astype(BF16)
        ones = jnp.ones((ONES_ROWS, v.shape[0]), BF16)
        for hd in range(MLA_H):
            v_out[hd * VT_ROWS:hd * VT_ROWS + V_DIM, :] = vt[hd * V_DIM:(hd + 1) * V_DIM, :]
            v_out[hd * VT_ROWS + V_DIM:(hd + 1) * VT_ROWS, :] = ones
    else:
        v_out[...] = v.astype(BF16)

    cos = cos_ref[...]
    sin = sin_ref[...]
    lane = lax.broadcasted_iota(jnp.int32, (1, LANES), 1)
    ka = ckvr[:, KV_LORA:KV_LORA + LANES]
    kb = ckvr[:, KV_LORA + LANES:KV_LORA + 2 * LANES]
    head_lanes = [((lane >= p * ROPE_DIM) & (lane < (p + 1) * ROPE_DIM)).astype(F32) for p in range(2)]
    ssq_kr = jnp.sum(ka * ka * head_lanes[0], axis=-1, keepdims=True)
    k_rot = [(ka * gk_ref[1 + p:2 + p, :]) * cos + (kb * gk_ref[3 + p:4 + p, :]) * sin for p in range(2)]
    q_scale = QK_DIM ** -0.5 * LOG2E
    rope0 = MLA_H * NOPE_DIM
    for hd in range(MLA_H):
        j, p = divmod(hd, 2)
        qn = qall[:, hd * NOPE_DIM:(hd + 1) * NOPE_DIM]
        qa = qall[:, rope0 + j * LANES:rope0 + (j + 1) * LANES]
        qb = qall[:, rope0 + 4 * LANES + j * LANES:rope0 + 4 * LANES + (j + 1) * LANES]
        ssq = (jnp.sum(qn * qn, axis=-1, keepdims=True)
               + jnp.sum(qa * qa * head_lanes[p], axis=-1, keepdims=True))
        rs = lax.rsqrt(ssq * (1.0 / QK_DIM) + NORM_EPS) * q_scale
        q_rot = (qa * gq_ref[1 + p:2 + p, :]) * cos + (qb * gq_ref[3 + p:4 + p, :]) * sin
        q_out[:, hd * QK_PAD:hd * QK_PAD + LANES] = (qn * gq_ref[0:1, :] * rs).astype(BF16)
        q_out[:, hd * QK_PAD + LANES:(hd + 1) * QK_PAD] = (q_rot * rs).astype(BF16)
        kn = kv[:, hd * NOPE_DIM:(hd + 1) * NOPE_DIM]
        rsk = lax.rsqrt((jnp.sum(kn * kn, axis=-1, keepdims=True) + ssq_kr) * (1.0 / QK_DIM) + NORM_EPS)
        k_out[:, hd * QK_PAD:hd * QK_PAD + LANES] = (kn * gk_ref[0:1, :] * rsk).astype(BF16)
        k_out[:, hd * QK_PAD + LANES:(hd + 1) * QK_PAD] = (k_rot[p] * rsk).astype(BF16)


def _project(x2d, w, cos_t, sin_t, seq_len, tm, transpose_v):
    n = x2d.shape[0]
    nt = seq_len // tm
    row = lambda i: (i, 0)
    pos = lambda i: (i % nt, 0)
    v_dims = MLA_H * V_DIM
    vt_rows = MLA_H * VT_ROWS
    nk, nv = GLA_H * GLA_DK, GLA_H * GLA_DV
    gla_outs = ((nk, F32), (nk, F32), (nv, BF16), (nv, BF16), (2 * nk, F32), (2 * D_MODEL, BF16))
    return pl.pallas_call(
        functools.partial(_proj_kernel, transpose_v=transpose_v),
        grid=(n // tm,),
        in_specs=[
            pl.BlockSpec((tm, D_MODEL), row),
            _resident((1, D_MODEL)),
            _resident(w["w_cq"].shape),
            _resident(w["w_ckvr"].shape),
            _resident((1, Q_LORA)),
            _resident(w["w_uq"].shape),
            _resident((1, KV_LORA)),
            _resident(w["w_ukv"].shape),
            _resident((8, LANES)),
            _resident((8, LANES)),
            pl.BlockSpec((tm, LANES), pos),
            pl.BlockSpec((tm, LANES), pos),
            _resident(w["w_g"].shape),
            _resident(w["w_a2"].shape),
            _resident((1, 2 * nk)),
        ],
        out_specs=[
            pl.BlockSpec((tm, MLA_H * QK_PAD), row),
            pl.BlockSpec((tm, MLA_H * QK_PAD), row),
            pl.BlockSpec((vt_rows, tm), lambda i: (0, i)) if transpose_v else pl.BlockSpec((tm, v_dims), row),
        ] + [pl.BlockSpec((tm, width), row) for width, _ in gla_outs],
        out_shape=[
            jax.ShapeDtypeStruct((n, MLA_H * QK_PAD), BF16),
            jax.ShapeDtypeStruct((n, MLA_H * QK_PAD), BF16),
            jax.ShapeDtypeStruct((vt_rows, n) if transpose_v else (n, v_dims), BF16),
        ] + [jax.ShapeDtypeStruct((n, width), dtype) for width, dtype in gla_outs],
        compiler_params=_params("parallel"),
        name="project",
    )(x2d, w["attn_norm"], w["w_cq"], w["w_ckvr"], w["q_a_norm"], w["w_uq"], w["kv_a_norm"],
      w["w_ukv"], w["g_q"], w["g_k"], cos_t, sin_t, w["w_g"], w["w_a2"], w["b_a2"])


def _attn_kernel(q_ref, qn_ref, k_ref, vt_ref, km_ref, vmt_ref, o_ref,
                 s_even, s_odd, acc_scr, qt_scr, qtn_scr, max_scr, *, tk, ck, n_tiles):
    qi = pl.program_id(2)
    n_q = pl.num_programs(2)
    s_scr = (s_even, s_odd)
    n_chunks = tk // ck

    def transposed(ref):
        return ref[0].astype(F32).T.astype(BF16)

    def score_chunk(i, c, slot, running_max, qt_ref):
        off = pl.multiple_of(i * tk + c * ck, ck)
        s = _dot(k_ref[0, pl.ds(off, ck), :], qt_ref[...])
        s_scr[slot][c * ck:(c + 1) * ck, :] = s
        s_max = jnp.max(s, axis=0, keepdims=True)
        return s_max if running_max is None else jnp.maximum(running_max, s_max)

    def gather_chunk(i, c, slot, m, partial):
        off = pl.multiple_of(i * tk + c * ck, ck)
        p = jnp.exp2(s_scr[slot][c * ck:(c + 1) * ck, :] - m).astype(BF16)
        d = _dot(vt_ref[:, pl.ds(off, ck)], p)
        return d if partial is None else partial + d

    def step(i, slot, s_max, m, following="tile"):
        m_new = jnp.maximum(m, s_max)
        next_max, partial = None, None
        for c in range(n_chunks):
            if following == "tile":
                next_max = score_chunk(i + 1, c, 1 - slot, next_max, qt_scr)
            elif following == "block":
                next_max = score_chunk(0, c, 1 - slot, next_max, qtn_scr)
            partial = gather_chunk(i, c, slot, m_new, partial)
        acc_scr[...] = jnp.exp2(m - m_new) * acc_scr[...] + partial
        return next_max, m_new

    def two_steps(j, carry):
        carry = step(2 * j, 0, *carry)
        return step(2 * j + 1, 1, *carry)

    @pl.when(qi == 0)
    def _():
        qt_scr[...] = transposed(q_ref)
        max0 = None
        for c in range(n_chunks):
            max0 = score_chunk(0, c, 0, max0, qt_scr)
        max_scr[...] = max0

    s_meta = _dot(km_ref[...], qt_scr[...])
    m_meta = jnp.max(s_meta, axis=0, keepdims=True)
    acc_scr[...] = _dot(vmt_ref[...], jnp.exp2(s_meta - m_meta).astype(BF16))
    carry = lax.fori_loop(0, n_tiles // 2 - 1, two_steps, (max_scr[...], m_meta))
    max_last, m = step(n_tiles - 2, 0, *carry)

    @pl.when(qi < n_q - 1)
    def _():
        qtn_scr[...] = transposed(qn_ref)
        max_scr[...] = step(n_tiles - 1, 1, max_last, m, following="block")[0]
        qt_scr[...] = qtn_scr[...]

    @pl.when(qi == n_q - 1)
    def _():
        step(n_tiles - 1, 1, max_last, m, following=None)

    acc = acc_scr[...]
    o_ref[0] = (acc[:V_DIM] / acc[V_DIM:V_DIM + 1]).T.astype(o_ref.dtype)


def _attention(q, k, vt, k_meta, vt_meta, tq, tk):
    b, seq, _ = q.shape
    assert seq % (2 * tk) == 0, (seq, tk)
    n_q = seq // tq
    return pl.pallas_call(
        functools.partial(_attn_kernel, tk=tk, ck=min(tk, MXU_DIM), n_tiles=seq // tk),
        grid=(b, MLA_H, n_q),
        in_specs=[
            pl.BlockSpec((1, tq, QK_PAD), lambda bi, hi, qi: (bi, qi, hi)),
            pl.BlockSpec((1, tq, QK_PAD), lambda bi, hi, qi: (bi, jnp.minimum(qi + 1, n_q - 1), hi)),
            pl.BlockSpec((1, seq, QK_PAD), lambda bi, hi, qi: (bi, 0, hi)),
            pl.BlockSpec((VT_ROWS, seq), lambda bi, hi, qi: (hi, bi)),
            pl.BlockSpec((N_META, QK_PAD), lambda bi, hi, qi: (0, hi)),
            pl.BlockSpec((VT_ROWS, N_META), lambda bi, hi, qi: (hi, 0)),
        ],
        out_specs=pl.BlockSpec((1, tq, V_DIM), lambda bi, hi, qi: (bi, qi, hi)),
        out_shape=jax.ShapeDtypeStruct((b, seq, MLA_H * V_DIM), BF16),
        scratch_shapes=[
            pltpu.VMEM((tk, tq), F32),
            pltpu.VMEM((tk, tq), F32),
            pltpu.VMEM((VT_ROWS, tq), F32),
            pltpu.VMEM((QK_PAD, tq), BF16),
            pltpu.VMEM((QK_PAD, tq), BF16),
            pltpu.VMEM((1, tq), F32),
        ],
        compiler_params=_params("arbitrary", "arbitrary", "arbitrary"),
        name="attention",
    )(q, q, k, vt, k_meta, vt_meta)


def _gla_scan_kernel(q_ref, k_ref, v_ref, lg_ref, s0_ref, o_ref, sfin_ref, state, u_scr, dec_scr, qe_scr,
                     *, reverse, n_chunks):
    blk = pl.program_id(1)

    @pl.when(blk == 0)
    def _():
        state[...] = s0_ref[...]

    ri = lax.broadcasted_iota(jnp.int32, (GLA_CHUNK, GLA_CHUNK), 0)
    ci = lax.broadcasted_iota(jnp.int32, (GLA_CHUNK, GLA_CHUNK), 1)
    keep = (ri <= ci) if reverse else (ri >= ci)
    csum = jnp.where(keep, 1.0, 0.0).astype(BF16)
    csum2 = jnp.concatenate([csum, csum], axis=1)
    mid = GLA_CHUNK // 2 if reverse else GLA_CHUNK // 2 - 1
    end = 0 if reverse else GLA_CHUNK - 1
    group = math.gcd(n_chunks, 4)

    heads = [(slice(hd * GLA_DK, (hd + 1) * GLA_DK), slice(hd * GLA_DV, (hd + 1) * GLA_DV)) for hd in range(GLA_H)]

    def local(g, carry):
        cs = [g * group + j for j in range(group)]
        rows = [pl.ds(pl.multiple_of(c * GLA_CHUNK, GLA_CHUNK), GLA_CHUNK) for c in cs]
        bcums = []
        for r in rows:
            lg = lg_ref[0, r, :]
            lg_hi = lg.astype(BF16)
            lg_lo = (lg - lg_hi.astype(F32)).astype(BF16)
            bcums.append(_dot(csum2, jnp.concatenate([lg_hi, lg_lo], axis=0)))
        qss, kss, kds = [], [], []
        for c, r, bcum in zip(cs, rows, bcums):
            b_mid = bcum[mid:mid + 1, :]
            b_end = bcum[end:end + 1, :]
            q = q_ref[0, r, :]
            k = k_ref[0, r, :]
            qss.append((q * jnp.exp(bcum - b_mid)).astype(BF16))
            kss.append((k * jnp.exp(b_mid - bcum)).astype(BF16))
            kds.append((k * jnp.exp(b_end - bcum)).astype(BF16))
            qe_scr[c] = (q * jnp.exp(bcum)).astype(BF16)
            dec_scr[c] = jnp.broadcast_to(jnp.exp(b_end), (8, GLA_H * GLA_DK))
        attn = [[_dot_nt(qs[:, kc], ks[:, kc]) for kc, _ in heads] for qs, ks in zip(qss, kss)]
        for c, r, kd in zip(cs, rows, kds):
            for hd, (kc, vc) in enumerate(heads):
                u_scr[c, hd] = _dot_tn(v_ref[0, r, vc], kd[:, kc])
        for r, a_c in zip(rows, attn):
            for (_, vc), a in zip(heads, a_c):
                o_ref[0, r, vc] = _dot(jnp.where(keep, a, 0.0).astype(BF16), v_ref[0, r, vc])
        return carry

    lax.fori_loop(0, n_chunks // group, local, 0)

    def recur(i, carry):
        c = (n_chunks - 1 - i) if reverse else i
        rows = pl.ds(pl.multiple_of(c * GLA_CHUNK, GLA_CHUNK), GLA_CHUNK)
        for hd, (kc, vc) in enumerate(heads):
            st = state[hd]
            o_ref[0, rows, vc] += _dot_nt(qe_scr[c, :, kc], st.astype(BF16))
            state[hd] = st * dec_scr[c, 0:1, kc] + u_scr[c, hd]
        return carry

    lax.fori_loop(0, n_chunks, recur, 0, unroll=group)

    @pl.when(blk == pl.num_programs(1) - 1)
    def _():
        sfin_ref[0] = state[...]


def _gla_scan(q, k, v, lg, s0, reverse, t_blk):
    b, t, _ = q.shape
    nb = t // t_blk
    lg_col = 1 if reverse else 0
    blk = (lambda bi, i: (bi, nb - 1 - i, 0)) if reverse else (lambda bi, i: (bi, i, 0))
    lg_blk = (lambda bi, i: (bi, nb - 1 - i, lg_col)) if reverse else (lambda bi, i: (bi, i, lg_col))
    st_shape = (GLA_H, GLA_DV, GLA_DK)
    n_chunks = t_blk // GLA_CHUNK
    return pl.pallas_call(
        functools.partial(_gla_scan_kernel, reverse=reverse, n_chunks=n_chunks),
        grid=(b, nb),
        in_specs=[
            pl.BlockSpec((1, t_blk, GLA_H * GLA_DK), blk),
            pl.BlockSpec((1, t_blk, GLA_H * GLA_DK), blk),
            pl.BlockSpec((1, t_blk, GLA_H * GLA_DV), blk),
            pl.BlockSpec((1, t_blk, GLA_H * GLA_DK), lg_blk),
            pl.BlockSpec(st_shape, lambda bi, i: (0, 0, 0)),
        ],
        out_specs=[
            pl.BlockSpec((1, t_blk, GLA_H * GLA_DV), blk),
            pl.BlockSpec((1,) + st_shape, lambda bi, i: (bi, 0, 0, 0)),
        ],
        out_shape=[
            jax.ShapeDtypeStruct((b, t, GLA_H * GLA_DV), F32),
            jax.ShapeDtypeStruct((b,) + st_shape, F32),
        ],
        scratch_shapes=[
            pltpu.VMEM(st_shape, F32),
            pltpu.VMEM((n_chunks,) + st_shape, F32),
            pltpu.VMEM((n_chunks, 8, GLA_H * GLA_DK), F32),
            pltpu.VMEM((n_chunks, GLA_CHUNK, GLA_H * GLA_DK), BF16),
        ],
        compiler_params=_params("parallel", "arbitrary"),
        name="gla_scan_bwd" if reverse else "gla_scan_fwd",
    )(q, k, v, lg, s0)


def _mix_ffn_kernel(x_ref, oa_ref, of_ref, ob_ref, og_ref, gate_ref, womla_ref, onorm_ref, wogla_ref,
                    wout_ref, fn_ref, wgate_ref, wup_ref, wdown_ref, y_ref):
    y_a = _dot(oa_ref[...], womla_ref[...])
    o = of_ref[...] + ob_ref[...]
    og = og_ref[...]
    parts = []
    for hd in range(GLA_H):
        cols = slice(hd * GLA_DV, (hd + 1) * GLA_DV)
        parts.append((_rms(o[:, cols]) * onorm_ref[...] * og[:, cols]).astype(BF16))
    y_b = _dot(jnp.concatenate(parts, axis=-1), wogla_ref[...])
    gates = gate_ref[...]
    mixed = (gates[:, :D_MODEL] * y_a + gates[:, D_MODEL:] * y_b).astype(BF16)
    x1 = x_ref[...] + _dot(mixed, wout_ref[...])
    h = (_rms(x1) * fn_ref[...]).astype(BF16)
    g = _dot(h, wgate_ref[...])
    u = _dot(h, wup_ref[...])
    act = (g * jax.nn.sigmoid(g) * u).astype(BF16)
    y_ref[...] = x1 + _dot(act, wdown_ref[...])


def _mix_ffn(x2d, o_attn, o_f, o_b, og, gates, w, tm):
    n = x2d.shape[0]
    row = lambda i: (i, 0)
    return pl.pallas_call(
        _mix_ffn_kernel,
        grid=(n // tm,),
        in_specs=[
            pl.BlockSpec((tm, D_MODEL), row),
            pl.BlockSpec((tm, MLA_H * V_DIM), row),
            pl.BlockSpec((tm, GLA_H * GLA_DV), row),
            pl.BlockSpec((tm, GLA_H * GLA_DV), row),
            pl.BlockSpec((tm, GLA_H * GLA_DV), row),
            pl.BlockSpec((tm, 2 * D_MODEL), row),
            _resident(w["w_o_mla"].shape),
            _resident((1, GLA_DV)),
            _resident(w["w_o_gla"].shape),
            _resident(w["w_out"].shape),
            _resident((1, D_MODEL)),
            _resident(w["w_ffn_gate"].shape),
            _resident(w["w_ffn_up"].shape),
            _resident(w["w_ffn_down"].shape),
        ],
        out_specs=pl.BlockSpec((tm, D_MODEL), row),
        out_shape=jax.ShapeDtypeStruct((n, D_MODEL), F32),
        compiler_params=_params("parallel"),
        name="mix_ffn",
    )(x2d, o_attn, o_f, o_b, og, gates, w["w_o_mla"], w["gla_o_norm"], w["w_o_gla"], w["w_out"],
      w["ffn_norm"], w["w_ffn_gate"], w["w_ffn_up"], w["w_ffn_down"])


def _prep_weights(attn_norm, w_in, q_a_norm, w_uq, kv_a_norm, w_ukv, q_norm, k_norm, w_o_mla,
                  w_a2_fwd, b_a2_fwd, w_a2_bwd, b_a2_bwd, gla_o_norm, w_o_gla, w_out, ffn_norm,
                  w_ffn_gate, w_ffn_up, w_ffn_down):
    half = ROPE_DIM // 2
    w_in, w_uq, w_ukv = w_in.astype(BF16), w_uq.astype(BF16), w_ukv.astype(BF16)
    o = 0
    cols = {}
    for name, size in (("cq", Q_LORA), ("ckv", KV_LORA), ("kr", ROPE_DIM), ("gq", GLA_H * GLA_DK),
                       ("gk", GLA_H * GLA_DK), ("gv", GLA_H * GLA_DV), ("gg", GLA_H * GLA_DV),
                       ("af", GATE_RANK), ("ab", GATE_RANK), ("ga", D_MODEL), ("gb", D_MODEL)):
        cols[name] = w_in[:, o:o + size]
        o += size
    kr1, kr2 = cols["kr"][:, :half], cols["kr"][:, half:]
    w_ckvr = jnp.concatenate([cols["ckv"], kr1, kr2, kr1, kr2, kr2, kr1, kr2, kr1], axis=1)

    uq = w_uq.reshape(Q_LORA, MLA_H, QK_DIM)
    uq_nope = uq[:, :, :NOPE_DIM].reshape(Q_LORA, MLA_H * NOPE_DIM)
    uq_r = uq[:, :, NOPE_DIM:]
    uq_a = uq_r.reshape(Q_LORA, MLA_H * ROPE_DIM)
    uq_b = jnp.concatenate([uq_r[:, :, half:], uq_r[:, :, :half]], axis=-1).reshape(Q_LORA, MLA_H * ROPE_DIM)
    ukv = w_ukv.reshape(KV_LORA, MLA_H, NOPE_DIM + V_DIM)
    w_ukv2 = jnp.concatenate([ukv[:, :, :NOPE_DIM].reshape(KV_LORA, MLA_H * NOPE_DIM),
                              ukv[:, :, NOPE_DIM:].reshape(KV_LORA, MLA_H * V_DIM)], axis=1)

    def gain_rows(g):
        z = jnp.zeros((ROPE_DIM,), F32)
        ga = g[NOPE_DIM:]
        gb = jnp.concatenate([g[NOPE_DIM + half:], g[NOPE_DIM:NOPE_DIM + half]])
        rows = [g[:NOPE_DIM], jnp.concatenate([ga, z]), jnp.concatenate([z, ga]),
                jnp.concatenate([gb, z]), jnp.concatenate([z, gb])]
        rows += [jnp.zeros((LANES,), F32)] * 3
        return jnp.stack(rows)

    zpad = jnp.zeros((D_MODEL, LANES - 2 * GATE_RANK), BF16)
    w_g = jnp.concatenate([cols["gq"], cols["gk"], cols["gv"], cols["gg"], cols["af"], cols["ab"], zpad,
                           cols["ga"], cols["gb"]], axis=1)
    nk = GLA_H * GLA_DK
    w_a2 = jnp.zeros((LANES, 2 * nk), F32)
    w_a2 = w_a2.at[:GATE_RANK, :nk].set(w_a2_fwd).at[GATE_RANK:2 * GATE_RANK, nk:].set(w_a2_bwd)
    return {
        "attn_norm": attn_norm[None], "w_cq": cols["cq"], "w_ckvr": w_ckvr,
        "q_a_norm": q_a_norm[None], "w_uq": jnp.concatenate([uq_nope, uq_a, uq_b], axis=1),
        "kv_a_norm": kv_a_norm[None], "w_ukv": w_ukv2,
        "g_q": gain_rows(q_norm), "g_k": gain_rows(k_norm),
        "w_g": w_g, "w_a2": w_a2.astype(BF16),
        "b_a2": jnp.concatenate([b_a2_fwd, b_a2_bwd])[None],
        "w_o_mla": w_o_mla.astype(BF16), "gla_o_norm": gla_o_norm[None], "w_o_gla": w_o_gla.astype(BF16),
        "w_out": w_out.astype(BF16), "ffn_norm": ffn_norm[None], "w_ffn_gate": w_ffn_gate.astype(BF16),
        "w_ffn_up": w_ffn_up.astype(BF16), "w_ffn_down": w_ffn_down.astype(BF16),
    }


def _rope_tables(length):
    inv = 1.0 / (ROPE_THETA ** (np.arange(0, ROPE_DIM, 2, dtype=np.float64) / ROPE_DIM))
    ang = np.arange(length, dtype=np.float64)[:, None] * inv[None, :]
    cos, sin = np.cos(ang), np.sin(ang)
    return (jnp.asarray(np.tile(cos, (1, 4)), F32),
            jnp.asarray(np.concatenate([-sin, sin, -sin, sin], axis=1), F32))


def _encode_group(x, w, meta, cos_t, sin_t):
    b, seq, _ = x.shape
    n = b * seq
    x2d = x.reshape(n, D_MODEL)
    tm = _row_tile(seq, 256)
    q, k, vt, gq, gk, gv, og, lg, gates = _project(
        x2d, w, cos_t[N_META:N_META + seq], sin_t[N_META:N_META + seq], seq, tm, True)

    o_attn = _attention(q.reshape(b, seq, -1), k.reshape(b, seq, -1), vt,
                        meta["k"], meta["vt"], _row_tile(seq // 2, 1024), _row_tile(seq // 8, 2048))

    t_blk = _row_tile(seq, 512)
    g3 = lambda a: a.reshape(b, seq, -1)
    o_f, _ = _gla_scan(g3(gq), g3(gk), g3(gv), g3(lg), meta["state"], False, t_blk)
    o_b, _ = _gla_scan(g3(gq), g3(gk), g3(gv), g3(lg), jnp.zeros_like(meta["state"]), True, t_blk)

    y = _mix_ffn(x2d, o_attn.reshape(n, -1), o_f.reshape(n, -1), o_b.reshape(n, -1), og, gates, w, tm)
    return y.reshape(b, seq, D_MODEL)


def kernel(x_prompt, x_sample, meta_tokens, attn_norm, w_in, q_a_norm, w_uq, kv_a_norm, w_ukv, q_norm, k_norm, w_o_mla, w_a2_fwd, b_a2_fwd, w_a2_bwd, b_a2_bwd, gla_o_norm, w_o_gla, w_out, ffn_norm, w_ffn_gate, w_ffn_up, w_ffn_down):
    assert attn_norm.shape[0] == 1, "single-layer encoder"
    w = _prep_weights(attn_norm[0], w_in[0], q_a_norm[0], w_uq[0], kv_a_norm[0], w_ukv[0], q_norm[0],
                      k_norm[0], w_o_mla[0], w_a2_fwd[0], b_a2_fwd[0], w_a2_bwd[0], b_a2_bwd[0],
                      gla_o_norm[0], w_o_gla[0], w_out[0], ffn_norm[0], w_ffn_gate[0], w_ffn_up[0],
                      w_ffn_down[0])
    max_len = N_META + max(x_prompt.shape[1], x_sample.shape[1])
    cos_t, sin_t = _rope_tables(max_len)

    xm = meta_tokens.astype(F32)
    _, k_m, v_m, mq, mk, mv, _, mlg, _ = _project(xm, w, cos_t[:N_META], sin_t[:N_META], N_META, N_META, False)
    pad = lambda a: jnp.pad(a, ((GLA_CHUNK - N_META, 0), (0, 0)))[None]
    _, s_meta = _gla_scan(pad(mq), pad(mk), pad(mv), pad(mlg),
                          jnp.zeros((GLA_H, GLA_DV, GLA_DK), F32), False, GLA_CHUNK)
    vt_m = jnp.concatenate([v_m.T.reshape(MLA_H, V_DIM, N_META), jnp.ones((MLA_H, ONES_ROWS, N_META), BF16)],
                           axis=1).reshape(MLA_H * VT_ROWS, N_META)
    meta = {"k": k_m, "vt": vt_m, "state": s_meta[0]}

    return (_encode_group(x_prompt, w, meta, cos_t, sin_t),
            _encode_group(x_sample, w, meta, cos_t, sin_t))
```

```python
import functools
import math

import jax
import jax.numpy as jnp
import numpy as np
from jax import lax
from jax.experimental import pallas as pl
from jax.experimental.pallas import tpu as pltpu

D_MODEL = 1024
N_META = 16
MLA_H = 8
NOPE_DIM = 128
ROPE_DIM = 64
QK_DIM = NOPE_DIM + ROPE_DIM
V_DIM = 128
Q_LORA = 768
KV_LORA = 256
ROPE_THETA = 10000.0
GLA_H = 4
GLA_DK = 128
GLA_DV = 256
GATE_RANK = 16
GATE_TEMP = 16.0
GLA_CHUNK = 64
D_FF = 2816
NORM_EPS = 1e-6

QK_PAD = 256
ONES_ROWS = 16
VT_ROWS = V_DIM + ONES_ROWS
LANES = 128
MXU_DIM = 256
LOG2E = 1.4426950408889634
VMEM_LIMIT = 56 * 1024 * 1024

F32 = jnp.float32
BF16 = jnp.bfloat16


def _dot(a, b):
    return jnp.dot(a, b, preferred_element_type=F32)


def _dot_nt(a, b):
    return lax.dot_general(a, b, (((1,), (1,)), ((), ())), preferred_element_type=F32)


def _dot_tn(a, b):
    return lax.dot_general(a, b, (((0,), (0,)), ((), ())), preferred_element_type=F32)


def _rms(x):
    return x * lax.rsqrt(jnp.mean(x * x, axis=-1, keepdims=True) + NORM_EPS)


def _resident(shape):
    return pl.BlockSpec(shape, lambda *_: (0,) * len(shape), pipeline_mode=pl.Buffered(1))


def _params(*sem):
    return pltpu.CompilerParams(dimension_semantics=sem, vmem_limit_bytes=VMEM_LIMIT)


def _row_tile(n, want):
    t = min(n, want)
    assert n % t == 0, (n, t)
    return t


G_Q0, G_K0, G_V0, G_G0, G_A0, G_S0, G_END = 0, 512, 1024, 2048, 3072, 3200, 5248


def _proj_kernel(x_ref, an_ref, wcq_ref, wckv_ref, qan_ref, wuq_ref, kvan_ref, wukv_ref, gq_ref, gk_ref,
                 cos_ref, sin_ref, wg_ref, wa2_ref, ba2_ref,
                 q_out, k_out, v_out, gla_q_out, gla_k_out, gla_v_out, og_out, lg_out, gate_out, *, transpose_v):
    h = (_rms(x_ref[...]) * an_ref[...]).astype(BF16)
    cq = _dot(h, wcq_ref[...])
    ckvr = _dot(h, wckv_ref[...])
    gla_q_out[...] = _dot(h, wg_ref[:, G_Q0:G_K0]) * (GLA_DK ** -0.5)
    gla_k_out[...] = _dot(h, wg_ref[:, G_K0:G_V0])
    cqn = (_rms(cq) * qan_ref[...]).astype(BF16)
    ckvn = (_rms(ckvr[:, :KV_LORA]) * kvan_ref[...]).astype(BF16)
    qall = _dot(cqn, wuq_ref[...])
    kv = _dot(ckvn, wukv_ref[...])
    gla_v_out[...] = _dot(h, wg_ref[:, G_V0:G_G0]).astype(BF16)
    g = _dot(h, wg_ref[:, G_G0:G_A0])
    og_out[...] = (g * jax.nn.sigmoid(g)).astype(og_out.dtype)
    a = _dot(h, wg_ref[:, G_A0:G_S0]).astype(BF16)
    pre = _dot(a, wa2_ref[...]) + ba2_ref[...]
    lg_out[...] = jax.nn.log_sigmoid(pre) * (1.0 / GATE_TEMP)
    gate_out[...] = jax.nn.sigmoid(_dot(h, wg_ref[:, G_S0:G_END])).astype(gate_out.dtype)

    v = kv[:, MLA_H * NOPE_DIM:]
    if transpose_v:
        vt = v.T.astype(BF16)
        ones = jnp.ones((ONES_ROWS, v.shape[0]), BF16)
        for hd in range(MLA_H):
            v_out[hd * VT_ROWS:hd * VT_ROWS + V_DIM, :] = vt[hd * V_DIM:(hd + 1) * V_DIM, :]
            v_out[hd * VT_ROWS + V_DIM:(hd + 1) * VT_ROWS, :] = ones
    else:
        v_out[...] = v.astype(BF16)

    cos = cos_ref[...]
    sin = sin_ref[...]
    lane = lax.broadcasted_iota(jnp.int32, (1, LANES), 1)
    ka = ckvr[:, KV_LORA:KV_LORA + LANES]
    kb = ckvr[:, KV_LORA + LANES:KV_LORA + 2 * LANES]
    head_lanes = [((lane >= p * ROPE_DIM) & (lane < (p + 1) * ROPE_DIM)).astype(F32) for p in range(2)]
    ssq_kr = jnp.sum(ka * ka * head_lanes[0], axis=-1, keepdims=True)
    k_rot = [(ka * gk_ref[1 + p:2 + p, :]) * cos + (kb * gk_ref[3 + p:4 + p, :]) * sin for p in range(2)]
    q_scale = QK_DIM ** -0.5 * LOG2E
    rope0 = MLA_H * NOPE_DIM
    for hd in range(MLA_H):
        j, p = divmod(hd, 2)
        qn = qall[:, hd * NOPE_DIM:(hd + 1) * NOPE_DIM]
        qa = qall[:, rope0 + j * LANES:rope0 + (j + 1) * LANES]
        qb = qall[:, rope0 + 4 * LANES + j * LANES:rope0 + 4 * LANES + (j + 1) * LANES]
        ssq = (jnp.sum(qn * qn, axis=-1, keepdims=True)
               + jnp.sum(qa * qa * head_lanes[p], axis=-1, keepdims=True))
        rs = lax.rsqrt(ssq * (1.0 / QK_DIM) + NORM_EPS) * q_scale
        q_rot = (qa * gq_ref[1 + p:2 + p, :]) * cos + (qb * gq_ref[3 + p:4 + p, :]) * sin
        q_out[:, hd * QK_PAD:hd * QK_PAD + LANES] = (qn * gq_ref[0:1, :] * rs).astype(BF16)
        q_out[:, hd * QK_PAD + LANES:(hd + 1) * QK_PAD] = (q_rot * rs).astype(BF16)
        kn = kv[:, hd * NOPE_DIM:(hd + 1) * NOPE_DIM]
        rsk = lax.rsqrt((jnp.sum(kn * kn, axis=-1, keepdims=True) + ssq_kr) * (1.0 / QK_DIM) + NORM_EPS)
        k_out[:, hd * QK_PAD:hd * QK_PAD + LANES] = (kn * gk_ref[0:1, :] * rsk).astype(BF16)
        k_out[:, hd * QK_PAD + LANES:(hd + 1) * QK_PAD] = (k_rot[p] * rsk).astype(BF16)


def _project(x2d, w, cos_t, sin_t, seq_len, tm, transpose_v):
    n = x2d.shape[0]
    nt = seq_len // tm
    row = lambda i: (i, 0)
    pos = lambda i: (i % nt, 0)
    v_dims = MLA_H * V_DIM
    vt_rows = MLA_H * VT_ROWS
    nk, nv = GLA_H * GLA_DK, GLA_H * GLA_DV
    gla_outs = ((nk, F32), (nk, F32), (nv, BF16), (nv, BF16), (2 * nk, F32), (2 * D_MODEL, BF16))
    return pl.pallas_call(
        functools.partial(_proj_kernel, transpose_v=transpose_v),
        grid=(n // tm,),
        in_specs=[
            pl.BlockSpec((tm, D_MODEL), row),
            _resident((1, D_MODEL)),
            _resident(w["w_cq"].shape),
            _resident(w["w_ckvr"].shape),
            _resident((1, Q_LORA)),
            _resident(w["w_uq"].shape),
            _resident((1, KV_LORA)),
            _resident(w["w_ukv"].shape),
            _resident((8, LANES)),
            _resident((8, LANES)),
            pl.BlockSpec((tm, LANES), pos),
            pl.BlockSpec((tm, LANES), pos),
            _resident(w["w_g"].shape),
            _resident(w["w_a2"].shape),
            _resident((1, 2 * nk)),
        ],
        out_specs=[
            pl.BlockSpec((tm, MLA_H * QK_PAD), row),
            pl.BlockSpec((tm, MLA_H * QK_PAD), row),
            pl.BlockSpec((vt_rows, tm), lambda i: (0, i)) if transpose_v else pl.BlockSpec((tm, v_dims), row),
        ] + [pl.BlockSpec((tm, width), row) for width, _ in gla_outs],
        out_shape=[
            jax.ShapeDtypeStruct((n, MLA_H * QK_PAD), BF16),
            jax.ShapeDtypeStruct((n, MLA_H * QK_PAD), BF16),
            jax.ShapeDtypeStruct((vt_rows, n) if transpose_v else (n, v_dims), BF16),
        ] + [jax.ShapeDtypeStruct((n, width), dtype) for width, dtype in gla_outs],
        compiler_params=_params("parallel"),
        name="project",
    )(x2d, w["attn_norm"], w["w_cq"], w["w_ckvr"], w["q_a_norm"], w["w_uq"], w["kv_a_norm"],
      w["w_ukv"], w["g_q"], w["g_k"], cos_t, sin_t, w["w_g"], w["w_a2"], w["b_a2"])


def _attn_kernel(q_ref, qn_ref, k_ref, vt_ref, km_ref, vmt_ref, o_ref,
                 s_even, s_odd, acc_scr, qt_scr, qtn_scr, max_scr, *, tk, ck, n_tiles):
    qi = pl.program_id(2)
    n_q = pl.num_programs(2)
    s_scr = (s_even, s_odd)
    n_chunks = tk // ck

    def transposed(ref):
        return ref[0].astype(F32).T.astype(BF16)

    def score_chunk(i, c, slot, running_max, qt_ref):
        off = pl.multiple_of(i * tk + c * ck, ck)
        s = _dot(k_ref[0, pl.ds(off, ck), :], qt_ref[...])
        s_scr[slot][c * ck:(c + 1) * ck, :] = s
        s_max = jnp.max(s, axis=0, keepdims=True)
        return s_max if running_max is None else jnp.maximum(running_max, s_max)

    def gather_chunk(i, c, slot, m, partial):
        off = pl.multiple_of(i * tk + c * ck, ck)
        p = jnp.exp2(s_scr[slot][c * ck:(c + 1) * ck, :] - m).astype(BF16)
        d = _dot(vt_ref[:, pl.ds(off, ck)], p)
        return d if partial is None else partial + d

    def step(i, slot, s_max, m, following="tile"):
        m_new = jnp.maximum(m, s_max)
        next_max, partial = None, None
        for c in range(n_chunks):
            if following == "tile":
                next_max = score_chunk(i + 1, c, 1 - slot, next_max, qt_scr)
            elif following == "block":
                next_max = score_chunk(0, c, 1 - slot, next_max, qtn_scr)
            partial = gather_chunk(i, c, slot, m_new, partial)
        acc_scr[...] = jnp.exp2(m - m_new) * acc_scr[...] + partial
        return next_max, m_new

    def two_steps(j, carry):
        carry = step(2 * j, 0, *carry)
        return step(2 * j + 1, 1, *carry)

    @pl.when(qi == 0)
    def _():
        qt_scr[...] = transposed(q_ref)
        max0 = None
        for c in range(n_chunks):
            max0 = score_chunk(0, c, 0, max0, qt_scr)
        max_scr[...] = max0

    s_meta = _dot(km_ref[...], qt_scr[...])
    m_meta = jnp.max(s_meta, axis=0, keepdims=True)
    acc_scr[...] = _dot(vmt_ref[...], jnp.exp2(s_meta - m_meta).astype(BF16))
    carry = lax.fori_loop(0, n_tiles // 2 - 1, two_steps, (max_scr[...], m_meta))
    max_last, m = step(n_tiles - 2, 0, *carry)

    @pl.when(qi < n_q - 1)
    def _():
        qtn_scr[...] = transposed(qn_ref)
        max_scr[...] = step(n_tiles - 1, 1, max_last, m, following="block")[0]
        qt_scr[...] = qtn_scr[...]

    @pl.when(qi == n_q - 1)
    def _():
        step(n_tiles - 1, 1, max_last, m, following=None)

    acc = acc_scr[...]
    o_ref[0] = (acc[:V_DIM] / acc[V_DIM:V_DIM + 1]).T.astype(o_ref.dtype)


def _attention(q, k, vt, k_meta, vt_meta, tq, tk):
    b, seq, _ = q.shape
    assert seq % (2 * tk) == 0, (seq, tk)
    n_q = seq // tq
    return pl.pallas_call(
        functools.partial(_attn_kernel, tk=tk, ck=min(tk, MXU_DIM), n_tiles=seq // tk),
        grid=(b, MLA_H, n_q),
        in_specs=[
            pl.BlockSpec((1, tq, QK_PAD), lambda bi, hi, qi: (bi, qi, hi)),
            pl.BlockSpec((1, tq, QK_PAD), lambda bi, hi, qi: (bi, jnp.minimum(qi + 1, n_q - 1), hi)),
            pl.BlockSpec((1, seq, QK_PAD), lambda bi, hi, qi: (bi, 0, hi)),
            pl.BlockSpec((VT_ROWS, seq), lambda bi, hi, qi: (hi, bi)),
            pl.BlockSpec((N_META, QK_PAD), lambda bi, hi, qi: (0, hi)),
            pl.BlockSpec((VT_ROWS, N_META), lambda bi, hi, qi: (hi, 0)),
        ],
        out_specs=pl.BlockSpec((1, tq, V_DIM), lambda bi, hi, qi: (bi, qi, hi)),
        out_shape=jax.ShapeDtypeStruct((b, seq, MLA_H * V_DIM), BF16),
        scratch_shapes=[
            pltpu.VMEM((tk, tq), F32),
            pltpu.VMEM((tk, tq), F32),
            pltpu.VMEM((VT_ROWS, tq), F32),
            pltpu.VMEM((QK_PAD, tq), BF16),
            pltpu.VMEM((QK_PAD, tq), BF16),
            pltpu.VMEM((1, tq), F32),
        ],
        compiler_params=_params("arbitrary", "arbitrary", "arbitrary"),
        name="attention",
    )(q, q, k, vt, k_meta, vt_meta)


def _gla_scan_kernel(q_ref, k_ref, v_ref, lg_ref, s0_ref, o_ref, sfin_ref, state, u_scr, dec_scr, qe_scr,
                     *, reverse, n_chunks):
    blk = pl.program_id(1)

    @pl.when(blk == 0)
    def _():
        state[...] = s0_ref[...]

    ri = lax.broadcasted_iota(jnp.int32, (GLA_CHUNK, GLA_CHUNK), 0)
    ci = lax.broadcasted_iota(jnp.int32, (GLA_CHUNK, GLA_CHUNK), 1)
    keep = (ri <= ci) if reverse else (ri >= ci)
    csum = jnp.where(keep, 1.0, 0.0).astype(BF16)
    csum2 = jnp.concatenate([csum, csum], axis=1)
    mid = GLA_CHUNK // 2 if reverse else GLA_CHUNK // 2 - 1
    end = 0 if reverse else GLA_CHUNK - 1
    group = math.gcd(n_chunks, 4)

    heads = [(slice(hd * GLA_DK, (hd + 1) * GLA_DK), slice(hd * GLA_DV, (hd + 1) * GLA_DV)) for hd in range(GLA_H)]

    def local(g, carry):
        cs = [g * group + j for j in range(group)]
        rows = [pl.ds(pl.multiple_of(c * GLA_CHUNK, GLA_CHUNK), GLA_CHUNK) for c in cs]
        bcums = []
        for r in rows:
            lg = lg_ref[0, r, :]
            lg_hi = lg.astype(BF16)
            lg_lo = (lg - lg_hi.astype(F32)).astype(BF16)
            bcums.append(_dot(csum2, jnp.concatenate([lg_hi, lg_lo], axis=0)))
        qss, kss, kds = [], [], []
        for c, r, bcum in zip(cs, rows, bcums):
            b_mid = bcum[mid:mid + 1, :]
            b_end = bcum[end:end + 1, :]
            q = q_ref[0, r, :]
            k = k_ref[0, r, :]
            qss.append((q * jnp.exp(bcum - b_mid)).astype(BF16))
            kss.append((k * jnp.exp(b_mid - bcum)).astype(BF16))
            kds.append((k * jnp.exp(b_end - bcum)).astype(BF16))
            qe_scr[c] = (q * jnp.exp(bcum)).astype(BF16)
            dec_scr[c] = jnp.broadcast_to(jnp.exp(b_end), (8, GLA_H * GLA_DK))
        attn = [[_dot_nt(qs[:, kc], ks[:, kc]) for kc, _ in heads] for qs, ks in zip(qss, kss)]
        for c, r, kd in zip(cs, rows, kds):
            for hd, (kc, vc) in enumerate(heads):
                u_scr[c, hd] = _dot_tn(v_ref[0, r, vc], kd[:, kc])
        for r, a_c in zip(rows, attn):
            for (_, vc), a in zip(heads, a_c):
                o_ref[0, r, vc] = _dot(jnp.where(keep, a, 0.0).astype(BF16), v_ref[0, r, vc])
        return carry

    lax.fori_loop(0, n_chunks // group, local, 0)

    def recur(i, carry):
        c = (n_chunks - 1 - i) if reverse else i
        rows = pl.ds(pl.multiple_of(c * GLA_CHUNK, GLA_CHUNK), GLA_CHUNK)
        for hd, (kc, vc) in enumerate(heads):
            st = state[hd]
            o_ref[0, rows, vc] += _dot_nt(qe_scr[c, :, kc], st.astype(BF16))
            state[hd] = st * dec_scr[c, 0:1, kc] + u_scr[c, hd]
        return carry

    lax.fori_loop(0, n_chunks, recur, 0, unroll=group)

    @pl.when(blk == pl.num_programs(1) - 1)
    def _():
        sfin_ref[0] = state[...]


def _gla_scan(q, k, v, lg, s0, reverse, t_blk):
    b, t, _ = q.shape
    nb = t // t_blk
    lg_col = 1 if reverse else 0
    blk = (lambda bi, i: (bi, nb - 1 - i, 0)) if reverse else (lambda bi, i: (bi, i, 0))
    lg_blk = (lambda bi, i: (bi, nb - 1 - i, lg_col)) if reverse else (lambda bi, i: (bi, i, lg_col))
    st_shape = (GLA_H, GLA_DV, GLA_DK)
    n_chunks = t_blk // GLA_CHUNK
    return pl.pallas_call(
        functools.partial(_gla_scan_kernel, reverse=reverse, n_chunks=n_chunks),
        grid=(b, nb),
        in_specs=[
            pl.BlockSpec((1, t_blk, GLA_H * GLA_DK), blk),
            pl.BlockSpec((1, t_blk, GLA_H * GLA_DK), blk),
            pl.BlockSpec((1, t_blk, GLA_H * GLA_DV), blk),
            pl.BlockSpec((1, t_blk, GLA_H * GLA_DK), lg_blk),
            pl.BlockSpec(st_shape, lambda bi, i: (0, 0, 0)),
        ],
        out_specs=[
            pl.BlockSpec((1, t_blk, GLA_H * GLA_DV), blk),
            pl.BlockSpec((1,) + st_shape, lambda bi, i: (bi, 0, 0, 0)),
        ],
        out_shape=[
            jax.ShapeDtypeStruct((b, t, GLA_H * GLA_DV), F32),
            jax.ShapeDtypeStruct((b,) + st_shape, F32),
        ],
        scratch_shapes=[
            pltpu.VMEM(st_shape, F32),
            pltpu.VMEM((n_chunks,) + st_shape, F32),
            pltpu.VMEM((n_chunks, 8, GLA_H * GLA_DK), F32),
            pltpu.VMEM((n_chunks, GLA_CHUNK, GLA_H * GLA_DK), BF16),
        ],
        compiler_params=_params("parallel", "arbitrary"),
        name="gla_scan_bwd" if reverse else "gla_scan_fwd",
    )(q, k, v, lg, s0)


def _mix_ffn_kernel(x_ref, oa_ref, of_ref, ob_ref, og_ref, gate_ref, womla_ref, onorm_ref, wogla_ref,
                    wout_ref, fn_ref, wgate_ref, wup_ref, wdown_ref, y_ref):
    y_a = _dot(oa_ref[...], womla_ref[...])
    o = of_ref[...] + ob_ref[...]
    og = og_ref[...]
    parts = []
    for hd in range(GLA_H):
        cols = slice(hd * GLA_DV, (hd + 1) * GLA_DV)
        parts.append((_rms(o[:, cols]) * onorm_ref[...] * og[:, cols]).astype(BF16))
    y_b = _dot(jnp.concatenate(parts, axis=-1), wogla_ref[...])
    gates = gate_ref[...]
    mixed = (gates[:, :D_MODEL] * y_a + gates[:, D_MODEL:] * y_b).astype(BF16)
    x1 = x_ref[...] + _dot(mixed, wout_ref[...])
    h = (_rms(x1) * fn_ref[...]).astype(BF16)
    g = _dot(h, wgate_ref[...])
    u = _dot(h, wup_ref[...])
    act = (g * jax.nn.sigmoid(g) * u).astype(BF16)
    y_ref[...] = x1 + _dot(act, wdown_ref[...])


def _mix_ffn(x2d, o_attn, o_f, o_b, og, gates, w, tm):
    n = x2d.shape[0]
    row = lambda i: (i, 0)
    return pl.pallas_call(
        _mix_ffn_kernel,
        grid=(n // tm,),
        in_specs=[
            pl.BlockSpec((tm, D_MODEL), row),
            pl.BlockSpec((tm, MLA_H * V_DIM), row),
            pl.BlockSpec((tm, GLA_H * GLA_DV), row),
            pl.BlockSpec((tm, GLA_H * GLA_DV), row),
            pl.BlockSpec((tm, GLA_H * GLA_DV), row),
            pl.BlockSpec((tm, 2 * D_MODEL), row),
            _resident(w["w_o_mla"].shape),
            _resident((1, GLA_DV)),
            _resident(w["w_o_gla"].shape),
            _resident(w["w_out"].shape),
            _resident((1, D_MODEL)),
            _resident(w["w_ffn_gate"].shape),
            _resident(w["w_ffn_up"].shape),
            _resident(w["w_ffn_down"].shape),
        ],
        out_specs=pl.BlockSpec((tm, D_MODEL), row),
        out_shape=jax.ShapeDtypeStruct((n, D_MODEL), F32),
        compiler_params=_params("parallel"),
        name="mix_ffn",
    )(x2d, o_attn, o_f, o_b, og, gates, w["w_o_mla"], w["gla_o_norm"], w["w_o_gla"], w["w_out"],
      w["ffn_norm"], w["w_ffn_gate"], w["w_ffn_up"], w["w_ffn_down"])


def _prep_weights(attn_norm, w_in, q_a_norm, w_uq, kv_a_norm, w_ukv, q_norm, k_norm, w_o_mla,
                  w_a2_fwd, b_a2_fwd, w_a2_bwd, b_a2_bwd, gla_o_norm, w_o_gla, w_out, ffn_norm,
                  w_ffn_gate, w_ffn_up, w_ffn_down):
    half = ROPE_DIM // 2
    w_in, w_uq, w_ukv = w_in.astype(BF16), w_uq.astype(BF16), w_ukv.astype(BF16)
    o = 0
    cols = {}
    for name, size in (("cq", Q_LORA), ("ckv", KV_LORA), ("kr", ROPE_DIM), ("gq", GLA_H * GLA_DK),
                       ("gk", GLA_H * GLA_DK), ("gv", GLA_H * GLA_DV), ("gg", GLA_H * GLA_DV),
                       ("af", GATE_RANK), ("ab", GATE_RANK), ("ga", D_MODEL), ("gb", D_MODEL)):
        cols[name] = w_in[:, o:o + size]
        o += size
    kr1, kr2 = cols["kr"][:, :half], cols["kr"][:, half:]
    w_ckvr = jnp.concatenate([cols["ckv"], kr1, kr2, kr1, kr2, kr2, kr1, kr2, kr1], axis=1)

    uq = w_uq.reshape(Q_LORA, MLA_H, QK_DIM)
    uq_nope = uq[:, :, :NOPE_DIM].reshape(Q_LORA, MLA_H * NOPE_DIM)
    uq_r = uq[:, :, NOPE_DIM:]
    uq_a = uq_r.reshape(Q_LORA, MLA_H * ROPE_DIM)
    uq_b = jnp.concatenate([uq_r[:, :, half:], uq_r[:, :, :half]], axis=-1).reshape(Q_LORA, MLA_H * ROPE_DIM)
    ukv = w_ukv.reshape(KV_LORA, MLA_H, NOPE_DIM + V_DIM)
    w_ukv2 = jnp.concatenate([ukv[:, :, :NOPE_DIM].reshape(KV_LORA, MLA_H * NOPE_DIM),
                              ukv[:, :, NOPE_DIM:].reshape(KV_LORA, MLA_H * V_DIM)], axis=1)

    def gain_rows(g):
        z = jnp.zeros((ROPE_DIM,), F32)
        ga = g[NOPE_DIM:]
        gb = jnp.concatenate([g[NOPE_DIM + half:], g[NOPE_DIM:NOPE_DIM + half]])
        rows = [g[:NOPE_DIM], jnp.concatenate([ga, z]), jnp.concatenate([z, ga]),
                jnp.concatenate([gb, z]), jnp.concatenate([z, gb])]
        rows += [jnp.zeros((LANES,), F32)] * 3
        return jnp.stack(rows)

    zpad = jnp.zeros((D_MODEL, LANES - 2 * GATE_RANK), BF16)
    w_g = jnp.concatenate([cols["gq"], cols["gk"], cols["gv"], cols["gg"], cols["af"], cols["ab"], zpad,
                           cols["ga"], cols["gb"]], axis=1)
    nk = GLA_H * GLA_DK
    w_a2 = jnp.zeros((LANES, 2 * nk), F32)
    w_a2 = w_a2.at[:GATE_RANK, :nk].set(w_a2_fwd).at[GATE_RANK:2 * GATE_RANK, nk:].set(w_a2_bwd)
    return {
        "attn_norm": attn_norm[None], "w_cq": cols["cq"], "w_ckvr": w_ckvr,
        "q_a_norm": q_a_norm[None], "w_uq": jnp.concatenate([uq_nope, uq_a, uq_b], axis=1),
        "kv_a_norm": kv_a_norm[None], "w_ukv": w_ukv2,
        "g_q": gain_rows(q_norm), "g_k": gain_rows(k_norm),
        "w_g": w_g, "w_a2": w_a2.astype(BF16),
        "b_a2": jnp.concatenate([b_a2_fwd, b_a2_bwd])[None],
        "w_o_mla": w_o_mla.astype(BF16), "gla_o_norm": gla_o_norm[None], "w_o_gla": w_o_gla.astype(BF16),
        "w_out": w_out.astype(BF16), "ffn_norm": ffn_norm[None], "w_ffn_gate": w_ffn_gate.astype(BF16),
        "w_ffn_up": w_ffn_up.astype(BF16), "w_ffn_down": w_ffn_down.astype(BF16),
    }


def _rope_tables(length):
    inv = 1.0 / (ROPE_THETA ** (np.arange(0, ROPE_DIM, 2, dtype=np.float64) / ROPE_DIM))
    ang = np.arange(length, dtype=np.float64)[:, None] * inv[None, :]
    cos, sin = np.cos(ang), np.sin(ang)
    return (jnp.asarray(np.tile(cos, (1, 4)), F32),
            jnp.asarray(np.concatenate([-sin, sin, -sin, sin], axis=1), F32))


def _encode_group(x, w, meta, cos_t, sin_t):
    b, seq, _ = x.shape
    n = b * seq
    x2d = x.reshape(n, D_MODEL)
    tm = _row_tile(seq, 256)
    q, k, vt, gq, gk, gv, og, lg, gates = _project(
        x2d, w, cos_t[N_META:N_META + seq], sin_t[N_META:N_META + seq], seq, tm, True)

    o_attn = _attention(q.reshape(b, seq, -1), k.reshape(b, seq, -1), vt,
                        meta["k"], meta["vt"], _row_tile(seq // 2, 1024), _row_tile(seq // 4, 2048))

    t_blk = _row_tile(seq, 512)
    g3 = lambda a: a.reshape(b, seq, -1)
    o_f, _ = _gla_scan(g3(gq), g3(gk), g3(gv), g3(lg), meta["state"], False, t_blk)
    o_b, _ = _gla_scan(g3(gq), g3(gk), g3(gv), g3(lg), jnp.zeros_like(meta["state"]), True, t_blk)

    y = _mix_ffn(x2d, o_attn.reshape(n, -1), o_f.reshape(n, -1), o_b.reshape(n, -1), og, gates, w, tm)
    return y.reshape(b, seq, D_MODEL)


def kernel(x_prompt, x_sample, meta_tokens, attn_norm, w_in, q_a_norm, w_uq, kv_a_norm, w_ukv, q_norm, k_norm, w_o_mla, w_a2_fwd, b_a2_fwd, w_a2_bwd, b_a2_bwd, gla_o_norm, w_o_gla, w_out, ffn_norm, w_ffn_gate, w_ffn_up, w_ffn_down):
    assert attn_norm.shape[0] == 1, "single-layer encoder"
    w = _prep_weights(attn_norm[0], w_in[0], q_a_norm[0], w_uq[0], kv_a_norm[0], w_ukv[0], q_norm[0],
                      k_norm[0], w_o_mla[0], w_a2_fwd[0], b_a2_fwd[0], w_a2_bwd[0], b_a2_bwd[0],
                      gla_o_norm[0], w_o_gla[0], w_out[0], ffn_norm[0], w_ffn_gate[0], w_ffn_up[0],
                      w_ffn_down[0])
    max_len = N_META + max(x_prompt.shape[1], x_sample.shape[1])
    cos_t, sin_t = _rope_tables(max_len)

    xm = meta_tokens.astype(F32)
    _, k_m, v_m, mq, mk, mv, _, mlg, _ = _project(xm, w, cos_t[:N_META], sin_t[:N_META], N_META, N_META, False)
    pad = lambda a: jnp.pad(a, ((GLA_CHUNK - N_META, 0), (0, 0)))[None]
    _, s_meta = _gla_scan(pad(mq), pad(mk), pad(mv), pad(mlg),
                          jnp.zeros((GLA_H, GLA_DV, GLA_DK), F32), False, GLA_CHUNK)
    vt_m = jnp.concatenate([v_m.T.reshape(MLA_H, V_DIM, N_META), jnp.ones((MLA_H, ONES_ROWS, N_META), BF16)],
                           axis=1).reshape(MLA_H * VT_ROWS, N_META)
    meta = {"k": k_m, "vt": vt_m, "state": s_meta[0]}

    return (_encode_group(x_prompt, w, meta, cos_t, sin_t),
            _encode_group(x_sample, w, meta, cos_t, sin_t))
```

```python
import functools
import math

import jax
import jax.numpy as jnp
import numpy as np
from jax import lax
from jax.experimental import pallas as pl
from jax.experimental.pallas import tpu as pltpu

D_MODEL = 1024
N_META = 16
MLA_H = 8
NOPE_DIM = 128
ROPE_DIM = 64
QK_DIM = NOPE_DIM + ROPE_DIM
V_DIM = 128
Q_LORA = 768
KV_LORA = 256
ROPE_THETA = 10000.0
GLA_H = 4
GLA_DK = 128
GLA_DV = 256
GATE_RANK = 16
GATE_TEMP = 16.0
GLA_CHUNK = 64
D_FF = 2816
NORM_EPS = 1e-6

QK_PAD = 256
ONES_ROWS = 16
VT_ROWS = V_DIM + ONES_ROWS
LANES = 128
MXU_DIM = 256
LOG2E = 1.4426950408889634
VMEM_LIMIT = 56 * 1024 * 1024

F32 = jnp.float32
BF16 = jnp.bfloat16


def _dot(a, b):
    return jnp.dot(a, b, preferred_element_type=F32)


def _dot_nt(a, b):
    return lax.dot_general(a, b, (((1,), (1,)), ((), ())), preferred_element_type=F32)


def _dot_tn(a, b):
    return lax.dot_general(a, b, (((0,), (0,)), ((), ())), preferred_element_type=F32)


def _rms(x):
    return x * lax.rsqrt(jnp.mean(x * x, axis=-1, keepdims=True) + NORM_EPS)


def _resident(shape):
    return pl.BlockSpec(shape, lambda *_: (0,) * len(shape), pipeline_mode=pl.Buffered(1))


def _params(*sem):
    return pltpu.CompilerParams(dimension_semantics=sem, vmem_limit_bytes=VMEM_LIMIT)


def _row_tile(n, want):
    t = min(n, want)
    assert n % t == 0, (n, t)
    return t


G_Q0, G_K0, G_V0, G_G0, G_A0, G_S0, G_END = 0, 512, 1024, 2048, 3072, 3200, 5248


def _proj_kernel(x_ref, an_ref, wcq_ref, wckv_ref, qan_ref, wuq_ref, kvan_ref, wukv_ref, gq_ref, gk_ref,
                 cos_ref, sin_ref, wg_ref, wa2_ref, ba2_ref,
                 q_out, k_out, v_out, gla_q_out, gla_k_out, gla_v_out, og_out, lg_out, gate_out, *, transpose_v):
    h = (_rms(x_ref[...]) * an_ref[...]).astype(BF16)
    cq = _dot(h, wcq_ref[...])
    ckvr = _dot(h, wckv_ref[...])
    gla_q_out[...] = _dot(h, wg_ref[:, G_Q0:G_K0]) * (GLA_DK ** -0.5)
    gla_k_out[...] = _dot(h, wg_ref[:, G_K0:G_V0])
    cqn = (_rms(cq) * qan_ref[...]).astype(BF16)
    ckvn = (_rms(ckvr[:, :KV_LORA]) * kvan_ref[...]).astype(BF16)
    qall = _dot(cqn, wuq_ref[...])
    kv = _dot(ckvn, wukv_ref[...])
    gla_v_out[...] = _dot(h, wg_ref[:, G_V0:G_G0]).astype(BF16)
    g = _dot(h, wg_ref[:, G_G0:G_A0])
    og_out[...] = (g * jax.nn.sigmoid(g)).astype(og_out.dtype)
    a = _dot(h, wg_ref[:, G_A0:G_S0]).astype(BF16)
    pre = _dot(a, wa2_ref[...]) + ba2_ref[...]
    lg_out[...] = jax.nn.log_sigmoid(pre) * (1.0 / GATE_TEMP)
    gate_out[...] = jax.nn.sigmoid(_dot(h, wg_ref[:, G_S0:G_END])).astype(gate_out.dtype)

    v = kv[:, MLA_H * NOPE_DIM:]
    if transpose_v:
        vt = v.T.astype(BF16)
        ones = jnp.ones((ONES_ROWS, v.shape[0]), BF16)
        for hd in range(MLA_H):
            v_out[hd * VT_ROWS:hd * VT_ROWS + V_DIM, :] = vt[hd * V_DIM:(hd + 1) * V_DIM, :]
            v_out[hd * VT_ROWS + V_DIM:(hd + 1) * VT_ROWS, :] = ones
    else:
        v_out[...] = v.astype(BF16)

    cos = cos_ref[...]
    sin = sin_ref[...]
    lane = lax.broadcasted_iota(jnp.int32, (1, LANES), 1)
    ka = ckvr[:, KV_LORA:KV_LORA + LANES]
    kb = ckvr[:, KV_LORA + LANES:KV_LORA + 2 * LANES]
    head_lanes = [((lane >= p * ROPE_DIM) & (lane < (p + 1) * ROPE_DIM)).astype(F32) for p in range(2)]
    ssq_kr = jnp.sum(ka * ka * head_lanes[0], axis=-1, keepdims=True)
    k_rot = [(ka * gk_ref[1 + p:2 + p, :]) * cos + (kb * gk_ref[3 + p:4 + p, :]) * sin for p in range(2)]
    q_scale = QK_DIM ** -0.5 * LOG2E
    rope0 = MLA_H * NOPE_DIM
    for hd in range(MLA_H):
        j, p = divmod(hd, 2)
        qn = qall[:, hd * NOPE_DIM:(hd + 1) * NOPE_DIM]
        qa = qall[:, rope0 + j * LANES:rope0 + (j + 1) * LANES]
        qb = qall[:, rope0 + 4 * LANES + j * LANES:rope0 + 4 * LANES + (j + 1) * LANES]
        ssq = (jnp.sum(qn * qn, axis=-1, keepdims=True)
               + jnp.sum(qa * qa * head_lanes[p], axis=-1, keepdims=True))
        rs = lax.rsqrt(ssq * (1.0 / QK_DIM) + NORM_EPS) * q_scale
        q_rot = (qa * gq_ref[1 + p:2 + p, :]) * cos + (qb * gq_ref[3 + p:4 + p, :]) * sin
        q_out[:, hd * QK_PAD:hd * QK_PAD + LANES] = (qn * gq_ref[0:1, :] * rs).astype(BF16)
        q_out[:, hd * QK_PAD + LANES:(hd + 1) * QK_PAD] = (q_rot * rs).astype(BF16)
        kn = kv[:, hd * NOPE_DIM:(hd + 1) * NOPE_DIM]
        rsk = lax.rsqrt((jnp.sum(kn * kn, axis=-1, keepdims=True) + ssq_kr) * (1.0 / QK_DIM) + NORM_EPS)
        k_out[:, hd * QK_PAD:hd * QK_PAD + LANES] = (kn * gk_ref[0:1, :] * rsk).astype(BF16)
        k_out[:, hd * QK_PAD + LANES:(hd + 1) * QK_PAD] = (k_rot[p] * rsk).astype(BF16)


def _project(x2d, w, cos_t, sin_t, seq_len, tm, transpose_v):
    n = x2d.shape[0]
    nt = seq_len // tm
    row = lambda i: (i, 0)
    pos = lambda i: (i % nt, 0)
    v_dims = MLA_H * V_DIM
    vt_rows = MLA_H * VT_ROWS
    nk, nv = GLA_H * GLA_DK, GLA_H * GLA_DV
    gla_outs = ((nk, F32), (nk, F32), (nv, BF16), (nv, BF16), (2 * nk, F32), (2 * D_MODEL, BF16))
    return pl.pallas_call(
        functools.partial(_proj_kernel, transpose_v=transpose_v),
        grid=(n // tm,),
        in_specs=[
            pl.BlockSpec((tm, D_MODEL), row),
            _resident((1, D_MODEL)),
            _resident(w["w_cq"].shape),
            _resident(w["w_ckvr"].shape),
            _resident((1, Q_LORA)),
            _resident(w["w_uq"].shape),
            _resident((1, KV_LORA)),
            _resident(w["w_ukv"].shape),
            _resident((8, LANES)),
            _resident((8, LANES)),
            pl.BlockSpec((tm, LANES), pos),
            pl.BlockSpec((tm, LANES), pos),
            _resident(w["w_g"].shape),
            _resident(w["w_a2"].shape),
            _resident((1, 2 * nk)),
        ],
        out_specs=[
            pl.BlockSpec((tm, MLA_H * QK_PAD), row),
            pl.BlockSpec((tm, MLA_H * QK_PAD), row),
            pl.BlockSpec((vt_rows, tm), lambda i: (0, i)) if transpose_v else pl.BlockSpec((tm, v_dims), row),
        ] + [pl.BlockSpec((tm, width), row) for width, _ in gla_outs],
        out_shape=[
            jax.ShapeDtypeStruct((n, MLA_H * QK_PAD), BF16),
            jax.ShapeDtypeStruct((n, MLA_H * QK_PAD), BF16),
            jax.ShapeDtypeStruct((vt_rows, n) if transpose_v else (n, v_dims), BF16),
        ] + [jax.ShapeDtypeStruct((n, width), dtype) for width, dtype in gla_outs],
        compiler_params=_params("parallel"),
        name="project",
    )(x2d, w["attn_norm"], w["w_cq"], w["w_ckvr"], w["q_a_norm"], w["w_uq"], w["kv_a_norm"],
      w["w_ukv"], w["g_q"], w["g_k"], cos_t, sin_t, w["w_g"], w["w_a2"], w["b_a2"])


def _attn_kernel(q_ref, qn_ref, k_ref, vt_ref, km_ref, vmt_ref, o_ref,
                 s_even, s_odd, acc_scr, qt_scr, qtn_scr, max_scr, *, tk, ck, n_tiles):
    qi = pl.program_id(2)
    n_q = pl.num_programs(2)
    s_scr = (s_even, s_odd)
    n_chunks = tk // ck

    def transposed(ref):
        return ref[0].astype(F32).T.astype(BF16)

    def score_chunk(i, c, slot, running_max, qt_ref):
        off = pl.multiple_of(i * tk + c * ck, ck)
        s = _dot(k_ref[0, pl.ds(off, ck), :], qt_ref[...])
        s_scr[slot][c * ck:(c + 1) * ck, :] = s
        s_max = jnp.max(s, axis=0, keepdims=True)
        return s_max if running_max is None else jnp.maximum(running_max, s_max)

    def gather_chunk(i, c, slot, m, partial):
        off = pl.multiple_of(i * tk + c * ck, ck)
        p = jnp.exp2(s_scr[slot][c * ck:(c + 1) * ck, :] - m).astype(BF16)
        d = _dot(vt_ref[:, pl.ds(off, ck)], p)
        return d if partial is None else partial + d

    def step(i, slot, s_max, m, following="tile"):
        m_new = jnp.maximum(m, s_max)
        next_max, partial = None, None
        for c in range(n_chunks):
            if following == "tile":
                next_max = score_chunk(i + 1, c, 1 - slot, next_max, qt_scr)
            elif following == "block":
                next_max = score_chunk(0, c, 1 - slot, next_max, qtn_scr)
            partial = gather_chunk(i, c, slot, m_new, partial)
        acc_scr[...] = jnp.exp2(m - m_new) * acc_scr[...] + partial
        return next_max, m_new

    def two_steps(j, carry):
        carry = step(2 * j, 0, *carry)
        return step(2 * j + 1, 1, *carry)

    @pl.when(qi == 0)
    def _():
        qt_scr[...] = transposed(q_ref)
        max0 = None
        for c in range(n_chunks):
            max0 = score_chunk(0, c, 0, max0, qt_scr)
        max_scr[...] = max0

    s_meta = _dot(km_ref[...], qt_scr[...])
    m_meta = jnp.max(s_meta, axis=0, keepdims=True)
    acc_scr[...] = _dot(vmt_ref[...], jnp.exp2(s_meta - m_meta).astype(BF16))
    carry = lax.fori_loop(0, n_tiles // 2 - 1, two_steps, (max_scr[...], m_meta), unroll=True)
    max_last, m = step(n_tiles - 2, 0, *carry)

    @pl.when(qi < n_q - 1)
    def _():
        qtn_scr[...] = transposed(qn_ref)
        max_scr[...] = step(n_tiles - 1, 1, max_last, m, following="block")[0]
        qt_scr[...] = qtn_scr[...]

    @pl.when(qi == n_q - 1)
    def _():
        step(n_tiles - 1, 1, max_last, m, following=None)

    acc = acc_scr[...]
    o_ref[0] = (acc[:V_DIM] / acc[V_DIM:V_DIM + 1]).T.astype(o_ref.dtype)


def _attention(q, k, vt, k_meta, vt_meta, tq, tk):
    b, seq, _ = q.shape
    assert seq % (2 * tk) == 0, (seq, tk)
    n_q = seq // tq
    return pl.pallas_call(
        functools.partial(_attn_kernel, tk=tk, ck=min(tk, MXU_DIM), n_tiles=seq // tk),
        grid=(b, MLA_H, n_q),
        in_specs=[
            pl.BlockSpec((1, tq, QK_PAD), lambda bi, hi, qi: (bi, qi, hi)),
            pl.BlockSpec((1, tq, QK_PAD), lambda bi, hi, qi: (bi, jnp.minimum(qi + 1, n_q - 1), hi)),
            pl.BlockSpec((1, seq, QK_PAD), lambda bi, hi, qi: (bi, 0, hi)),
            pl.BlockSpec((VT_ROWS, seq), lambda bi, hi, qi: (hi, bi)),
            pl.BlockSpec((N_META, QK_PAD), lambda bi, hi, qi: (0, hi)),
            pl.BlockSpec((VT_ROWS, N_META), lambda bi, hi, qi: (hi, 0)),
        ],
        out_specs=pl.BlockSpec((1, tq, V_DIM), lambda bi, hi, qi: (bi, qi, hi)),
        out_shape=jax.ShapeDtypeStruct((b, seq, MLA_H * V_DIM), BF16),
        scratch_shapes=[
            pltpu.VMEM((tk, tq), F32),
            pltpu.VMEM((tk, tq), F32),
            pltpu.VMEM((VT_ROWS, tq), F32),
            pltpu.VMEM((QK_PAD, tq), BF16),
            pltpu.VMEM((QK_PAD, tq), BF16),
            pltpu.VMEM((1, tq), F32),
        ],
        compiler_params=_params("arbitrary", "arbitrary", "arbitrary"),
        name="attention",
    )(q, q, k, vt, k_meta, vt_meta)


def _gla_scan_kernel(q_ref, k_ref, v_ref, lg_ref, s0_ref, o_ref, sfin_ref, state, u_scr, dec_scr, qe_scr,
                     *, reverse, n_chunks):
    blk = pl.program_id(1)

    @pl.when(blk == 0)
    def _():
        state[...] = s0_ref[...]

    ri = lax.broadcasted_iota(jnp.int32, (GLA_CHUNK, GLA_CHUNK), 0)
    ci = lax.broadcasted_iota(jnp.int32, (GLA_CHUNK, GLA_CHUNK), 1)
    keep = (ri <= ci) if reverse else (ri >= ci)
    csum = jnp.where(keep, 1.0, 0.0).astype(BF16)
    csum2 = jnp.concatenate([csum, csum], axis=1)
    mid = GLA_CHUNK // 2 if reverse else GLA_CHUNK // 2 - 1
    end = 0 if reverse else GLA_CHUNK - 1
    group = math.gcd(n_chunks, 4)

    heads = [(slice(hd * GLA_DK, (hd + 1) * GLA_DK), slice(hd * GLA_DV, (hd + 1) * GLA_DV)) for hd in range(GLA_H)]

    def local(g, carry):
        cs = [g * group + j for j in range(group)]
        rows = [pl.ds(pl.multiple_of(c * GLA_CHUNK, GLA_CHUNK), GLA_CHUNK) for c in cs]
        bcums = []
        for r in rows:
            lg = lg_ref[0, r, :]
            lg_hi = lg.astype(BF16)
            lg_lo = (lg - lg_hi.astype(F32)).astype(BF16)
            bcums.append(_dot(csum2, jnp.concatenate([lg_hi, lg_lo], axis=0)))
        qss, kss, kds = [], [], []
        for c, r, bcum in zip(cs, rows, bcums):
            b_mid = bcum[mid:mid + 1, :]
            b_end = bcum[end:end + 1, :]
            q = q_ref[0, r, :]
            k = k_ref[0, r, :]
            qss.append((q * jnp.exp(bcum - b_mid)).astype(BF16))
            kss.append((k * jnp.exp(b_mid - bcum)).astype(BF16))
            kds.append((k * jnp.exp(b_end - bcum)).astype(BF16))
            qe_scr[c] = (q * jnp.exp(bcum)).astype(BF16)
            dec_scr[c] = jnp.broadcast_to(jnp.exp(b_end), (8, GLA_H * GLA_DK))
        attn = [[_dot_nt(qs[:, kc], ks[:, kc]) for kc, _ in heads] for qs, ks in zip(qss, kss)]
        for c, r, kd in zip(cs, rows, kds):
            for hd, (kc, vc) in enumerate(heads):
                u_scr[c, hd] = _dot_tn(v_ref[0, r, vc], kd[:, kc])
        for r, a_c in zip(rows, attn):
            for (_, vc), a in zip(heads, a_c):
                o_ref[0, r, vc] = _dot(jnp.where(keep, a, 0.0).astype(BF16), v_ref[0, r, vc])
        return carry

    lax.fori_loop(0, n_chunks // group, local, 0)

    def recur(i, carry):
        c = (n_chunks - 1 - i) if reverse else i
        rows = pl.ds(pl.multiple_of(c * GLA_CHUNK, GLA_CHUNK), GLA_CHUNK)
        for hd, (kc, vc) in enumerate(heads):
            st = state[hd]
            o_ref[0, rows, vc] += _dot_nt(qe_scr[c, :, kc], st.astype(BF16))
            state[hd] = st * dec_scr[c, 0:1, kc] + u_scr[c, hd]
        return carry

    lax.fori_loop(0, n_chunks, recur, 0, unroll=group)

    @pl.when(blk == pl.num_programs(1) - 1)
    def _():
        sfin_ref[0] = state[...]


def _gla_scan(q, k, v, lg, s0, reverse, t_blk):
    b, t, _ = q.shape
    nb = t // t_blk
    lg_col = 1 if reverse else 0
    blk = (lambda bi, i: (bi, nb - 1 - i, 0)) if reverse else (lambda bi, i: (bi, i, 0))
    lg_blk = (lambda bi, i: (bi, nb - 1 - i, lg_col)) if reverse else (lambda bi, i: (bi, i, lg_col))
    st_shape = (GLA_H, GLA_DV, GLA_DK)
    n_chunks = t_blk // GLA_CHUNK
    return pl.pallas_call(
        functools.partial(_gla_scan_kernel, reverse=reverse, n_chunks=n_chunks),
        grid=(b, nb),
        in_specs=[
            pl.BlockSpec((1, t_blk, GLA_H * GLA_DK), blk),
            pl.BlockSpec((1, t_blk, GLA_H * GLA_DK), blk),
            pl.BlockSpec((1, t_blk, GLA_H * GLA_DV), blk),
            pl.BlockSpec((1, t_blk, GLA_H * GLA_DK), lg_blk),
            pl.BlockSpec(st_shape, lambda bi, i: (0, 0, 0)),
        ],
        out_specs=[
            pl.BlockSpec((1, t_blk, GLA_H * GLA_DV), blk),
            pl.BlockSpec((1,) + st_shape, lambda bi, i: (bi, 0, 0, 0)),
        ],
        out_shape=[
            jax.ShapeDtypeStruct((b, t, GLA_H * GLA_DV), F32),
            jax.ShapeDtypeStruct((b,) + st_shape, F32),
        ],
        scratch_shapes=[
            pltpu.VMEM(st_shape, F32),
            pltpu.VMEM((n_chunks,) + st_shape, F32),
            pltpu.VMEM((n_chunks, 8, GLA_H * GLA_DK), F32),
            pltpu.VMEM((n_chunks, GLA_CHUNK, GLA_H * GLA_DK), BF16),
        ],
        compiler_params=_params("parallel", "arbitrary"),
        name="gla_scan_bwd" if reverse else "gla_scan_fwd",
    )(q, k, v, lg, s0)


def _mix_ffn_kernel(x_ref, oa_ref, of_ref, ob_ref, og_ref, gate_ref, womla_ref, onorm_ref, wogla_ref,
                    wout_ref, fn_ref, wgate_ref, wup_ref, wdown_ref, y_ref):
    y_a = _dot(oa_ref[...], womla_ref[...])
    o = of_ref[...] + ob_ref[...]
    og = og_ref[...]
    parts = []
    for hd in range(GLA_H):
        cols = slice(hd * GLA_DV, (hd + 1) * GLA_DV)
        parts.append((_rms(o[:, cols]) * onorm_ref[...] * og[:, cols]).astype(BF16))
    y_b = _dot(jnp.concatenate(parts, axis=-1), wogla_ref[...])
    gates = gate_ref[...]
    mixed = (gates[:, :D_MODEL] * y_a + gates[:, D_MODEL:] * y_b).astype(BF16)
    x1 = x_ref[...] + _dot(mixed, wout_ref[...])
    h = (_rms(x1) * fn_ref[...]).astype(BF16)
    g = _dot(h, wgate_ref[...])
    u = _dot(h, wup_ref[...])
    act = (g * jax.nn.sigmoid(g) * u).astype(BF16)
    y_ref[...] = x1 + _dot(act, wdown_ref[...])


def _mix_ffn(x2d, o_attn, o_f, o_b, og, gates, w, tm):
    n = x2d.shape[0]
    row = lambda i: (i, 0)
    return pl.pallas_call(
        _mix_ffn_kernel,
        grid=(n // tm,),
        in_specs=[
            pl.BlockSpec((tm, D_MODEL), row),
            pl.BlockSpec((tm, MLA_H * V_DIM), row),
            pl.BlockSpec((tm, GLA_H * GLA_DV), row),
            pl.BlockSpec((tm, GLA_H * GLA_DV), row),
            pl.BlockSpec((tm, GLA_H * GLA_DV), row),
            pl.BlockSpec((tm, 2 * D_MODEL), row),
            _resident(w["w_o_mla"].shape),
            _resident((1, GLA_DV)),
            _resident(w["w_o_gla"].shape),
            _resident(w["w_out"].shape),
            _resident((1, D_MODEL)),
            _resident(w["w_ffn_gate"].shape),
            _resident(w["w_ffn_up"].shape),
            _resident(w["w_ffn_down"].shape),
        ],
        out_specs=pl.BlockSpec((tm, D_MODEL), row),
        out_shape=jax.ShapeDtypeStruct((n, D_MODEL), F32),
        compiler_params=_params("parallel"),
        name="mix_ffn",
    )(x2d, o_attn, o_f, o_b, og, gates, w["w_o_mla"], w["gla_o_norm"], w["w_o_gla"], w["w_out"],
      w["ffn_norm"], w["w_ffn_gate"], w["w_ffn_up"], w["w_ffn_down"])


def _prep_weights(attn_norm, w_in, q_a_norm, w_uq, kv_a_norm, w_ukv, q_norm, k_norm, w_o_mla,
                  w_a2_fwd, b_a2_fwd, w_a2_bwd, b_a2_bwd, gla_o_norm, w_o_gla, w_out, ffn_norm,
                  w_ffn_gate, w_ffn_up, w_ffn_down):
    half = ROPE_DIM // 2
    w_in, w_uq, w_ukv = w_in.astype(BF16), w_uq.astype(BF16), w_ukv.astype(BF16)
    o = 0
    cols = {}
    for name, size in (("cq", Q_LORA), ("ckv", KV_LORA), ("kr", ROPE_DIM), ("gq", GLA_H * GLA_DK),
                       ("gk", GLA_H * GLA_DK), ("gv", GLA_H * GLA_DV), ("gg", GLA_H * GLA_DV),
                       ("af", GATE_RANK), ("ab", GATE_RANK), ("ga", D_MODEL), ("gb", D_MODEL)):
        cols[name] = w_in[:, o:o + size]
        o += size
    kr1, kr2 = cols["kr"][:, :half], cols["kr"][:, half:]
    w_ckvr = jnp.concatenate([cols["ckv"], kr1, kr2, kr1, kr2, kr2, kr1, kr2, kr1], axis=1)

    uq = w_uq.reshape(Q_LORA, MLA_H, QK_DIM)
    uq_nope = uq[:, :, :NOPE_DIM].reshape(Q_LORA, MLA_H * NOPE_DIM)
    uq_r = uq[:, :, NOPE_DIM:]
    uq_a = uq_r.reshape(Q_LORA, MLA_H * ROPE_DIM)
    uq_b = jnp.concatenate([uq_r[:, :, half:], uq_r[:, :, :half]], axis=-1).reshape(Q_LORA, MLA_H * ROPE_DIM)
    ukv = w_ukv.reshape(KV_LORA, MLA_H, NOPE_DIM + V_DIM)
    w_ukv2 = jnp.concatenate([ukv[:, :, :NOPE_DIM].reshape(KV_LORA, MLA_H * NOPE_DIM),
                              ukv[:, :, NOPE_DIM:].reshape(KV_LORA, MLA_H * V_DIM)], axis=1)

    def gain_rows(g):
        z = jnp.zeros((ROPE_DIM,), F32)
        ga = g[NOPE_DIM:]
        gb = jnp.concatenate([g[NOPE_DIM + half:], g[NOPE_DIM:NOPE_DIM + half]])
        rows = [g[:NOPE_DIM], jnp.concatenate([ga, z]), jnp.concatenate([z, ga]),
                jnp.concatenate([gb, z]), jnp.concatenate([z, gb])]
        rows += [jnp.zeros((LANES,), F32)] * 3
        return jnp.stack(rows)

    zpad = jnp.zeros((D_MODEL, LANES - 2 * GATE_RANK), BF16)
    w_g = jnp.concatenate([cols["gq"], cols["gk"], cols["gv"], cols["gg"], cols["af"], cols["ab"], zpad,
                           cols["ga"], cols["gb"]], axis=1)
    nk = GLA_H * GLA_DK
    w_a2 = jnp.zeros((LANES, 2 * nk), F32)
    w_a2 = w_a2.at[:GATE_RANK, :nk].set(w_a2_fwd).at[GATE_RANK:2 * GATE_RANK, nk:].set(w_a2_bwd)
    return {
        "attn_norm": attn_norm[None], "w_cq": cols["cq"], "w_ckvr": w_ckvr,
        "q_a_norm": q_a_norm[None], "w_uq": jnp.concatenate([uq_nope, uq_a, uq_b], axis=1),
        "kv_a_norm": kv_a_norm[None], "w_ukv": w_ukv2,
        "g_q": gain_rows(q_norm), "g_k": gain_rows(k_norm),
        "w_g": w_g, "w_a2": w_a2.astype(BF16),
        "b_a2": jnp.concatenate([b_a2_fwd, b_a2_bwd])[None],
        "w_o_mla": w_o_mla.astype(BF16), "gla_o_norm": gla_o_norm[None], "w_o_gla": w_o_gla.astype(BF16),
        "w_out": w_out.astype(BF16), "ffn_norm": ffn_norm[None], "w_ffn_gate": w_ffn_gate.astype(BF16),
        "w_ffn_up": w_ffn_up.astype(BF16), "w_ffn_down": w_ffn_down.astype(BF16),
    }


def _rope_tables(length):
    inv = 1.0 / (ROPE_THETA ** (np.arange(0, ROPE_DIM, 2, dtype=np.float64) / ROPE_DIM))
    ang = np.arange(length, dtype=np.float64)[:, None] * inv[None, :]
    cos, sin = np.cos(ang), np.sin(ang)
    return (jnp.asarray(np.tile(cos, (1, 4)), F32),
            jnp.asarray(np.concatenate([-sin, sin, -sin, sin], axis=1), F32))


def _encode_group(x, w, meta, cos_t, sin_t):
    b, seq, _ = x.shape
    n = b * seq
    x2d = x.reshape(n, D_MODEL)
    tm = _row_tile(seq, 256)
    q, k, vt, gq, gk, gv, og, lg, gates = _project(
        x2d, w, cos_t[N_META:N_META + seq], sin_t[N_META:N_META + seq], seq, tm, True)

    o_attn = _attention(q.reshape(b, seq, -1), k.reshape(b, seq, -1), vt,
                        meta["k"], meta["vt"], _row_tile(seq // 2, 1024), _row_tile(seq // 4, 2048))

    t_blk = _row_tile(seq, 512)
    g3 = lambda a: a.reshape(b, seq, -1)
    o_f, _ = _gla_scan(g3(gq), g3(gk), g3(gv), g3(lg), meta["state"], False, t_blk)
    o_b, _ = _gla_scan(g3(gq), g3(gk), g3(gv), g3(lg), jnp.zeros_like(meta["state"]), True, t_blk)

    y = _mix_ffn(x2d, o_attn.reshape(n, -1), o_f.reshape(n, -1), o_b.reshape(n, -1), og, gates, w, tm)
    return y.reshape(b, seq, D_MODEL)


def kernel(x_prompt, x_sample, meta_tokens, attn_norm, w_in, q_a_norm, w_uq, kv_a_norm, w_ukv, q_norm, k_norm, w_o_mla, w_a2_fwd, b_a2_fwd, w_a2_bwd, b_a2_bwd, gla_o_norm, w_o_gla, w_out, ffn_norm, w_ffn_gate, w_ffn_up, w_ffn_down):
    assert attn_norm.shape[0] == 1, "single-layer encoder"
    w = _prep_weights(attn_norm[0], w_in[0], q_a_norm[0], w_uq[0], kv_a_norm[0], w_ukv[0], q_norm[0],
                      k_norm[0], w_o_mla[0], w_a2_fwd[0], b_a2_fwd[0], w_a2_bwd[0], b_a2_bwd[0],
                      gla_o_norm[0], w_o_gla[0], w_out[0], ffn_norm[0], w_ffn_gate[0], w_ffn_up[0],
                      w_ffn_down[0])
    max_len = N_META + max(x_prompt.shape[1], x_sample.shape[1])
    cos_t, sin_t = _rope_tables(max_len)

    xm = meta_tokens.astype(F32)
    _, k_m, v_m, mq, mk, mv, _, mlg, _ = _project(xm, w, cos_t[:N_META], sin_t[:N_META], N_META, N_META, False)
    pad = lambda a: jnp.pad(a, ((GLA_CHUNK - N_META, 0), (0, 0)))[None]
    _, s_meta = _gla_scan(pad(mq), pad(mk), pad(mv), pad(mlg),
                          jnp.zeros((GLA_H, GLA_DV, GLA_DK), F32), False, GLA_CHUNK)
    vt_m = jnp.concatenate([v_m.T.reshape(MLA_H, V_DIM, N_META), jnp.ones((MLA_H, ONES_ROWS, N_META), BF16)],
                           axis=1).reshape(MLA_H * VT_ROWS, N_META)
    meta = {"k": k_m, "vt": vt_m, "state": s_meta[0]}

    return (_encode_group(x_prompt, w, meta, cos_t, sin_t),
            _encode_group(x_sample, w, meta, cos_t, sin_t))
```

```python
import functools
import math

import jax
import jax.numpy as jnp
import numpy as np
from jax import lax
from jax.experimental import pallas as pl
from jax.experimental.pallas import tpu as pltpu

D_MODEL = 1024
N_META = 16
MLA_H = 8
NOPE_DIM = 128
ROPE_DIM = 64
QK_DIM = NOPE_DIM + ROPE_DIM
V_DIM = 128
Q_LORA = 768
KV_LORA = 256
ROPE_THETA = 10000.0
GLA_H = 4
GLA_DK = 128
GLA_DV = 256
GATE_RANK = 16
GATE_TEMP = 16.0
GLA_CHUNK = 64
D_FF = 2816
NORM_EPS = 1e-6

QK_PAD = 256
ONES_ROWS = 16
VT_ROWS = V_DIM + ONES_ROWS
LANES = 128
MXU_DIM = 256
LOG2E = 1.4426950408889634
VMEM_LIMIT = 56 * 1024 * 1024

F32 = jnp.float32
BF16 = jnp.bfloat16


def _dot(a, b):
    return jnp.dot(a, b, preferred_element_type=F32)


def _dot_nt(a, b):
    return lax.dot_general(a, b, (((1,), (1,)), ((), ())), preferred_element_type=F32)


def _dot_tn(a, b):
    return lax.dot_general(a, b, (((0,), (0,)), ((), ())), preferred_element_type=F32)


def _rms(x):
    return x * lax.rsqrt(jnp.mean(x * x, axis=-1, keepdims=True) + NORM_EPS)


def _resident(shape):
    return pl.BlockSpec(shape, lambda *_: (0,) * len(shape), pipeline_mode=pl.Buffered(1))


def _params(*sem):
    return pltpu.CompilerParams(dimension_semantics=sem, vmem_limit_bytes=VMEM_LIMIT)


def _row_tile(n, want):
    t = min(n, want)
    assert n % t == 0, (n, t)
    return t


G_Q0, G_K0, G_V0, G_G0, G_A0, G_S0, G_END = 0, 512, 1024, 2048, 3072, 3200, 5248


def _proj_kernel(x_ref, an_ref, wcq_ref, wckv_ref, qan_ref, wuq_ref, kvan_ref, wukv_ref, gq_ref, gk_ref,
                 cos_ref, sin_ref, wg_ref, wa2_ref, ba2_ref,
                 q_out, k_out, v_out, gla_q_out, gla_k_out, gla_v_out, og_out, lg_out, gate_out, *, transpose_v):
    h = (_rms(x_ref[...]) * an_ref[...]).astype(BF16)
    cq = _dot(h, wcq_ref[...])
    ckvr = _dot(h, wckv_ref[...])
    gla_q_out[...] = _dot(h, wg_ref[:, G_Q0:G_K0]) * (GLA_DK ** -0.5)
    gla_k_out[...] = _dot(h, wg_ref[:, G_K0:G_V0])
    cqn = (_rms(cq) * qan_ref[...]).astype(BF16)
    ckvn = (_rms(ckvr[:, :KV_LORA]) * kvan_ref[...]).astype(BF16)
    qall = _dot(cqn, wuq_ref[...])
    kv = _dot(ckvn, wukv_ref[...])
    gla_v_out[...] = _dot(h, wg_ref[:, G_V0:G_G0]).astype(BF16)
    g = _dot(h, wg_ref[:, G_G0:G_A0])
    og_out[...] = (g * jax.nn.sigmoid(g)).astype(og_out.dtype)
    a = _dot(h, wg_ref[:, G_A0:G_S0]).astype(BF16)
    pre = _dot(a, wa2_ref[...]) + ba2_ref[...]
    lg_out[...] = jax.nn.log_sigmoid(pre) * (1.0 / GATE_TEMP)
    gate_out[...] = jax.nn.sigmoid(_dot(h, wg_ref[:, G_S0:G_END])).astype(gate_out.dtype)

    v = kv[:, MLA_H * NOPE_DIM:]
    if transpose_v:
        vt = v.T.astype(BF16)
        ones = jnp.ones((ONES_ROWS, v.shape[0]), BF16)
        for hd in range(MLA_H):
            v_out[hd * VT_ROWS:hd * VT_ROWS + V_DIM, :] = vt[hd * V_DIM:(hd + 1) * V_DIM, :]
            v_out[hd * VT_ROWS + V_DIM:(hd + 1) * VT_ROWS, :] = ones
    else:
        v_out[...] = v.astype(BF16)

    cos = cos_ref[...]
    sin = sin_ref[...]
    lane = lax.broadcasted_iota(jnp.int32, (1, LANES), 1)
    ka = ckvr[:, KV_LORA:KV_LORA + LANES]
    kb = ckvr[:, KV_LORA + LANES:KV_LORA + 2 * LANES]
    head_lanes = [((lane >= p * ROPE_DIM) & (lane < (p + 1) * ROPE_DIM)).astype(F32) for p in range(2)]
    ssq_kr = jnp.sum(ka * ka * head_lanes[0], axis=-1, keepdims=True)
    k_rot = [(ka * gk_ref[1 + p:2 + p, :]) * cos + (kb * gk_ref[3 + p:4 + p, :]) * sin for p in range(2)]
    q_scale = QK_DIM ** -0.5 * LOG2E
    rope0 = MLA_H * NOPE_DIM
    for hd in range(MLA_H):
        j, p = divmod(hd, 2)
        qn = qall[:, hd * NOPE_DIM:(hd + 1) * NOPE_DIM]
        qa = qall[:, rope0 + j * LANES:rope0 + (j + 1) * LANES]
        qb = qall[:, rope0 + 4 * LANES + j * LANES:rope0 + 4 * LANES + (j + 1) * LANES]
        ssq = (jnp.sum(qn * qn, axis=-1, keepdims=True)
               + jnp.sum(qa * qa * head_lanes[p], axis=-1, keepdims=True))
        rs = lax.rsqrt(ssq * (1.0 / QK_DIM) + NORM_EPS) * q_scale
        q_rot = (qa * gq_ref[1 + p:2 + p, :]) * cos + (qb * gq_ref[3 + p:4 + p, :]) * sin
        q_out[:, hd * QK_PAD:hd * QK_PAD + LANES] = (qn * gq_ref[0:1, :] * rs).astype(BF16)
        q_out[:, hd * QK_PAD + LANES:(hd + 1) * QK_PAD] = (q_rot * rs).astype(BF16)
        kn = kv[:, hd * NOPE_DIM:(hd + 1) * NOPE_DIM]
        rsk = lax.rsqrt((jnp.sum(kn * kn, axis=-1, keepdims=True) + ssq_kr) * (1.0 / QK_DIM) + NORM_EPS)
        k_out[:, hd * QK_PAD:hd * QK_PAD + LANES] = (kn * gk_ref[0:1, :] * rsk).astype(BF16)
        k_out[:, hd * QK_PAD + LANES:(hd + 1) * QK_PAD] = (k_rot[p] * rsk).astype(BF16)


def _project(x2d, w, cos_t, sin_t, seq_len, tm, transpose_v):
    n = x2d.shape[0]
    nt = seq_len // tm
    row = lambda i: (i, 0)
    pos = lambda i: (i % nt, 0)
    v_dims = MLA_H * V_DIM
    vt_rows = MLA_H * VT_ROWS
    nk, nv = GLA_H * GLA_DK, GLA_H * GLA_DV
    gla_outs = ((nk, F32), (nk, F32), (nv, BF16), (nv, BF16), (2 * nk, F32), (2 * D_MODEL, BF16))
    return pl.pallas_call(
        functools.partial(_proj_kernel, transpose_v=transpose_v),
        grid=(n // tm,),
        in_specs=[
            pl.BlockSpec((tm, D_MODEL), row),
            _resident((1, D_MODEL)),
            _resident(w["w_cq"].shape),
            _resident(w["w_ckvr"].shape),
            _resident((1, Q_LORA)),
            _resident(w["w_uq"].shape),
            _resident((1, KV_LORA)),
            _resident(w["w_ukv"].shape),
            _resident((8, LANES)),
            _resident((8, LANES)),
            pl.BlockSpec((tm, LANES), pos),
            pl.BlockSpec((tm, LANES), pos),
            _resident(w["w_g"].shape),
            _resident(w["w_a2"].shape),
            _resident((1, 2 * nk)),
        ],
        out_specs=[
            pl.BlockSpec((tm, MLA_H * QK_PAD), row),
            pl.BlockSpec((tm, MLA_H * QK_PAD), row),
            pl.BlockSpec((vt_rows, tm), lambda i: (0, i)) if transpose_v else pl.BlockSpec((tm, v_dims), row),
        ] + [pl.BlockSpec((tm, width), row) for width, _ in gla_outs],
        out_shape=[
            jax.ShapeDtypeStruct((n, MLA_H * QK_PAD), BF16),
            jax.ShapeDtypeStruct((n, MLA_H * QK_PAD), BF16),
            jax.ShapeDtypeStruct((vt_rows, n) if transpose_v else (n, v_dims), BF16),
        ] + [jax.ShapeDtypeStruct((n, width), dtype) for width, dtype in gla_outs],
        compiler_params=_params("parallel"),
        name="project",
    )(x2d, w["attn_norm"], w["w_cq"], w["w_ckvr"], w["q_a_norm"], w["w_uq"], w["kv_a_norm"],
      w["w_ukv"], w["g_q"], w["g_k"], cos_t, sin_t, w["w_g"], w["w_a2"], w["b_a2"])


def _attn_kernel(q_ref, qn_ref, k_ref, vt_ref, km_ref, vmt_ref, o_ref,
                 s_even, s_odd, acc_scr, qt_scr, qtn_scr, max_scr, *, tk, ck, n_tiles):
    qi = pl.program_id(2)
    n_q = pl.num_programs(2)
    s_scr = (s_even, s_odd)
    n_chunks = tk // ck

    def transposed(ref):
        return ref[0].astype(F32).T.astype(BF16)

    def score_chunk(i, c, slot, running_max, qt_ref):
        off = pl.multiple_of(i * tk + c * ck, ck)
        s = _dot(k_ref[0, pl.ds(off, ck), :], qt_ref[...])
        s_scr[slot][c * ck:(c + 1) * ck, :] = s
        s_max = jnp.max(s, axis=0, keepdims=True)
        return s_max if running_max is None else jnp.maximum(running_max, s_max)

    def gather_chunk(i, c, slot, m, partial):
        off = pl.multiple_of(i * tk + c * ck, ck)
        p = jnp.exp2(s_scr[slot][c * ck:(c + 1) * ck, :] - m).astype(BF16)
        d = _dot(vt_ref[:, pl.ds(off, ck)], p)
        return d if partial is None else partial + d

    def step(i, slot, s_max, m, following="tile"):
        m_new = jnp.maximum(m, s_max)
        next_max, partial = None, None
        for c in range(n_chunks):
            if following == "tile":
                next_max = score_chunk(i + 1, c, 1 - slot, next_max, qt_scr)
            elif following == "block":
                next_max = score_chunk(0, c, 1 - slot, next_max, qtn_scr)
            partial = gather_chunk(i, c, slot, m_new, partial)
        acc_scr[...] = jnp.exp2(m - m_new) * acc_scr[...] + partial
        return next_max, m_new

    def two_steps(j, carry):
        carry = step(2 * j, 0, *carry)
        return step(2 * j + 1, 1, *carry)

    @pl.when(qi == 0)
    def _():
        qt_scr[...] = transposed(q_ref)
        max0 = None
        for c in range(n_chunks):
            max0 = score_chunk(0, c, 0, max0, qt_scr)
        max_scr[...] = max0

    s_meta = _dot(km_ref[...], qt_scr[...])
    m_meta = jnp.max(s_meta, axis=0, keepdims=True)
    acc_scr[...] = _dot(vmt_ref[...], jnp.exp2(s_meta - m_meta).astype(BF16))
    carry = lax.fori_loop(0, n_tiles // 2 - 1, two_steps, (max_scr[...], m_meta), unroll=True)
    max_last, m = step(n_tiles - 2, 0, *carry)

    @pl.when(qi < n_q - 1)
    def _():
        qtn_scr[...] = transposed(qn_ref)
        max_scr[...] = step(n_tiles - 1, 1, max_last, m, following="block")[0]
        qt_scr[...] = qtn_scr[...]

    @pl.when(qi == n_q - 1)
    def _():
        step(n_tiles - 1, 1, max_last, m, following=None)

    acc = acc_scr[...]
    o_ref[0] = (acc[:V_DIM] / acc[V_DIM:V_DIM + 1]).T.astype(o_ref.dtype)


def _attention(q, k, vt, k_meta, vt_meta, tq, tk):
    b, seq, _ = q.shape
    assert seq % (2 * tk) == 0, (seq, tk)
    n_q = seq // tq
    return pl.pallas_call(
        functools.partial(_attn_kernel, tk=tk, ck=min(tk, MXU_DIM), n_tiles=seq // tk),
        grid=(b, MLA_H, n_q),
        in_specs=[
            pl.BlockSpec((1, tq, QK_PAD), lambda bi, hi, qi: (bi, qi, hi)),
            pl.BlockSpec((1, tq, QK_PAD), lambda bi, hi, qi: (bi, jnp.minimum(qi + 1, n_q - 1), hi)),
            pl.BlockSpec((1, seq, QK_PAD), lambda bi, hi, qi: (bi, 0, hi)),
            pl.BlockSpec((VT_ROWS, seq), lambda bi, hi, qi: (hi, bi)),
            pl.BlockSpec((N_META, QK_PAD), lambda bi, hi, qi: (0, hi)),
            pl.BlockSpec((VT_ROWS, N_META), lambda bi, hi, qi: (hi, 0)),
        ],
        out_specs=pl.BlockSpec((1, tq, V_DIM), lambda bi, hi, qi: (bi, qi, hi)),
        out_shape=jax.ShapeDtypeStruct((b, seq, MLA_H * V_DIM), BF16),
        scratch_shapes=[
            pltpu.VMEM((tk, tq), F32),
            pltpu.VMEM((tk, tq), F32),
            pltpu.VMEM((VT_ROWS, tq), F32),
            pltpu.VMEM((QK_PAD, tq), BF16),
            pltpu.VMEM((QK_PAD, tq), BF16),
            pltpu.VMEM((1, tq), F32),
        ],
        compiler_params=_params("arbitrary", "arbitrary", "arbitrary"),
        name="attention",
    )(q, q, k, vt, k_meta, vt_meta)


def _gla_scan_kernel(q_ref, k_ref, v_ref, lg_ref, s0_ref, o_ref, sfin_ref, state, u_scr, dec_scr, qe_scr,
                     *, reverse, n_chunks):
    blk = pl.program_id(1)

    @pl.when(blk == 0)
    def _():
        state[...] = s0_ref[...]

    ri = lax.broadcasted_iota(jnp.int32, (GLA_CHUNK, GLA_CHUNK), 0)
    ci = lax.broadcasted_iota(jnp.int32, (GLA_CHUNK, GLA_CHUNK), 1)
    keep = (ri <= ci) if reverse else (ri >= ci)
    csum = jnp.where(keep, 1.0, 0.0).astype(BF16)
    csum2 = jnp.concatenate([csum, csum], axis=1)
    mid = GLA_CHUNK // 2 if reverse else GLA_CHUNK // 2 - 1
    end = 0 if reverse else GLA_CHUNK - 1
    group = math.gcd(n_chunks, 4)

    heads = [(slice(hd * GLA_DK, (hd + 1) * GLA_DK), slice(hd * GLA_DV, (hd + 1) * GLA_DV)) for hd in range(GLA_H)]

    def local(g, carry):
        cs = [g * group + j for j in range(group)]
        rows = [pl.ds(pl.multiple_of(c * GLA_CHUNK, GLA_CHUNK), GLA_CHUNK) for c in cs]
        bcums = []
        for r in rows:
            lg = lg_ref[0, r, :]
            lg_hi = lg.astype(BF16)
            lg_lo = (lg - lg_hi.astype(F32)).astype(BF16)
            bcums.append(_dot(csum2, jnp.concatenate([lg_hi, lg_lo], axis=0)))
        qss, kss, kds = [], [], []
        for c, r, bcum in zip(cs, rows, bcums):
            b_mid = bcum[mid:mid + 1, :]
            b_end = bcum[end:end + 1, :]
            q = q_ref[0, r, :]
            k = k_ref[0, r, :]
            qss.append((q * jnp.exp(bcum - b_mid)).astype(BF16))
            kss.append((k * jnp.exp(b_mid - bcum)).astype(BF16))
            kds.append((k * jnp.exp(b_end - bcum)).astype(BF16))
            qe_scr[c] = (q * jnp.exp(bcum)).astype(BF16)
            dec_scr[c] = jnp.broadcast_to(jnp.exp(b_end), (8, GLA_H * GLA_DK))
        attn = [[_dot_nt(qs[:, kc], ks[:, kc]) for kc, _ in heads] for qs, ks in zip(qss, kss)]
        for c, r, kd in zip(cs, rows, kds):
            for hd, (kc, vc) in enumerate(heads):
                u_scr[c, hd] = _dot_tn(v_ref[0, r, vc], kd[:, kc])
        for r, a_c in zip(rows, attn):
            for (_, vc), a in zip(heads, a_c):
                o_ref[0, r, vc] = _dot(jnp.where(keep, a, 0.0).astype(BF16), v_ref[0, r, vc])
        return carry

    lax.fori_loop(0, n_chunks // group, local, 0, unroll=True)

    def recur(i, carry):
        c = (n_chunks - 1 - i) if reverse else i
        rows = pl.ds(pl.multiple_of(c * GLA_CHUNK, GLA_CHUNK), GLA_CHUNK)
        for hd, (kc, vc) in enumerate(heads):
            st = state[hd]
            o_ref[0, rows, vc] += _dot_nt(qe_scr[c, :, kc], st.astype(BF16))
            state[hd] = st * dec_scr[c, 0:1, kc] + u_scr[c, hd]
        return carry

    lax.fori_loop(0, n_chunks, recur, 0, unroll=True)

    @pl.when(blk == pl.num_programs(1) - 1)
    def _():
        sfin_ref[0] = state[...]


def _gla_scan(q, k, v, lg, s0, reverse, t_blk):
    b, t, _ = q.shape
    nb = t // t_blk
    lg_col = 1 if reverse else 0
    blk = (lambda bi, i: (bi, nb - 1 - i, 0)) if reverse else (lambda bi, i: (bi, i, 0))
    lg_blk = (lambda bi, i: (bi, nb - 1 - i, lg_col)) if reverse else (lambda bi, i: (bi, i, lg_col))
    st_shape = (GLA_H, GLA_DV, GLA_DK)
    n_chunks = t_blk // GLA_CHUNK
    return pl.pallas_call(
        functools.partial(_gla_scan_kernel, reverse=reverse, n_chunks=n_chunks),
        grid=(b, nb),
        in_specs=[
            pl.BlockSpec((1, t_blk, GLA_H * GLA_DK), blk),
            pl.BlockSpec((1, t_blk, GLA_H * GLA_DK), blk),
            pl.BlockSpec((1, t_blk, GLA_H * GLA_DV), blk),
            pl.BlockSpec((1, t_blk, GLA_H * GLA_DK), lg_blk),
            pl.BlockSpec(st_shape, lambda bi, i: (0, 0, 0)),
        ],
        out_specs=[
            pl.BlockSpec((1, t_blk, GLA_H * GLA_DV), blk),
            pl.BlockSpec((1,) + st_shape, lambda bi, i: (bi, 0, 0, 0)),
        ],
        out_shape=[
            jax.ShapeDtypeStruct((b, t, GLA_H * GLA_DV), F32),
            jax.ShapeDtypeStruct((b,) + st_shape, F32),
        ],
        scratch_shapes=[
            pltpu.VMEM(st_shape, F32),
            pltpu.VMEM((n_chunks,) + st_shape, F32),
            pltpu.VMEM((n_chunks, 8, GLA_H * GLA_DK), F32),
            pltpu.VMEM((n_chunks, GLA_CHUNK, GLA_H * GLA_DK), BF16),
        ],
        compiler_params=_params("parallel", "arbitrary"),
        name="gla_scan_bwd" if reverse else "gla_scan_fwd",
    )(q, k, v, lg, s0)


def _mix_ffn_kernel(x_ref, oa_ref, of_ref, ob_ref, og_ref, gate_ref, womla_ref, onorm_ref, wogla_ref,
                    wout_ref, fn_ref, wgate_ref, wup_ref, wdown_ref, y_ref):
    y_a = _dot(oa_ref[...], womla_ref[...])
    o = of_ref[...] + ob_ref[...]
    og = og_ref[...]
    parts = []
    for hd in range(GLA_H):
        cols = slice(hd * GLA_DV, (hd + 1) * GLA_DV)
        parts.append((_rms(o[:, cols]) * onorm_ref[...] * og[:, cols]).astype(BF16))
    y_b = _dot(jnp.concatenate(parts, axis=-1), wogla_ref[...])
    gates = gate_ref[...]
    mixed = (gates[:, :D_MODEL] * y_a + gates[:, D_MODEL:] * y_b).astype(BF16)
    x1 = x_ref[...] + _dot(mixed, wout_ref[...])
    h = (_rms(x1) * fn_ref[...]).astype(BF16)
    g = _dot(h, wgate_ref[...])
    u = _dot(h, wup_ref[...])
    act = (g * jax.nn.sigmoid(g) * u).astype(BF16)
    y_ref[...] = x1 + _dot(act, wdown_ref[...])


def _mix_ffn(x2d, o_attn, o_f, o_b, og, gates, w, tm):
    n = x2d.shape[0]
    row = lambda i: (i, 0)
    return pl.pallas_call(
        _mix_ffn_kernel,
        grid=(n // tm,),
        in_specs=[
            pl.BlockSpec((tm, D_MODEL), row),
            pl.BlockSpec((tm, MLA_H * V_DIM), row),
            pl.BlockSpec((tm, GLA_H * GLA_DV), row),
            pl.BlockSpec((tm, GLA_H * GLA_DV), row),
            pl.BlockSpec((tm, GLA_H * GLA_DV), row),
            pl.BlockSpec((tm, 2 * D_MODEL), row),
            _resident(w["w_o_mla"].shape),
            _resident((1, GLA_DV)),
            _resident(w["w_o_gla"].shape),
            _resident(w["w_out"].shape),
            _resident((1, D_MODEL)),
            _resident(w["w_ffn_gate"].shape),
            _resident(w["w_ffn_up"].shape),
            _resident(w["w_ffn_down"].shape),
        ],
        out_specs=pl.BlockSpec((tm, D_MODEL), row),
        out_shape=jax.ShapeDtypeStruct((n, D_MODEL), F32),
        compiler_params=_params("parallel"),
        name="mix_ffn",
    )(x2d, o_attn, o_f, o_b, og, gates, w["w_o_mla"], w["gla_o_norm"], w["w_o_gla"], w["w_out"],
      w["ffn_norm"], w["w_ffn_gate"], w["w_ffn_up"], w["w_ffn_down"])


def _prep_weights(attn_norm, w_in, q_a_norm, w_uq, kv_a_norm, w_ukv, q_norm, k_norm, w_o_mla,
                  w_a2_fwd, b_a2_fwd, w_a2_bwd, b_a2_bwd, gla_o_norm, w_o_gla, w_out, ffn_norm,
                  w_ffn_gate, w_ffn_up, w_ffn_down):
    half = ROPE_DIM // 2
    w_in, w_uq, w_ukv = w_in.astype(BF16), w_uq.astype(BF16), w_ukv.astype(BF16)
    o = 0
    cols = {}
    for name, size in (("cq", Q_LORA), ("ckv", KV_LORA), ("kr", ROPE_DIM), ("gq", GLA_H * GLA_DK),
                       ("gk", GLA_H * GLA_DK), ("gv", GLA_H * GLA_DV), ("gg", GLA_H * GLA_DV),
                       ("af", GATE_RANK), ("ab", GATE_RANK), ("ga", D_MODEL), ("gb", D_MODEL)):
        cols[name] = w_in[:, o:o + size]
        o += size
    kr1, kr2 = cols["kr"][:, :half], cols["kr"][:, half:]
    w_ckvr = jnp.concatenate([cols["ckv"], kr1, kr2, kr1, kr2, kr2, kr1, kr2, kr1], axis=1)

    uq = w_uq.reshape(Q_LORA, MLA_H, QK_DIM)
    uq_nope = uq[:, :, :NOPE_DIM].reshape(Q_LORA, MLA_H * NOPE_DIM)
    uq_r = uq[:, :, NOPE_DIM:]
    uq_a = uq_r.reshape(Q_LORA, MLA_H * ROPE_DIM)
    uq_b = jnp.concatenate([uq_r[:, :, half:], uq_r[:, :, :half]], axis=-1).reshape(Q_LORA, MLA_H * ROPE_DIM)
    ukv = w_ukv.reshape(KV_LORA, MLA_H, NOPE_DIM + V_DIM)
    w_ukv2 = jnp.concatenate([ukv[:, :, :NOPE_DIM].reshape(KV_LORA, MLA_H * NOPE_DIM),
                              ukv[:, :, NOPE_DIM:].reshape(KV_LORA, MLA_H * V_DIM)], axis=1)

    def gain_rows(g):
        z = jnp.zeros((ROPE_DIM,), F32)
        ga = g[NOPE_DIM:]
        gb = jnp.concatenate([g[NOPE_DIM + half:], g[NOPE_DIM:NOPE_DIM + half]])
        rows = [g[:NOPE_DIM], jnp.concatenate([ga, z]), jnp.concatenate([z, ga]),
                jnp.concatenate([gb, z]), jnp.concatenate([z, gb])]
        rows += [jnp.zeros((LANES,), F32)] * 3
        return jnp.stack(rows)

    zpad = jnp.zeros((D_MODEL, LANES - 2 * GATE_RANK), BF16)
    w_g = jnp.concatenate([cols["gq"], cols["gk"], cols["gv"], cols["gg"], cols["af"], cols["ab"], zpad,
                           cols["ga"], cols["gb"]], axis=1)
    nk = GLA_H * GLA_DK
    w_a2 = jnp.zeros((LANES, 2 * nk), F32)
    w_a2 = w_a2.at[:GATE_RANK, :nk].set(w_a2_fwd).at[GATE_RANK:2 * GATE_RANK, nk:].set(w_a2_bwd)
    return {
        "attn_norm": attn_norm[None], "w_cq": cols["cq"], "w_ckvr": w_ckvr,
        "q_a_norm": q_a_norm[None], "w_uq": jnp.concatenate([uq_nope, uq_a, uq_b], axis=1),
        "kv_a_norm": kv_a_norm[None], "w_ukv": w_ukv2,
        "g_q": gain_rows(q_norm), "g_k": gain_rows(k_norm),
        "w_g": w_g, "w_a2": w_a2.astype(BF16),
        "b_a2": jnp.concatenate([b_a2_fwd, b_a2_bwd])[None],
        "w_o_mla": w_o_mla.astype(BF16), "gla_o_norm": gla_o_norm[None], "w_o_gla": w_o_gla.astype(BF16),
        "w_out": w_out.astype(BF16), "ffn_norm": ffn_norm[None], "w_ffn_gate": w_ffn_gate.astype(BF16),
        "w_ffn_up": w_ffn_up.astype(BF16), "w_ffn_down": w_ffn_down.astype(BF16),
    }


def _rope_tables(length):
    inv = 1.0 / (ROPE_THETA ** (np.arange(0, ROPE_DIM, 2, dtype=np.float64) / ROPE_DIM))
    ang = np.arange(length, dtype=np.float64)[:, None] * inv[None, :]
    cos, sin = np.cos(ang), np.sin(ang)
    return (jnp.asarray(np.tile(cos, (1, 4)), F32),
            jnp.asarray(np.concatenate([-sin, sin, -sin, sin], axis=1), F32))


def _encode_group(x, w, meta, cos_t, sin_t):
    b, seq, _ = x.shape
    n = b * seq
    x2d = x.reshape(n, D_MODEL)
    tm = _row_tile(seq, 256)
    q, k, vt, gq, gk, gv, og, lg, gates = _project(
        x2d, w, cos_t[N_META:N_META + seq], sin_t[N_META:N_META + seq], seq, tm, True)

    o_attn = _attention(q.reshape(b, seq, -1), k.reshape(b, seq, -1), vt,
                        meta["k"], meta["vt"], _row_tile(seq // 2, 1024), _row_tile(seq // 4, 2048))

    t_blk = _row_tile(seq, 1024)
    g3 = lambda a: a.reshape(b, seq, -1)
    o_f, _ = _gla_scan(g3(gq), g3(gk), g3(gv), g3(lg), meta["state"], False, t_blk)
    o_b, _ = _gla_scan(g3(gq), g3(gk), g3(gv), g3(lg), jnp.zeros_like(meta["state"]), True, t_blk)

    y = _mix_ffn(x2d, o_attn.reshape(n, -1), o_f.reshape(n, -1), o_b.reshape(n, -1), og, gates, w, tm)
    return y.reshape(b, seq, D_MODEL)


def kernel(x_prompt, x_sample, meta_tokens, attn_norm, w_in, q_a_norm, w_uq, kv_a_norm, w_ukv, q_norm, k_norm, w_o_mla, w_a2_fwd, b_a2_fwd, w_a2_bwd, b_a2_bwd, gla_o_norm, w_o_gla, w_out, ffn_norm, w_ffn_gate, w_ffn_up, w_ffn_down):
    assert attn_norm.shape[0] == 1, "single-layer encoder"
    w = _prep_weights(attn_norm[0], w_in[0], q_a_norm[0], w_uq[0], kv_a_norm[0], w_ukv[0], q_norm[0],
                      k_norm[0], w_o_mla[0], w_a2_fwd[0], b_a2_fwd[0], w_a2_bwd[0], b_a2_bwd[0],
                      gla_o_norm[0], w_o_gla[0], w_out[0], ffn_norm[0], w_ffn_gate[0], w_ffn_up[0],
                      w_ffn_down[0])
    max_len = N_META + max(x_prompt.shape[1], x_sample.shape[1])
    cos_t, sin_t = _rope_tables(max_len)

    xm = meta_tokens.astype(F32)
    _, k_m, v_m, mq, mk, mv, _, mlg, _ = _project(xm, w, cos_t[:N_META], sin_t[:N_META], N_META, N_META, False)
    pad = lambda a: jnp.pad(a, ((GLA_CHUNK - N_META, 0), (0, 0)))[None]
    _, s_meta = _gla_scan(pad(mq), pad(mk), pad(mv), pad(mlg),
                          jnp.zeros((GLA_H, GLA_DV, GLA_DK), F32), False, GLA_CHUNK)
    vt_m = jnp.concatenate([v_m.T.reshape(MLA_H, V_DIM, N_META), jnp.ones((MLA_H, ONES_ROWS, N_META), BF16)],
                           axis=1).reshape(MLA_H * VT_ROWS, N_META)
    meta = {"k": k_m, "vt": vt_m, "state": s_meta[0]}

    return (_encode_group(x_prompt, w, meta, cos_t, sin_t),
            _encode_group(x_sample, w, meta, cos_t, sin_t))
```

```python
import functools
import math

import jax
import jax.numpy as jnp
import numpy as np
from jax import lax
from jax.experimental import pallas as pl
from jax.experimental.pallas import tpu as pltpu

D_MODEL = 1024
N_META = 16
MLA_H = 8
NOPE_DIM = 128
ROPE_DIM = 64
QK_DIM = NOPE_DIM + ROPE_DIM
V_DIM = 128
Q_LORA = 768
KV_LORA = 256
ROPE_THETA = 10000.0
GLA_H = 4
GLA_DK = 128
GLA_DV = 256
GATE_RANK = 16
GATE_TEMP = 16.0
GLA_CHUNK = 64
NORM_EPS = 1e-6

QK_PAD = 256
ONES_ROWS = 16
VT_ROWS = V_DIM + ONES_ROWS
LANES = 128
MXU_DIM = 256
LOG2E = 1.4426950408889634
VMEM_LIMIT = 56 * 1024 * 1024

F32 = jnp.float32
BF16 = jnp.bfloat16


def _dot(a, b):
    return jnp.dot(a, b, preferred_element_type=F32)


def _dot_nt(a, b):
    return lax.dot_general(a, b, (((1,), (1,)), ((), ())), preferred_element_type=F32)


def _dot_tn(a, b):
    return lax.dot_general(a, b, (((0,), (0,)), ((), ())), preferred_element_type=F32)


def _rms(x):
    return x * lax.rsqrt(jnp.mean(x * x, axis=-1, keepdims=True) + NORM_EPS)


def _resident(shape):
    return pl.BlockSpec(shape, lambda *_: (0,) * len(shape), pipeline_mode=pl.Buffered(1))


def _params(*sem):
    return pltpu.CompilerParams(dimension_semantics=sem, vmem_limit_bytes=VMEM_LIMIT)


def _row_tile(n, want):
    t = min(n, want)
    assert n % t == 0, (n, t)
    return t


G_Q0 = 0
G_K0 = G_Q0 + GLA_H * GLA_DK
G_V0 = G_K0 + GLA_H * GLA_DK
G_G0 = G_V0 + GLA_H * GLA_DV
G_A0 = G_G0 + GLA_H * GLA_DV
G_S0 = G_A0 + LANES
G_END = G_S0 + 2 * D_MODEL


def _proj_kernel(x_ref, an_ref, wcq_ref, wckv_ref, qan_ref, wuq_ref, kvan_ref, wukv_ref, gq_ref, gk_ref,
                 cos_ref, sin_ref, wg_ref, wa2_ref, ba2_ref,
                 q_out, k_out, v_out, gla_q_out, gla_k_out, gla_v_out, og_out, lg_out, gate_out, *, transpose_v):
    h = (_rms(x_ref[...]) * an_ref[...]).astype(BF16)
    cq = _dot(h, wcq_ref[...])
    ckvr = _dot(h, wckv_ref[...])
    gla_q_out[...] = _dot(h, wg_ref[:, G_Q0:G_K0]) * (GLA_DK ** -0.5)
    gla_k_out[...] = _dot(h, wg_ref[:, G_K0:G_V0])
    cqn = (_rms(cq) * qan_ref[...]).astype(BF16)
    ckvn = (_rms(ckvr[:, :KV_LORA]) * kvan_ref[...]).astype(BF16)
    qall = _dot(cqn, wuq_ref[...])
    kv = _dot(ckvn, wukv_ref[...])
    gla_v_out[...] = _dot(h, wg_ref[:, G_V0:G_G0]).astype(BF16)
    g = _dot(h, wg_ref[:, G_G0:G_A0])
    og_out[...] = (g * jax.nn.sigmoid(g)).astype(og_out.dtype)
    a = _dot(h, wg_ref[:, G_A0:G_S0]).astype(BF16)
    pre = _dot(a, wa2_ref[...]) + ba2_ref[...]
    lg_out[...] = jax.nn.log_sigmoid(pre) * (1.0 / GATE_TEMP)
    gate_out[...] = jax.nn.sigmoid(_dot(h, wg_ref[:, G_S0:G_END])).astype(gate_out.dtype)

    v = kv[:, MLA_H * NOPE_DIM:]
    if transpose_v:
        vt = v.T.astype(BF16)
        ones = jnp.ones((ONES_ROWS, v.shape[0]), BF16)
        for hd in range(MLA_H):
            v_out[hd * VT_ROWS:hd * VT_ROWS + V_DIM, :] = vt[hd * V_DIM:(hd + 1) * V_DIM, :]
            v_out[hd * VT_ROWS + V_DIM:(hd + 1) * VT_ROWS, :] = ones
    else:
        v_out[...] = v.astype(BF16)

    cos = cos_ref[...]
    sin = sin_ref[...]
    lane = lax.broadcasted_iota(jnp.int32, (1, LANES), 1)
    ka = ckvr[:, KV_LORA:KV_LORA + LANES]
    kb = ckvr[:, KV_LORA + LANES:KV_LORA + 2 * LANES]
    head_lanes = [((lane >= p * ROPE_DIM) & (lane < (p + 1) * ROPE_DIM)).astype(F32) for p in range(2)]
    ssq_kr = jnp.sum(ka * ka * head_lanes[0], axis=-1, keepdims=True)
    k_rot = [(ka * gk_ref[1 + p:2 + p, :]) * cos + (kb * gk_ref[3 + p:4 + p, :]) * sin for p in range(2)]
    q_scale = QK_DIM ** -0.5 * LOG2E
    rope0 = MLA_H * NOPE_DIM
    for hd in range(MLA_H):
        j, p = divmod(hd, 2)
        qn = qall[:, hd * NOPE_DIM:(hd + 1) * NOPE_DIM]
        qa = qall[:, rope0 + j * LANES:rope0 + (j + 1) * LANES]
        qb = qall[:, rope0 + 4 * LANES + j * LANES:rope0 + 4 * LANES + (j + 1) * LANES]
        ssq = (jnp.sum(qn * qn, axis=-1, keepdims=True)
               + jnp.sum(qa * qa * head_lanes[p], axis=-1, keepdims=True))
        rs = lax.rsqrt(ssq * (1.0 / QK_DIM) + NORM_EPS) * q_scale
        q_rot = (qa * gq_ref[1 + p:2 + p, :]) * cos + (qb * gq_ref[3 + p:4 + p, :]) * sin
        q_out[:, hd * QK_PAD:hd * QK_PAD + LANES] = (qn * gq_ref[0:1, :] * rs).astype(BF16)
        q_out[:, hd * QK_PAD + LANES:(hd + 1) * QK_PAD] = (q_rot * rs).astype(BF16)
        kn = kv[:, hd * NOPE_DIM:(hd + 1) * NOPE_DIM]
        rsk = lax.rsqrt((jnp.sum(kn * kn, axis=-1, keepdims=True) + ssq_kr) * (1.0 / QK_DIM) + NORM_EPS)
        k_out[:, hd * QK_PAD:hd * QK_PAD + LANES] = (kn * gk_ref[0:1, :] * rsk).astype(BF16)
        k_out[:, hd * QK_PAD + LANES:(hd + 1) * QK_PAD] = (k_rot[p] * rsk).astype(BF16)


def _project(x2d, w, cos_t, sin_t, seq_len, tm, transpose_v):
    n = x2d.shape[0]
    nt = seq_len // tm
    row = lambda i: (i, 0)
    pos = lambda i: (i % nt, 0)
    v_dims = MLA_H * V_DIM
    vt_rows = MLA_H * VT_ROWS
    nk, nv = GLA_H * GLA_DK, GLA_H * GLA_DV
    gla_outs = ((nk, F32), (nk, F32), (nv, BF16), (nv, BF16), (2 * nk, F32), (2 * D_MODEL, BF16))
    return pl.pallas_call(
        functools.partial(_proj_kernel, transpose_v=transpose_v),
        grid=(n // tm,),
        in_specs=[
            pl.BlockSpec((tm, D_MODEL), row),
            _resident((1, D_MODEL)),
            _resident(w["w_cq"].shape),
            _resident(w["w_ckvr"].shape),
            _resident((1, Q_LORA)),
            _resident(w["w_uq"].shape),
            _resident((1, KV_LORA)),
            _resident(w["w_ukv"].shape),
            _resident((8, LANES)),
            _resident((8, LANES)),
            pl.BlockSpec((tm, LANES), pos),
            pl.BlockSpec((tm, LANES), pos),
            _resident(w["w_g"].shape),
            _resident(w["w_a2"].shape),
            _resident((1, 2 * nk)),
        ],
        out_specs=[
            pl.BlockSpec((tm, MLA_H * QK_PAD), row),
            pl.BlockSpec((tm, MLA_H * QK_PAD), row),
            pl.BlockSpec((vt_rows, tm), lambda i: (0, i)) if transpose_v else pl.BlockSpec((tm, v_dims), row),
        ] + [pl.BlockSpec((tm, width), row) for width, _ in gla_outs],
        out_shape=[
            jax.ShapeDtypeStruct((n, MLA_H * QK_PAD), BF16),
            jax.ShapeDtypeStruct((n, MLA_H * QK_PAD), BF16),
            jax.ShapeDtypeStruct((vt_rows, n) if transpose_v else (n, v_dims), BF16),
        ] + [jax.ShapeDtypeStruct((n, width), dtype) for width, dtype in gla_outs],
        compiler_params=_params("parallel"),
        name="project",
    )(x2d, w["attn_norm"], w["w_cq"], w["w_ckvr"], w["q_a_norm"], w["w_uq"], w["kv_a_norm"],
      w["w_ukv"], w["g_q"], w["g_k"], cos_t, sin_t, w["w_g"], w["w_a2"], w["b_a2"])


def _attn_kernel(q_ref, qn_ref, k_ref, vt_ref, km_ref, vmt_ref, o_ref,
                 s_even, s_odd, acc_scr, qt_scr, qtn_scr, max_scr, *, tk, ck, n_tiles):
    qi = pl.program_id(2)
    n_q = pl.num_programs(2)
    s_scr = (s_even, s_odd)
    n_chunks = tk // ck

    def transposed(ref):
        return ref[0].astype(F32).T.astype(BF16)

    def score_chunk(i, c, slot, running_max, qt_ref):
        off = pl.multiple_of(i * tk + c * ck, ck)
        s = _dot(k_ref[0, pl.ds(off, ck), :], qt_ref[...])
        s_scr[slot][c * ck:(c + 1) * ck, :] = s
        s_max = jnp.max(s, axis=0, keepdims=True)
        return s_max if running_max is None else jnp.maximum(running_max, s_max)

    def gather_chunk(i, c, slot, m, partial):
        off = pl.multiple_of(i * tk + c * ck, ck)
        p = jnp.exp2(s_scr[slot][c * ck:(c + 1) * ck, :] - m).astype(BF16)
        d = _dot(vt_ref[:, pl.ds(off, ck)], p)
        return d if partial is None else partial + d

    def step(i, slot, s_max, m, following="tile"):
        m_new = jnp.maximum(m, s_max)
        next_max, partial = None, None
        for c in range(n_chunks):
            if following == "tile":
                next_max = score_chunk(i + 1, c, 1 - slot, next_max, qt_scr)
            elif following == "block":
                next_max = score_chunk(0, c, 1 - slot, next_max, qtn_scr)
            partial = gather_chunk(i, c, slot, m_new, partial)
        acc_scr[...] = jnp.exp2(m - m_new) * acc_scr[...] + partial
        return next_max, m_new

    def two_steps(j, carry):
        carry = step(2 * j, 0, *carry)
        return step(2 * j + 1, 1, *carry)

    @pl.when(qi == 0)
    def _():
        qt_scr[...] = transposed(q_ref)
        max0 = None
        for c in range(n_chunks):
            max0 = score_chunk(0, c, 0, max0, qt_scr)
        max_scr[...] = max0

    s_meta = _dot(km_ref[...], qt_scr[...])
    m_meta = jnp.max(s_meta, axis=0, keepdims=True)
    acc_scr[...] = _dot(vmt_ref[...], jnp.exp2(s_meta - m_meta).astype(BF16))
    carry = lax.fori_loop(0, n_tiles // 2 - 1, two_steps, (max_scr[...], m_meta), unroll=True)
    max_last, m = step(n_tiles - 2, 0, *carry)

    @pl.when(qi < n_q - 1)
    def _():
        qtn_scr[...] = transposed(qn_ref)
        max_scr[...] = step(n_tiles - 1, 1, max_last, m, following="block")[0]
        qt_scr[...] = qtn_scr[...]

    @pl.when(qi == n_q - 1)
    def _():
        step(n_tiles - 1, 1, max_last, m, following=None)

    acc = acc_scr[...]
    o_ref[0] = (acc[:V_DIM] / acc[V_DIM:V_DIM + 1]).T.astype(o_ref.dtype)


def _attention(q, k, vt, k_meta, vt_meta, tq, tk):
    b, seq, _ = q.shape
    assert seq % (2 * tk) == 0, (seq, tk)
    n_q = seq // tq
    return pl.pallas_call(
        functools.partial(_attn_kernel, tk=tk, ck=min(tk, MXU_DIM), n_tiles=seq // tk),
        grid=(b, MLA_H, n_q),
        in_specs=[
            pl.BlockSpec((1, tq, QK_PAD), lambda bi, hi, qi: (bi, qi, hi)),
            pl.BlockSpec((1, tq, QK_PAD), lambda bi, hi, qi: (bi, jnp.minimum(qi + 1, n_q - 1), hi)),
            pl.BlockSpec((1, seq, QK_PAD), lambda bi, hi, qi: (bi, 0, hi)),
            pl.BlockSpec((VT_ROWS, seq), lambda bi, hi, qi: (hi, bi)),
            pl.BlockSpec((N_META, QK_PAD), lambda bi, hi, qi: (0, hi)),
            pl.BlockSpec((VT_ROWS, N_META), lambda bi, hi, qi: (hi, 0)),
        ],
        out_specs=pl.BlockSpec((1, tq, V_DIM), lambda bi, hi, qi: (bi, qi, hi)),
        out_shape=jax.ShapeDtypeStruct((b, seq, MLA_H * V_DIM), BF16),
        scratch_shapes=[
            pltpu.VMEM((tk, tq), F32),
            pltpu.VMEM((tk, tq), F32),
            pltpu.VMEM((VT_ROWS, tq), F32),
            pltpu.VMEM((QK_PAD, tq), BF16),
            pltpu.VMEM((QK_PAD, tq), BF16),
            pltpu.VMEM((1, tq), F32),
        ],
        compiler_params=_params("arbitrary", "arbitrary", "arbitrary"),
        name="attention",
    )(q, q, k, vt, k_meta, vt_meta)


def _gla_scan_kernel(q_ref, k_ref, v_ref, lg_ref, s0_ref, o_ref, sfin_ref, state, u_scr, dec_scr, qe_scr,
                     *, reverse, n_chunks):
    blk = pl.program_id(1)

    @pl.when(blk == 0)
    def _():
        state[...] = s0_ref[...]

    ri = lax.broadcasted_iota(jnp.int32, (GLA_CHUNK, GLA_CHUNK), 0)
    ci = lax.broadcasted_iota(jnp.int32, (GLA_CHUNK, GLA_CHUNK), 1)
    keep = (ri <= ci) if reverse else (ri >= ci)
    csum = jnp.where(keep, 1.0, 0.0).astype(BF16)
    csum2 = jnp.concatenate([csum, csum], axis=1)
    mid = GLA_CHUNK // 2 if reverse else GLA_CHUNK // 2 - 1
    end = 0 if reverse else GLA_CHUNK - 1
    group = math.gcd(n_chunks, 4)

    heads = [(slice(hd * GLA_DK, (hd + 1) * GLA_DK), slice(hd * GLA_DV, (hd + 1) * GLA_DV)) for hd in range(GLA_H)]

    def local(g, carry):
        cs = [g * group + j for j in range(group)]
        rows = [pl.ds(pl.multiple_of(c * GLA_CHUNK, GLA_CHUNK), GLA_CHUNK) for c in cs]
        bcums = []
        for r in rows:
            lg = lg_ref[0, r, :]
            lg_hi = lg.astype(BF16)
            lg_lo = (lg - lg_hi.astype(F32)).astype(BF16)
            bcums.append(_dot(csum2, jnp.concatenate([lg_hi, lg_lo], axis=0)))
        qss, kss, kds = [], [], []
        for c, r, bcum in zip(cs, rows, bcums):
            b_mid = bcum[mid:mid + 1, :]
            b_end = bcum[end:end + 1, :]
            q = q_ref[0, r, :]
            k = k_ref[0, r, :]
            qss.append((q * jnp.exp(bcum - b_mid)).astype(BF16))
            kss.append((k * jnp.exp(b_mid - bcum)).astype(BF16))
            kds.append((k * jnp.exp(b_end - bcum)).astype(BF16))
            qe_scr[c] = (q * jnp.exp(bcum)).astype(BF16)
            dec_scr[c] = jnp.broadcast_to(jnp.exp(b_end), (8, GLA_H * GLA_DK))
        attn = [[_dot_nt(qs[:, kc], ks[:, kc]) for kc, _ in heads] for qs, ks in zip(qss, kss)]
        for c, r, kd in zip(cs, rows, kds):
            for hd, (kc, vc) in enumerate(heads):
                u_scr[c, hd] = _dot_tn(v_ref[0, r, vc], kd[:, kc])
        for r, a_c in zip(rows, attn):
            for (_, vc), a in zip(heads, a_c):
                o_ref[0, r, vc] = _dot(jnp.where(keep, a, 0.0).astype(BF16), v_ref[0, r, vc])
        return carry

    lax.fori_loop(0, n_chunks // group, local, 0, unroll=True)

    def recur(i, carry):
        c = (n_chunks - 1 - i) if reverse else i
        rows = pl.ds(pl.multiple_of(c * GLA_CHUNK, GLA_CHUNK), GLA_CHUNK)
        for hd, (kc, vc) in enumerate(heads):
            st = state[hd]
            o_ref[0, rows, vc] += _dot_nt(qe_scr[c, :, kc], st.astype(BF16))
            state[hd] = st * dec_scr[c, 0:1, kc] + u_scr[c, hd]
        return carry

    lax.fori_loop(0, n_chunks, recur, 0, unroll=True)

    @pl.when(blk == pl.num_programs(1) - 1)
    def _():
        sfin_ref[0] = state[...]


def _gla_scan(q, k, v, lg, s0, reverse, t_blk):
    b, t, _ = q.shape
    nb = t // t_blk
    lg_col = 1 if reverse else 0
    blk = (lambda bi, i: (bi, nb - 1 - i, 0)) if reverse else (lambda bi, i: (bi, i, 0))
    lg_blk = (lambda bi, i: (bi, nb - 1 - i, lg_col)) if reverse else (lambda bi, i: (bi, i, lg_col))
    st_shape = (GLA_H, GLA_DV, GLA_DK)
    n_chunks = t_blk // GLA_CHUNK
    return pl.pallas_call(
        functools.partial(_gla_scan_kernel, reverse=reverse, n_chunks=n_chunks),
        grid=(b, nb),
        in_specs=[
            pl.BlockSpec((1, t_blk, GLA_H * GLA_DK), blk),
            pl.BlockSpec((1, t_blk, GLA_H * GLA_DK), blk),
            pl.BlockSpec((1, t_blk, GLA_H * GLA_DV), blk),
            pl.BlockSpec((1, t_blk, GLA_H * GLA_DK), lg_blk),
            pl.BlockSpec(st_shape, lambda bi, i: (0, 0, 0)),
        ],
        out_specs=[
            pl.BlockSpec((1, t_blk, GLA_H * GLA_DV), blk),
            pl.BlockSpec((1,) + st_shape, lambda bi, i: (bi, 0, 0, 0)),
        ],
        out_shape=[
            jax.ShapeDtypeStruct((b, t, GLA_H * GLA_DV), F32),
            jax.ShapeDtypeStruct((b,) + st_shape, F32),
        ],
        scratch_shapes=[
            pltpu.VMEM(st_shape, F32),
            pltpu.VMEM((n_chunks,) + st_shape, F32),
            pltpu.VMEM((n_chunks, 8, GLA_H * GLA_DK), F32),
            pltpu.VMEM((n_chunks, GLA_CHUNK, GLA_H * GLA_DK), BF16),
        ],
        compiler_params=_params("parallel", "arbitrary"),
        name="gla_scan_bwd" if reverse else "gla_scan_fwd",
    )(q, k, v, lg, s0)


def _mix_ffn_kernel(x_ref, oa_ref, of_ref, ob_ref, og_ref, gate_ref, womla_ref, onorm_ref, wogla_ref,
                    wout_ref, fn_ref, wgate_ref, wup_ref, wdown_ref, y_ref):
    y_a = _dot(oa_ref[...], womla_ref[...])
    o = of_ref[...] + ob_ref[...]
    og = og_ref[...]
    parts = []
    for hd in range(GLA_H):
        cols = slice(hd * GLA_DV, (hd + 1) * GLA_DV)
        parts.append((_rms(o[:, cols]) * onorm_ref[...] * og[:, cols]).astype(BF16))
    y_b = _dot(jnp.concatenate(parts, axis=-1), wogla_ref[...])
    gates = gate_ref[...]
    mixed = (gates[:, :D_MODEL] * y_a + gates[:, D_MODEL:] * y_b).astype(BF16)
    x1 = x_ref[...] + _dot(mixed, wout_ref[...])
    h = (_rms(x1) * fn_ref[...]).astype(BF16)
    g = _dot(h, wgate_ref[...])
    u = _dot(h, wup_ref[...])
    act = (g * jax.nn.sigmoid(g) * u).astype(BF16)
    y_ref[...] = x1 + _dot(act, wdown_ref[...])


def _mix_ffn(x2d, o_attn, o_f, o_b, og, gates, w, tm):
    n = x2d.shape[0]
    row = lambda i: (i, 0)
    return pl.pallas_call(
        _mix_ffn_kernel,
        grid=(n // tm,),
        in_specs=[
            pl.BlockSpec((tm, D_MODEL), row),
            pl.BlockSpec((tm, MLA_H * V_DIM), row),
            pl.BlockSpec((tm, GLA_H * GLA_DV), row),
            pl.BlockSpec((tm, GLA_H * GLA_DV), row),
            pl.BlockSpec((tm, GLA_H * GLA_DV), row),
            pl.BlockSpec((tm, 2 * D_MODEL), row),
            _resident(w["w_o_mla"].shape),
            _resident((1, GLA_DV)),
            _resident(w["w_o_gla"].shape),
            _resident(w["w_out"].shape),
            _resident((1, D_MODEL)),
            _resident(w["w_ffn_gate"].shape),
            _resident(w["w_ffn_up"].shape),
            _resident(w["w_ffn_down"].shape),
        ],
        out_specs=pl.BlockSpec((tm, D_MODEL), row),
        out_shape=jax.ShapeDtypeStruct((n, D_MODEL), F32),
        compiler_params=_params("parallel"),
        name="mix_ffn",
    )(x2d, o_attn, o_f, o_b, og, gates, w["w_o_mla"], w["gla_o_norm"], w["w_o_gla"], w["w_out"],
      w["ffn_norm"], w["w_ffn_gate"], w["w_ffn_up"], w["w_ffn_down"])


def _prep_weights(attn_norm, w_in, q_a_norm, w_uq, kv_a_norm, w_ukv, q_norm, k_norm, w_o_mla,
                  w_a2_fwd, b_a2_fwd, w_a2_bwd, b_a2_bwd, gla_o_norm, w_o_gla, w_out, ffn_norm,
                  w_ffn_gate, w_ffn_up, w_ffn_down):
    half = ROPE_DIM // 2
    w_in, w_uq, w_ukv = w_in.astype(BF16), w_uq.astype(BF16), w_ukv.astype(BF16)
    o = 0
    cols = {}
    for name, size in (("cq", Q_LORA), ("ckv", KV_LORA), ("kr", ROPE_DIM), ("gq", GLA_H * GLA_DK),
                       ("gk", GLA_H * GLA_DK), ("gv", GLA_H * GLA_DV), ("gg", GLA_H * GLA_DV),
                       ("af", GATE_RANK), ("ab", GATE_RANK), ("ga", D_MODEL), ("gb", D_MODEL)):
        cols[name] = w_in[:, o:o + size]
        o += size
    kr1, kr2 = cols["kr"][:, :half], cols["kr"][:, half:]
    w_ckvr = jnp.concatenate([cols["ckv"], kr1, kr2, kr1, kr2, kr2, kr1, kr2, kr1], axis=1)

    uq = w_uq.reshape(Q_LORA, MLA_H, QK_DIM)
    uq_nope = uq[:, :, :NOPE_DIM].reshape(Q_LORA, MLA_H * NOPE_DIM)
    uq_r = uq[:, :, NOPE_DIM:]
    uq_a = uq_r.reshape(Q_LORA, MLA_H * ROPE_DIM)
    uq_b = jnp.concatenate([uq_r[:, :, half:], uq_r[:, :, :half]], axis=-1).reshape(Q_LORA, MLA_H * ROPE_DIM)
    ukv = w_ukv.reshape(KV_LORA, MLA_H, NOPE_DIM + V_DIM)
    w_ukv2 = jnp.concatenate([ukv[:, :, :NOPE_DIM].reshape(KV_LORA, MLA_H * NOPE_DIM),
                              ukv[:, :, NOPE_DIM:].reshape(KV_LORA, MLA_H * V_DIM)], axis=1)

    def gain_rows(g):
        z = jnp.zeros((ROPE_DIM,), F32)
        ga = g[NOPE_DIM:]
        gb = jnp.concatenate([g[NOPE_DIM + half:], g[NOPE_DIM:NOPE_DIM + half]])
        rows = [g[:NOPE_DIM], jnp.concatenate([ga, z]), jnp.concatenate([z, ga]),
                jnp.concatenate([gb, z]), jnp.concatenate([z, gb])]
        rows += [jnp.zeros((LANES,), F32)] * 3
        return jnp.stack(rows)

    zpad = jnp.zeros((D_MODEL, LANES - 2 * GATE_RANK), BF16)
    w_g = jnp.concatenate([cols["gq"], cols["gk"], cols["gv"], cols["gg"], cols["af"], cols["ab"], zpad,
                           cols["ga"], cols["gb"]], axis=1)
    nk = GLA_H * GLA_DK
    w_a2 = jnp.zeros((LANES, 2 * nk), F32)
    w_a2 = w_a2.at[:GATE_RANK, :nk].set(w_a2_fwd).at[GATE_RANK:2 * GATE_RANK, nk:].set(w_a2_bwd)
    return {
        "attn_norm": attn_norm[None], "w_cq": cols["cq"], "w_ckvr": w_ckvr,
        "q_a_norm": q_a_norm[None], "w_uq": jnp.concatenate([uq_nope, uq_a, uq_b], axis=1),
        "kv_a_norm": kv_a_norm[None], "w_ukv": w_ukv2,
        "g_q": gain_rows(q_norm), "g_k": gain_rows(k_norm),
        "w_g": w_g, "w_a2": w_a2.astype(BF16),
        "b_a2": jnp.concatenate([b_a2_fwd, b_a2_bwd])[None],
        "w_o_mla": w_o_mla.astype(BF16), "gla_o_norm": gla_o_norm[None], "w_o_gla": w_o_gla.astype(BF16),
        "w_out": w_out.astype(BF16), "ffn_norm": ffn_norm[None], "w_ffn_gate": w_ffn_gate.astype(BF16),
        "w_ffn_up": w_ffn_up.astype(BF16), "w_ffn_down": w_ffn_down.astype(BF16),
    }


def _rope_tables(length):
    inv = 1.0 / (ROPE_THETA ** (np.arange(0, ROPE_DIM, 2, dtype=np.float64) / ROPE_DIM))
    ang = np.arange(length, dtype=np.float64)[:, None] * inv[None, :]
    cos, sin = np.cos(ang), np.sin(ang)
    return (jnp.asarray(np.tile(cos, (1, 4)), F32),
            jnp.asarray(np.concatenate([-sin, sin, -sin, sin], axis=1), F32))


def _encode_group(x, w, meta, cos_t, sin_t):
    b, seq, _ = x.shape
    n = b * seq
    x2d = x.reshape(n, D_MODEL)
    tm = _row_tile(seq, 256)
    q, k, vt, gq, gk, gv, og, lg, gates = _project(
        x2d, w, cos_t[N_META:N_META + seq], sin_t[N_META:N_META + seq], seq, tm, True)

    o_attn = _attention(q.reshape(b, seq, -1), k.reshape(b, seq, -1), vt,
                        meta["k"], meta["vt"], _row_tile(seq // 2, 1024), _row_tile(seq // 4, 2048))

    t_blk = _row_tile(seq, 1024)
    g3 = lambda a: a.reshape(b, seq, -1)
    o_f, _ = _gla_scan(g3(gq), g3(gk), g3(gv), g3(lg), meta["state"], False, t_blk)
    o_b, _ = _gla_scan(g3(gq), g3(gk), g3(gv), g3(lg), jnp.zeros_like(meta["state"]), True, t_blk)

    y = _mix_ffn(x2d, o_attn.reshape(n, -1), o_f.reshape(n, -1), o_b.reshape(n, -1), og, gates, w, tm)
    return y.reshape(b, seq, D_MODEL)


def kernel(x_prompt, x_sample, meta_tokens, attn_norm, w_in, q_a_norm, w_uq, kv_a_norm, w_ukv, q_norm, k_norm, w_o_mla, w_a2_fwd, b_a2_fwd, w_a2_bwd, b_a2_bwd, gla_o_norm, w_o_gla, w_out, ffn_norm, w_ffn_gate, w_ffn_up, w_ffn_down):
    assert attn_norm.shape[0] == 1, "single-layer encoder"
    w = _prep_weights(attn_norm[0], w_in[0], q_a_norm[0], w_uq[0], kv_a_norm[0], w_ukv[0], q_norm[0],
                      k_norm[0], w_o_mla[0], w_a2_fwd[0], b_a2_fwd[0], w_a2_bwd[0], b_a2_bwd[0],
                      gla_o_norm[0], w_o_gla[0], w_out[0], ffn_norm[0], w_ffn_gate[0], w_ffn_up[0],
                      w_ffn_down[0])
    max_len = N_META + max(x_prompt.shape[1], x_sample.shape[1])
    cos_t, sin_t = _rope_tables(max_len)

    xm = meta_tokens.astype(F32)
    _, k_m, v_m, mq, mk, mv, _, mlg, _ = _project(xm, w, cos_t[:N_META], sin_t[:N_META], N_META, N_META, False)
    pad = lambda a: jnp.pad(a, ((GLA_CHUNK - N_META, 0), (0, 0)))[None]
    _, s_meta = _gla_scan(pad(mq), pad(mk), pad(mv), pad(mlg),
                          jnp.zeros((GLA_H, GLA_DV, GLA_DK), F32), False, GLA_CHUNK)
    vt_m = jnp.concatenate([v_m.T.reshape(MLA_H, V_DIM, N_META), jnp.ones((MLA_H, ONES_ROWS, N_META), BF16)],
                           axis=1).reshape(MLA_H * VT_ROWS, N_META)
    meta = {"k": k_m, "vt": vt_m, "state": s_meta[0]}

    return (_encode_group(x_prompt, w, meta, cos_t, sin_t),
            _encode_group(x_sample, w, meta, cos_t, sin_t))
```

```python
import functools
import math

import jax
import jax.numpy as jnp
import numpy as np
from jax import lax
from jax.experimental import pallas as pl
from jax.experimental.pallas import tpu as pltpu

D_MODEL = 1024
N_META = 16
MLA_H = 8
NOPE_DIM = 128
ROPE_DIM = 64
QK_DIM = NOPE_DIM + ROPE_DIM
V_DIM = 128
Q_LORA = 768
KV_LORA = 256
ROPE_THETA = 10000.0
GLA_H = 4
GLA_DK = 128
GLA_DV = 256
GATE_RANK = 16
GATE_TEMP = 16.0
GLA_CHUNK = 64
NORM_EPS = 1e-6

QK_PAD = 256
ONES_ROWS = 16
VT_ROWS = V_DIM + ONES_ROWS
LANES = 128
MXU_DIM = 256
LOG2E = 1.4426950408889634
VMEM_LIMIT = 56 * 1024 * 1024

F32 = jnp.float32
BF16 = jnp.bfloat16


def _dot(a, b):
    return jnp.dot(a, b, preferred_element_type=F32)


def _dot_nt(a, b):
    return lax.dot_general(a, b, (((1,), (1,)), ((), ())), preferred_element_type=F32)


def _dot_tn(a, b):
    return lax.dot_general(a, b, (((0,), (0,)), ((), ())), preferred_element_type=F32)


def _rms(x):
    return x * lax.rsqrt(jnp.mean(x * x, axis=-1, keepdims=True) + NORM_EPS)


def _resident(shape):
    return pl.BlockSpec(shape, lambda *_: (0,) * len(shape), pipeline_mode=pl.Buffered(1))


def _params(*sem):
    return pltpu.CompilerParams(dimension_semantics=sem, vmem_limit_bytes=VMEM_LIMIT)


def _row_tile(n, want):
    t = min(n, want)
    assert n % t == 0, (n, t)
    return t


G_Q0 = 0
G_K0 = G_Q0 + GLA_H * GLA_DK
G_V0 = G_K0 + GLA_H * GLA_DK
G_G0 = G_V0 + GLA_H * GLA_DV
G_A0 = G_G0 + GLA_H * GLA_DV
G_S0 = G_A0 + LANES
G_END = G_S0 + 2 * D_MODEL


def _proj_kernel(x_ref, an_ref, wcq_ref, wckv_ref, qan_ref, wuq_ref, kvan_ref, wukv_ref, gq_ref, gk_ref,
                 cos_ref, sin_ref, wg_ref, wa2_ref, ba2_ref,
                 q_out, k_out, v_out, gla_q_out, gla_k_out, gla_v_out, og_out, lg_out, gate_out, *, transpose_v):
    h = (_rms(x_ref[...]) * an_ref[...]).astype(BF16)
    cq = _dot(h, wcq_ref[...])
    ckvr = _dot(h, wckv_ref[...])
    gla_q_out[...] = _dot(h, wg_ref[:, G_Q0:G_K0]) * (GLA_DK ** -0.5)
    gla_k_out[...] = _dot(h, wg_ref[:, G_K0:G_V0])
    cqn = (_rms(cq) * qan_ref[...]).astype(BF16)
    ckvn = (_rms(ckvr[:, :KV_LORA]) * kvan_ref[...]).astype(BF16)
    qall = _dot(cqn, wuq_ref[...])
    kv = _dot(ckvn, wukv_ref[...])
    gla_v_out[...] = _dot(h, wg_ref[:, G_V0:G_G0]).astype(BF16)
    g = _dot(h, wg_ref[:, G_G0:G_A0])
    og_out[...] = (g * jax.nn.sigmoid(g)).astype(og_out.dtype)
    a = _dot(h, wg_ref[:, G_A0:G_S0]).astype(BF16)
    pre = _dot(a, wa2_ref[...]) + ba2_ref[...]
    lg_out[...] = jax.nn.log_sigmoid(pre) * (1.0 / GATE_TEMP)
    gate_out[...] = jax.nn.sigmoid(_dot(h, wg_ref[:, G_S0:G_END])).astype(gate_out.dtype)

    v = kv[:, MLA_H * NOPE_DIM:]
    if transpose_v:
        vt = v.T.astype(BF16)
        ones = jnp.ones((ONES_ROWS, v.shape[0]), BF16)
        for hd in range(MLA_H):
            v_out[hd * VT_ROWS:hd * VT_ROWS + V_DIM, :] = vt[hd * V_DIM:(hd + 1) * V_DIM, :]
            v_out[hd * VT_ROWS + V_DIM:(hd + 1) * VT_ROWS, :] = ones
    else:
        v_out[...] = v.astype(BF16)

    cos = cos_ref[...]
    sin = sin_ref[...]
    lane = lax.broadcasted_iota(jnp.int32, (1, LANES), 1)
    ka = ckvr[:, KV_LORA:KV_LORA + LANES]
    kb = ckvr[:, KV_LORA + LANES:KV_LORA + 2 * LANES]
    head_lanes = [((lane >= p * ROPE_DIM) & (lane < (p + 1) * ROPE_DIM)).astype(F32) for p in range(2)]
    ssq_kr = jnp.sum(ka * ka * head_lanes[0], axis=-1, keepdims=True)
    k_rot = [(ka * gk_ref[1 + p:2 + p, :]) * cos + (kb * gk_ref[3 + p:4 + p, :]) * sin for p in range(2)]
    q_scale = QK_DIM ** -0.5 * LOG2E
    rope0 = MLA_H * NOPE_DIM
    for hd in range(MLA_H):
        j, p = divmod(hd, 2)
        qn = qall[:, hd * NOPE_DIM:(hd + 1) * NOPE_DIM]
        qa = qall[:, rope0 + j * LANES:rope0 + (j + 1) * LANES]
        qb = qall[:, rope0 + 4 * LANES + j * LANES:rope0 + 4 * LANES + (j + 1) * LANES]
        ssq = (jnp.sum(qn * qn, axis=-1, keepdims=True)
               + jnp.sum(qa * qa * head_lanes[p], axis=-1, keepdims=True))
        rs = lax.rsqrt(ssq * (1.0 / QK_DIM) + NORM_EPS) * q_scale
        q_rot = (qa * gq_ref[1 + p:2 + p, :]) * cos + (qb * gq_ref[3 + p:4 + p, :]) * sin
        q_out[:, hd * QK_PAD:hd * QK_PAD + LANES] = (qn * gq_ref[0:1, :] * rs).astype(BF16)
        q_out[:, hd * QK_PAD + LANES:(hd + 1) * QK_PAD] = (q_rot * rs).astype(BF16)
        kn = kv[:, hd * NOPE_DIM:(hd + 1) * NOPE_DIM]
        rsk = lax.rsqrt((jnp.sum(kn * kn, axis=-1, keepdims=True) + ssq_kr) * (1.0 / QK_DIM) + NORM_EPS)
        k_out[:, hd * QK_PAD:hd * QK_PAD + LANES] = (kn * gk_ref[0:1, :] * rsk).astype(BF16)
        k_out[:, hd * QK_PAD + LANES:(hd + 1) * QK_PAD] = (k_rot[p] * rsk).astype(BF16)


def _project(x2d, w, cos_t, sin_t, seq_len, tm, transpose_v):
    n = x2d.shape[0]
    nt = seq_len // tm
    row = lambda i: (i, 0)
    pos = lambda i: (i % nt, 0)
    v_dims = MLA_H * V_DIM
    vt_rows = MLA_H * VT_ROWS
    nk, nv = GLA_H * GLA_DK, GLA_H * GLA_DV
    gla_outs = ((nk, F32), (nk, F32), (nv, BF16), (nv, BF16), (2 * nk, F32), (2 * D_MODEL, BF16))
    return pl.pallas_call(
        functools.partial(_proj_kernel, transpose_v=transpose_v),
        grid=(n // tm,),
        in_specs=[
            pl.BlockSpec((tm, D_MODEL), row),
            _resident((1, D_MODEL)),
            _resident(w["w_cq"].shape),
            _resident(w["w_ckvr"].shape),
            _resident((1, Q_LORA)),
            _resident(w["w_uq"].shape),
            _resident((1, KV_LORA)),
            _resident(w["w_ukv"].shape),
            _resident((8, LANES)),
            _resident((8, LANES)),
            pl.BlockSpec((tm, LANES), pos),
            pl.BlockSpec((tm, LANES), pos),
            _resident(w["w_g"].shape),
            _resident(w["w_a2"].shape),
            _resident((1, 2 * nk)),
        ],
        out_specs=[
            pl.BlockSpec((tm, MLA_H * QK_PAD), row),
            pl.BlockSpec((tm, MLA_H * QK_PAD), row),
            pl.BlockSpec((vt_rows, tm), lambda i: (0, i)) if transpose_v else pl.BlockSpec((tm, v_dims), row),
        ] + [pl.BlockSpec((tm, width), row) for width, _ in gla_outs],
        out_shape=[
            jax.ShapeDtypeStruct((n, MLA_H * QK_PAD), BF16),
            jax.ShapeDtypeStruct((n, MLA_H * QK_PAD), BF16),
            jax.ShapeDtypeStruct((vt_rows, n) if transpose_v else (n, v_dims), BF16),
        ] + [jax.ShapeDtypeStruct((n, width), dtype) for width, dtype in gla_outs],
        compiler_params=_params("parallel"),
        name="project",
    )(x2d, w["attn_norm"], w["w_cq"], w["w_ckvr"], w["q_a_norm"], w["w_uq"], w["kv_a_norm"],
      w["w_ukv"], w["g_q"], w["g_k"], cos_t, sin_t, w["w_g"], w["w_a2"], w["b_a2"])


def _attn_kernel(q_ref, qn_ref, k_ref, vt_ref, km_ref, vmt_ref, o_ref,
                 s_even, s_odd, acc_scr, qt_scr, qtn_scr, max_scr, *, tk, ck, n_tiles):
    qi = pl.program_id(2)
    n_q = pl.num_programs(2)
    s_scr = (s_even, s_odd)
    n_chunks = tk // ck

    def transposed(ref):
        return ref[0].astype(F32).T.astype(BF16)

    def score_chunk(i, c, slot, running_max, qt_ref):
        off = pl.multiple_of(i * tk + c * ck, ck)
        s = _dot(k_ref[0, pl.ds(off, ck), :], qt_ref[...])
        s_scr[slot][c * ck:(c + 1) * ck, :] = s
        s_max = jnp.max(s, axis=0, keepdims=True)
        return s_max if running_max is None else jnp.maximum(running_max, s_max)

    def gather_chunk(i, c, slot, m, partial):
        off = pl.multiple_of(i * tk + c * ck, ck)
        p = jnp.exp2(s_scr[slot][c * ck:(c + 1) * ck, :] - m).astype(BF16)
        d = _dot(vt_ref[:, pl.ds(off, ck)], p)
        return d if partial is None else partial + d

    def step(i, slot, s_max, m, following="tile"):
        m_new = jnp.maximum(m, s_max)
        next_max, partial = None, None
        for c in range(n_chunks):
            if following == "tile":
                next_max = score_chunk(i + 1, c, 1 - slot, next_max, qt_scr)
            elif following == "block":
                next_max = score_chunk(0, c, 1 - slot, next_max, qtn_scr)
            partial = gather_chunk(i, c, slot, m_new, partial)
        acc_scr[...] = jnp.exp2(m - m_new) * acc_scr[...] + partial
        return next_max, m_new

    def two_steps(j, carry):
        carry = step(2 * j, 0, *carry)
        return step(2 * j + 1, 1, *carry)

    @pl.when(qi == 0)
    def _():
        qt_scr[...] = transposed(q_ref)
        max0 = None
        for c in range(n_chunks):
            max0 = score_chunk(0, c, 0, max0, qt_scr)
        max_scr[...] = max0

    s_meta = _dot(km_ref[...], qt_scr[...])
    m_meta = jnp.max(s_meta, axis=0, keepdims=True)
    acc_scr[...] = _dot(vmt_ref[...], jnp.exp2(s_meta - m_meta).astype(BF16))
    carry = lax.fori_loop(0, n_tiles // 2 - 1, two_steps, (max_scr[...], m_meta), unroll=True)
    max_last, m = step(n_tiles - 2, 0, *carry)

    @pl.when(qi < n_q - 1)
    def _():
        qtn_scr[...] = transposed(qn_ref)
        max_scr[...] = step(n_tiles - 1, 1, max_last, m, following="block")[0]
        qt_scr[...] = qtn_scr[...]

    @pl.when(qi == n_q - 1)
    def _():
        step(n_tiles - 1, 1, max_last, m, following=None)

    acc = acc_scr[...]
    o_ref[0] = (acc[:V_DIM] / acc[V_DIM:V_DIM + 1]).T.astype(o_ref.dtype)


def _attention(q, k, vt, k_meta, vt_meta, tq, tk):
    b, seq, _ = q.shape
    assert seq % (2 * tk) == 0, (seq, tk)
    n_q = seq // tq
    return pl.pallas_call(
        functools.partial(_attn_kernel, tk=tk, ck=min(tk, MXU_DIM), n_tiles=seq // tk),
        grid=(b, MLA_H, n_q),
        in_specs=[
            pl.BlockSpec((1, tq, QK_PAD), lambda bi, hi, qi: (bi, qi, hi)),
            pl.BlockSpec((1, tq, QK_PAD), lambda bi, hi, qi: (bi, jnp.minimum(qi + 1, n_q - 1), hi)),
            pl.BlockSpec((1, seq, QK_PAD), lambda bi, hi, qi: (bi, 0, hi)),
            pl.BlockSpec((VT_ROWS, seq), lambda bi, hi, qi: (hi, bi)),
            pl.BlockSpec((N_META, QK_PAD), lambda bi, hi, qi: (0, hi)),
            pl.BlockSpec((VT_ROWS, N_META), lambda bi, hi, qi: (hi, 0)),
        ],
        out_specs=pl.BlockSpec((1, tq, V_DIM), lambda bi, hi, qi: (bi, qi, hi)),
        out_shape=jax.ShapeDtypeStruct((b, seq, MLA_H * V_DIM), BF16),
        scratch_shapes=[
            pltpu.VMEM((tk, tq), F32),
            pltpu.VMEM((tk, tq), F32),
            pltpu.VMEM((VT_ROWS, tq), F32),
            pltpu.VMEM((QK_PAD, tq), BF16),
            pltpu.VMEM((QK_PAD, tq), BF16),
            pltpu.VMEM((1, tq), F32),
        ],
        compiler_params=_params("arbitrary", "arbitrary", "arbitrary"),
        name="attention",
    )(q, q, k, vt, k_meta, vt_meta)


def _gla_scan_kernel(q_ref, k_ref, v_ref, lg_ref, s0_ref, o_ref, sfin_ref, state, u_scr, dec_scr, qe_scr,
                     *, reverse, n_chunks):
    blk = pl.program_id(1)

    @pl.when(blk == 0)
    def _():
        state[...] = s0_ref[...]

    ri = lax.broadcasted_iota(jnp.int32, (GLA_CHUNK, GLA_CHUNK), 0)
    ci = lax.broadcasted_iota(jnp.int32, (GLA_CHUNK, GLA_CHUNK), 1)
    keep = (ri <= ci) if reverse else (ri >= ci)
    csum = jnp.where(keep, 1.0, 0.0).astype(BF16)
    csum2 = jnp.concatenate([csum, csum], axis=1)
    mid = GLA_CHUNK // 2 if reverse else GLA_CHUNK // 2 - 1
    end = 0 if reverse else GLA_CHUNK - 1
    group = math.gcd(n_chunks, 8)

    heads = [(slice(hd * GLA_DK, (hd + 1) * GLA_DK), slice(hd * GLA_DV, (hd + 1) * GLA_DV)) for hd in range(GLA_H)]

    def local(g, carry):
        cs = [g * group + j for j in range(group)]
        rows = [pl.ds(pl.multiple_of(c * GLA_CHUNK, GLA_CHUNK), GLA_CHUNK) for c in cs]
        bcums = []
        for r in rows:
            lg = lg_ref[0, r, :]
            lg_hi = lg.astype(BF16)
            lg_lo = (lg - lg_hi.astype(F32)).astype(BF16)
            bcums.append(_dot(csum2, jnp.concatenate([lg_hi, lg_lo], axis=0)))
        qss, kss, kds = [], [], []
        for c, r, bcum in zip(cs, rows, bcums):
            b_mid = bcum[mid:mid + 1, :]
            b_end = bcum[end:end + 1, :]
            q = q_ref[0, r, :]
            k = k_ref[0, r, :]
            qss.append((q * jnp.exp(bcum - b_mid)).astype(BF16))
            kss.append((k * jnp.exp(b_mid - bcum)).astype(BF16))
            kds.append((k * jnp.exp(b_end - bcum)).astype(BF16))
            qe_scr[c] = (q * jnp.exp(bcum)).astype(BF16)
            dec_scr[c] = jnp.broadcast_to(jnp.exp(b_end), (8, GLA_H * GLA_DK))
        attn = [[_dot_nt(qs[:, kc], ks[:, kc]) for kc, _ in heads] for qs, ks in zip(qss, kss)]
        for c, r, kd in zip(cs, rows, kds):
            for hd, (kc, vc) in enumerate(heads):
                u_scr[c, hd] = _dot_tn(v_ref[0, r, vc], kd[:, kc])
        for r, a_c in zip(rows, attn):
            for (_, vc), a in zip(heads, a_c):
                o_ref[0, r, vc] = _dot(jnp.where(keep, a, 0.0).astype(BF16), v_ref[0, r, vc])
        return carry

    lax.fori_loop(0, n_chunks // group, local, 0, unroll=True)

    def recur(i, carry):
        c = (n_chunks - 1 - i) if reverse else i
        rows = pl.ds(pl.multiple_of(c * GLA_CHUNK, GLA_CHUNK), GLA_CHUNK)
        for hd, (kc, vc) in enumerate(heads):
            st = state[hd]
            o_ref[0, rows, vc] += _dot_nt(qe_scr[c, :, kc], st.astype(BF16))
            state[hd] = st * dec_scr[c, 0:1, kc] + u_scr[c, hd]
        return carry

    lax.fori_loop(0, n_chunks, recur, 0, unroll=True)

    @pl.when(blk == pl.num_programs(1) - 1)
    def _():
        sfin_ref[0] = state[...]


def _gla_scan(q, k, v, lg, s0, reverse, t_blk):
    b, t, _ = q.shape
    nb = t // t_blk
    lg_col = 1 if reverse else 0
    blk = (lambda bi, i: (bi, nb - 1 - i, 0)) if reverse else (lambda bi, i: (bi, i, 0))
    lg_blk = (lambda bi, i: (bi, nb - 1 - i, lg_col)) if reverse else (lambda bi, i: (bi, i, lg_col))
    st_shape = (GLA_H, GLA_DV, GLA_DK)
    n_chunks = t_blk // GLA_CHUNK
    return pl.pallas_call(
        functools.partial(_gla_scan_kernel, reverse=reverse, n_chunks=n_chunks),
        grid=(b, nb),
        in_specs=[
            pl.BlockSpec((1, t_blk, GLA_H * GLA_DK), blk),
            pl.BlockSpec((1, t_blk, GLA_H * GLA_DK), blk),
            pl.BlockSpec((1, t_blk, GLA_H * GLA_DV), blk),
            pl.BlockSpec((1, t_blk, GLA_H * GLA_DK), lg_blk),
            pl.BlockSpec(st_shape, lambda bi, i: (0, 0, 0)),
        ],
        out_specs=[
            pl.BlockSpec((1, t_blk, GLA_H * GLA_DV), blk),
            pl.BlockSpec((1,) + st_shape, lambda bi, i: (bi, 0, 0, 0)),
        ],
        out_shape=[
            jax.ShapeDtypeStruct((b, t, GLA_H * GLA_DV), F32),
            jax.ShapeDtypeStruct((b,) + st_shape, F32),
        ],
        scratch_shapes=[
            pltpu.VMEM(st_shape, F32),
            pltpu.VMEM((n_chunks,) + st_shape, F32),
            pltpu.VMEM((n_chunks, 8, GLA_H * GLA_DK), F32),
            pltpu.VMEM((n_chunks, GLA_CHUNK, GLA_H * GLA_DK), BF16),
        ],
        compiler_params=_params("parallel", "arbitrary"),
        name="gla_scan_bwd" if reverse else "gla_scan_fwd",
    )(q, k, v, lg, s0)


def _mix_ffn_kernel(x_ref, oa_ref, of_ref, ob_ref, og_ref, gate_ref, womla_ref, onorm_ref, wogla_ref,
                    wout_ref, fn_ref, wgate_ref, wup_ref, wdown_ref, y_ref):
    y_a = _dot(oa_ref[...], womla_ref[...])
    o = of_ref[...] + ob_ref[...]
    og = og_ref[...]
    parts = []
    for hd in range(GLA_H):
        cols = slice(hd * GLA_DV, (hd + 1) * GLA_DV)
        parts.append((_rms(o[:, cols]) * onorm_ref[...] * og[:, cols]).astype(BF16))
    y_b = _dot(jnp.concatenate(parts, axis=-1), wogla_ref[...])
    gates = gate_ref[...]
    mixed = (gates[:, :D_MODEL] * y_a + gates[:, D_MODEL:] * y_b).astype(BF16)
    x1 = x_ref[...] + _dot(mixed, wout_ref[...])
    h = (_rms(x1) * fn_ref[...]).astype(BF16)
    g = _dot(h, wgate_ref[...])
    u = _dot(h, wup_ref[...])
    act = (g * jax.nn.sigmoid(g) * u).astype(BF16)
    y_ref[...] = x1 + _dot(act, wdown_ref[...])


def _mix_ffn(x2d, o_attn, o_f, o_b, og, gates, w, tm):
    n = x2d.shape[0]
    row = lambda i: (i, 0)
    return pl.pallas_call(
        _mix_ffn_kernel,
        grid=(n // tm,),
        in_specs=[
            pl.BlockSpec((tm, D_MODEL), row),
            pl.BlockSpec((tm, MLA_H * V_DIM), row),
            pl.BlockSpec((tm, GLA_H * GLA_DV), row),
            pl.BlockSpec((tm, GLA_H * GLA_DV), row),
            pl.BlockSpec((tm, GLA_H * GLA_DV), row),
            pl.BlockSpec((tm, 2 * D_MODEL), row),
            _resident(w["w_o_mla"].shape),
            _resident((1, GLA_DV)),
            _resident(w["w_o_gla"].shape),
            _resident(w["w_out"].shape),
            _resident((1, D_MODEL)),
            _resident(w["w_ffn_gate"].shape),
            _resident(w["w_ffn_up"].shape),
            _resident(w["w_ffn_down"].shape),
        ],
        out_specs=pl.BlockSpec((tm, D_MODEL), row),
        out_shape=jax.ShapeDtypeStruct((n, D_MODEL), F32),
        compiler_params=_params("parallel"),
        name="mix_ffn",
    )(x2d, o_attn, o_f, o_b, og, gates, w["w_o_mla"], w["gla_o_norm"], w["w_o_gla"], w["w_out"],
      w["ffn_norm"], w["w_ffn_gate"], w["w_ffn_up"], w["w_ffn_down"])


def _prep_weights(attn_norm, w_in, q_a_norm, w_uq, kv_a_norm, w_ukv, q_norm, k_norm, w_o_mla,
                  w_a2_fwd, b_a2_fwd, w_a2_bwd, b_a2_bwd, gla_o_norm, w_o_gla, w_out, ffn_norm,
                  w_ffn_gate, w_ffn_up, w_ffn_down):
    half = ROPE_DIM // 2
    w_in, w_uq, w_ukv = w_in.astype(BF16), w_uq.astype(BF16), w_ukv.astype(BF16)
    o = 0
    cols = {}
    for name, size in (("cq", Q_LORA), ("ckv", KV_LORA), ("kr", ROPE_DIM), ("gq", GLA_H * GLA_DK),
                       ("gk", GLA_H * GLA_DK), ("gv", GLA_H * GLA_DV), ("gg", GLA_H * GLA_DV),
                       ("af", GATE_RANK), ("ab", GATE_RANK), ("ga", D_MODEL), ("gb", D_MODEL)):
        cols[name] = w_in[:, o:o + size]
        o += size
    kr1, kr2 = cols["kr"][:, :half], cols["kr"][:, half:]
    w_ckvr = jnp.concatenate([cols["ckv"], kr1, kr2, kr1, kr2, kr2, kr1, kr2, kr1], axis=1)

    uq = w_uq.reshape(Q_LORA, MLA_H, QK_DIM)
    uq_nope = uq[:, :, :NOPE_DIM].reshape(Q_LORA, MLA_H * NOPE_DIM)
    uq_r = uq[:, :, NOPE_DIM:]
    uq_a = uq_r.reshape(Q_LORA, MLA_H * ROPE_DIM)
    uq_b = jnp.concatenate([uq_r[:, :, half:], uq_r[:, :, :half]], axis=-1).reshape(Q_LORA, MLA_H * ROPE_DIM)
    ukv = w_ukv.reshape(KV_LORA, MLA_H, NOPE_DIM + V_DIM)
    w_ukv2 = jnp.concatenate([ukv[:, :, :NOPE_DIM].reshape(KV_LORA, MLA_H * NOPE_DIM),
                              ukv[:, :, NOPE_DIM:].reshape(KV_LORA, MLA_H * V_DIM)], axis=1)

    def gain_rows(g):
        z = jnp.zeros((ROPE_DIM,), F32)
        ga = g[NOPE_DIM:]
        gb = jnp.concatenate([g[NOPE_DIM + half:], g[NOPE_DIM:NOPE_DIM + half]])
        rows = [g[:NOPE_DIM], jnp.concatenate([ga, z]), jnp.concatenate([z, ga]),
                jnp.concatenate([gb, z]), jnp.concatenate([z, gb])]
        rows += [jnp.zeros((LANES,), F32)] * 3
        return jnp.stack(rows)

    zpad = jnp.zeros((D_MODEL, LANES - 2 * GATE_RANK), BF16)
    w_g = jnp.concatenate([cols["gq"], cols["gk"], cols["gv"], cols["gg"], cols["af"], cols["ab"], zpad,
                           cols["ga"], cols["gb"]], axis=1)
    nk = GLA_H * GLA_DK
    w_a2 = jnp.zeros((LANES, 2 * nk), F32)
    w_a2 = w_a2.at[:GATE_RANK, :nk].set(w_a2_fwd).at[GATE_RANK:2 * GATE_RANK, nk:].set(w_a2_bwd)
    return {
        "attn_norm": attn_norm[None], "w_cq": cols["cq"], "w_ckvr": w_ckvr,
        "q_a_norm": q_a_norm[None], "w_uq": jnp.concatenate([uq_nope, uq_a, uq_b], axis=1),
        "kv_a_norm": kv_a_norm[None], "w_ukv": w_ukv2,
        "g_q": gain_rows(q_norm), "g_k": gain_rows(k_norm),
        "w_g": w_g, "w_a2": w_a2.astype(BF16),
        "b_a2": jnp.concatenate([b_a2_fwd, b_a2_bwd])[None],
        "w_o_mla": w_o_mla.astype(BF16), "gla_o_norm": gla_o_norm[None], "w_o_gla": w_o_gla.astype(BF16),
        "w_out": w_out.astype(BF16), "ffn_norm": ffn_norm[None], "w_ffn_gate": w_ffn_gate.astype(BF16),
        "w_ffn_up": w_ffn_up.astype(BF16), "w_ffn_down": w_ffn_down.astype(BF16),
    }


def _rope_tables(length):
    inv = 1.0 / (ROPE_THETA ** (np.arange(0, ROPE_DIM, 2, dtype=np.float64) / ROPE_DIM))
    ang = np.arange(length, dtype=np.float64)[:, None] * inv[None, :]
    cos, sin = np.cos(ang), np.sin(ang)
    return (jnp.asarray(np.tile(cos, (1, 4)), F32),
            jnp.asarray(np.concatenate([-sin, sin, -sin, sin], axis=1), F32))


def _encode_group(x, w, meta, cos_t, sin_t):
    b, seq, _ = x.shape
    n = b * seq
    x2d = x.reshape(n, D_MODEL)
    tm = _row_tile(seq, 256)
    q, k, vt, gq, gk, gv, og, lg, gates = _project(
        x2d, w, cos_t[N_META:N_META + seq], sin_t[N_META:N_META + seq], seq, tm, True)

    o_attn = _attention(q.reshape(b, seq, -1), k.reshape(b, seq, -1), vt,
                        meta["k"], meta["vt"], _row_tile(seq // 2, 1024), _row_tile(seq // 4, 2048))

    t_blk = _row_tile(seq, 1024)
    g3 = lambda a: a.reshape(b, seq, -1)
    o_f, _ = _gla_scan(g3(gq), g3(gk), g3(gv), g3(lg), meta["state"], False, t_blk)
    o_b, _ = _gla_scan(g3(gq), g3(gk), g3(gv), g3(lg), jnp.zeros_like(meta["state"]), True, t_blk)

    y = _mix_ffn(x2d, o_attn.reshape(n, -1), o_f.reshape(n, -1), o_b.reshape(n, -1), og, gates, w, tm)
    return y.reshape(b, seq, D_MODEL)


def kernel(x_prompt, x_sample, meta_tokens, attn_norm, w_in, q_a_norm, w_uq, kv_a_norm, w_ukv, q_norm, k_norm, w_o_mla, w_a2_fwd, b_a2_fwd, w_a2_bwd, b_a2_bwd, gla_o_norm, w_o_gla, w_out, ffn_norm, w_ffn_gate, w_ffn_up, w_ffn_down):
    assert attn_norm.shape[0] == 1, "single-layer encoder"
    w = _prep_weights(attn_norm[0], w_in[0], q_a_norm[0], w_uq[0], kv_a_norm[0], w_ukv[0], q_norm[0],
                      k_norm[0], w_o_mla[0], w_a2_fwd[0], b_a2_fwd[0], w_a2_bwd[0], b_a2_bwd[0],
                      gla_o_norm[0], w_o_gla[0], w_out[0], ffn_norm[0], w_ffn_gate[0], w_ffn_up[0],
                      w_ffn_down[0])
    max_len = N_META + max(x_prompt.shape[1], x_sample.shape[1])
    cos_t, sin_t = _rope_tables(max_len)

    xm = meta_tokens.astype(F32)
    _, k_m, v_m, mq, mk, mv, _, mlg, _ = _project(xm, w, cos_t[:N_META], sin_t[:N_META], N_META, N_META, False)
    pad = lambda a: jnp.pad(a, ((GLA_CHUNK - N_META, 0), (0, 0)))[None]
    _, s_meta = _gla_scan(pad(mq), pad(mk), pad(mv), pad(mlg),
                          jnp.zeros((GLA_H, GLA_DV, GLA_DK), F32), False, GLA_CHUNK)
    vt_m = jnp.concatenate([v_m.T.reshape(MLA_H, V_DIM, N_META), jnp.ones((MLA_H, ONES_ROWS, N_META), BF16)],
                           axis=1).reshape(MLA_H * VT_ROWS, N_META)
    meta = {"k": k_m, "vt": vt_m, "state": s_meta[0]}

    return (_encode_group(x_prompt, w, meta, cos_t, sin_t),
            _encode_group(x_sample, w, meta, cos_t, sin_t))
```

```python
import functools
import math

import jax
import jax.numpy as jnp
import numpy as np
from jax import lax
from jax.experimental import pallas as pl
from jax.experimental.pallas import tpu as pltpu

D_MODEL = 1024
N_META = 16
MLA_H = 8
NOPE_DIM = 128
ROPE_DIM = 64
QK_DIM = NOPE_DIM + ROPE_DIM
V_DIM = 128
Q_LORA = 768
KV_LORA = 256
ROPE_THETA = 10000.0
GLA_H = 4
GLA_DK = 128
GLA_DV = 256
GATE_RANK = 16
GATE_TEMP = 16.0
GLA_CHUNK = 64
NORM_EPS = 1e-6

QK_PAD = 256
ONES_ROWS = 16
VT_ROWS = V_DIM + ONES_ROWS
LANES = 128
MXU_DIM = 256
LOG2E = 1.4426950408889634
VMEM_LIMIT = 56 * 1024 * 1024

F32 = jnp.float32
BF16 = jnp.bfloat16


def _dot(a, b):
    return jnp.dot(a, b, preferred_element_type=F32)


def _dot_nt(a, b):
    return lax.dot_general(a, b, (((1,), (1,)), ((), ())), preferred_element_type=F32)


def _dot_tn(a, b):
    return lax.dot_general(a, b, (((0,), (0,)), ((), ())), preferred_element_type=F32)


def _rms(x):
    return x * lax.rsqrt(jnp.mean(x * x, axis=-1, keepdims=True) + NORM_EPS)


def _resident(shape):
    return pl.BlockSpec(shape, lambda *_: (0,) * len(shape), pipeline_mode=pl.Buffered(1))


def _params(*sem):
    return pltpu.CompilerParams(dimension_semantics=sem, vmem_limit_bytes=VMEM_LIMIT)


def _row_tile(n, want):
    t = min(n, want)
    assert n % t == 0, (n, t)
    return t


G_Q0 = 0
G_K0 = G_Q0 + GLA_H * GLA_DK
G_V0 = G_K0 + GLA_H * GLA_DK
G_G0 = G_V0 + GLA_H * GLA_DV
G_A0 = G_G0 + GLA_H * GLA_DV
G_S0 = G_A0 + LANES
G_END = G_S0 + 2 * D_MODEL


def _proj_kernel(x_ref, an_ref, wcq_ref, wckv_ref, qan_ref, wuq_ref, kvan_ref, wukv_ref, gq_ref, gk_ref,
                 cos_ref, sin_ref, wg_ref, wa2_ref, ba2_ref,
                 q_out, k_out, v_out, gla_q_out, gla_k_out, gla_v_out, og_out, lg_out, gate_out, *, transpose_v):
    h = (_rms(x_ref[...]) * an_ref[...]).astype(BF16)
    cq = _dot(h, wcq_ref[...])
    ckvr = _dot(h, wckv_ref[...])
    gla_q_out[...] = _dot(h, wg_ref[:, G_Q0:G_K0]) * (GLA_DK ** -0.5)
    gla_k_out[...] = _dot(h, wg_ref[:, G_K0:G_V0])
    cqn = (_rms(cq) * qan_ref[...]).astype(BF16)
    ckvn = (_rms(ckvr[:, :KV_LORA]) * kvan_ref[...]).astype(BF16)
    qall = _dot(cqn, wuq_ref[...])
    kv = _dot(ckvn, wukv_ref[...])
    gla_v_out[...] = _dot(h, wg_ref[:, G_V0:G_G0]).astype(BF16)
    g = _dot(h, wg_ref[:, G_G0:G_A0])
    og_out[...] = (g * jax.nn.sigmoid(g)).astype(og_out.dtype)
    a = _dot(h, wg_ref[:, G_A0:G_S0]).astype(BF16)
    pre = _dot(a, wa2_ref[...]) + ba2_ref[...]
    lg_out[...] = jax.nn.log_sigmoid(pre) * (1.0 / GATE_TEMP)
    gate_out[...] = jax.nn.sigmoid(_dot(h, wg_ref[:, G_S0:G_END])).astype(gate_out.dtype)

    v = kv[:, MLA_H * NOPE_DIM:]
    if transpose_v:
        vt = v.T.astype(BF16)
        ones = jnp.ones((ONES_ROWS, v.shape[0]), BF16)
        for hd in range(MLA_H):
            v_out[hd * VT_ROWS:hd * VT_ROWS + V_DIM, :] = vt[hd * V_DIM:(hd + 1) * V_DIM, :]
            v_out[hd * VT_ROWS + V_DIM:(hd + 1) * VT_ROWS, :] = ones
    else:
        v_out[...] = v.astype(BF16)

    cos = cos_ref[...]
    sin = sin_ref[...]
    lane = lax.broadcasted_iota(jnp.int32, (1, LANES), 1)
    ka = ckvr[:, KV_LORA:KV_LORA + LANES]
    kb = ckvr[:, KV_LORA + LANES:KV_LORA + 2 * LANES]
    head_lanes = [((lane >= p * ROPE_DIM) & (lane < (p + 1) * ROPE_DIM)).astype(F32) for p in range(2)]
    ssq_kr = jnp.sum(ka * ka * head_lanes[0], axis=-1, keepdims=True)
    k_rot = [(ka * gk_ref[1 + p:2 + p, :]) * cos + (kb * gk_ref[3 + p:4 + p, :]) * sin for p in range(2)]
    q_scale = QK_DIM ** -0.5 * LOG2E
    rope0 = MLA_H * NOPE_DIM
    for hd in range(MLA_H):
        j, p = divmod(hd, 2)
        qn = qall[:, hd * NOPE_DIM:(hd + 1) * NOPE_DIM]
        qa = qall[:, rope0 + j * LANES:rope0 + (j + 1) * LANES]
        qb = qall[:, rope0 + 4 * LANES + j * LANES:rope0 + 4 * LANES + (j + 1) * LANES]
        ssq = (jnp.sum(qn * qn, axis=-1, keepdims=True)
               + jnp.sum(qa * qa * head_lanes[p], axis=-1, keepdims=True))
        rs = lax.rsqrt(ssq * (1.0 / QK_DIM) + NORM_EPS) * q_scale
        q_rot = (qa * gq_ref[1 + p:2 + p, :]) * cos + (qb * gq_ref[3 + p:4 + p, :]) * sin
        q_out[:, hd * QK_PAD:hd * QK_PAD + LANES] = (qn * gq_ref[0:1, :] * rs).astype(BF16)
        q_out[:, hd * QK_PAD + LANES:(hd + 1) * QK_PAD] = (q_rot * rs).astype(BF16)
        kn = kv[:, hd * NOPE_DIM:(hd + 1) * NOPE_DIM]
        rsk = lax.rsqrt((jnp.sum(kn * kn, axis=-1, keepdims=True) + ssq_kr) * (1.0 / QK_DIM) + NORM_EPS)
        k_out[:, hd * QK_PAD:hd * QK_PAD + LANES] = (kn * gk_ref[0:1, :] * rsk).astype(BF16)
        k_out[:, hd * QK_PAD + LANES:(hd + 1) * QK_PAD] = (k_rot[p] * rsk).astype(BF16)


def _project(x2d, w, cos_t, sin_t, seq_len, tm, transpose_v):
    n = x2d.shape[0]
    nt = seq_len // tm
    row = lambda i: (i, 0)
    pos = lambda i: (i % nt, 0)
    v_dims = MLA_H * V_DIM
    vt_rows = MLA_H * VT_ROWS
    nk, nv = GLA_H * GLA_DK, GLA_H * GLA_DV
    gla_outs = ((nk, F32), (nk, F32), (nv, BF16), (nv, BF16), (2 * nk, F32), (2 * D_MODEL, BF16))
    return pl.pallas_call(
        functools.partial(_proj_kernel, transpose_v=transpose_v),
        grid=(n // tm,),
        in_specs=[
            pl.BlockSpec((tm, D_MODEL), row),
            _resident((1, D_MODEL)),
            _resident(w["w_cq"].shape),
            _resident(w["w_ckvr"].shape),
            _resident((1, Q_LORA)),
            _resident(w["w_uq"].shape),
            _resident((1, KV_LORA)),
            _resident(w["w_ukv"].shape),
            _resident((8, LANES)),
            _resident((8, LANES)),
            pl.BlockSpec((tm, LANES), pos),
            pl.BlockSpec((tm, LANES), pos),
            _resident(w["w_g"].shape),
            _resident(w["w_a2"].shape),
            _resident((1, 2 * nk)),
        ],
        out_specs=[
            pl.BlockSpec((tm, MLA_H * QK_PAD), row),
            pl.BlockSpec((tm, MLA_H * QK_PAD), row),
            pl.BlockSpec((vt_rows, tm), lambda i: (0, i)) if transpose_v else pl.BlockSpec((tm, v_dims), row),
        ] + [pl.BlockSpec((tm, width), row) for width, _ in gla_outs],
        out_shape=[
            jax.ShapeDtypeStruct((n, MLA_H * QK_PAD), BF16),
            jax.ShapeDtypeStruct((n, MLA_H * QK_PAD), BF16),
            jax.ShapeDtypeStruct((vt_rows, n) if transpose_v else (n, v_dims), BF16),
        ] + [jax.ShapeDtypeStruct((n, width), dtype) for width, dtype in gla_outs],
        compiler_params=_params("parallel"),
        name="project",
    )(x2d, w["attn_norm"], w["w_cq"], w["w_ckvr"], w["q_a_norm"], w["w_uq"], w["kv_a_norm"],
      w["w_ukv"], w["g_q"], w["g_k"], cos_t, sin_t, w["w_g"], w["w_a2"], w["b_a2"])


def _attn_kernel(q_ref, qn_ref, k_ref, vt_ref, km_ref, vmt_ref, o_ref,
                 s_even, s_odd, acc_scr, qt_scr, qtn_scr, max_scr, *, tk, ck, n_tiles):
    qi = pl.program_id(2)
    n_q = pl.num_programs(2)
    s_scr = (s_even, s_odd)
    n_chunks = tk // ck

    def transposed(ref):
        return ref[0].astype(F32).T.astype(BF16)

    def score_chunk(i, c, slot, running_max, qt_ref):
        off = pl.multiple_of(i * tk + c * ck, ck)
        s = _dot(k_ref[0, pl.ds(off, ck), :], qt_ref[...])
        s_scr[slot][c * ck:(c + 1) * ck, :] = s
        s_max = jnp.max(s, axis=0, keepdims=True)
        return s_max if running_max is None else jnp.maximum(running_max, s_max)

    def gather_chunk(i, c, slot, m, partial):
        off = pl.multiple_of(i * tk + c * ck, ck)
        p = jnp.exp2(s_scr[slot][c * ck:(c + 1) * ck, :] - m).astype(BF16)
        d = _dot(vt_ref[:, pl.ds(off, ck)], p)
        return d if partial is None else partial + d

    def step(i, slot, s_max, m, following="tile"):
        m_new = jnp.maximum(m, s_max)
        next_max, partial = None, None
        for c in range(n_chunks):
            if following == "tile":
                next_max = score_chunk(i + 1, c, 1 - slot, next_max, qt_scr)
            elif following == "block":
                next_max = score_chunk(0, c, 1 - slot, next_max, qtn_scr)
            partial = gather_chunk(i, c, slot, m_new, partial)
        acc_scr[...] = jnp.exp2(m - m_new) * acc_scr[...] + partial
        return next_max, m_new

    def two_steps(j, carry):
        carry = step(2 * j, 0, *carry)
        return step(2 * j + 1, 1, *carry)

    @pl.when(qi == 0)
    def _():
        qt_scr[...] = transposed(q_ref)
        max0 = None
        for c in range(n_chunks):
            max0 = score_chunk(0, c, 0, max0, qt_scr)
        max_scr[...] = max0

    qtn_scr[...] = transposed(qn_ref)
    s_meta = _dot(km_ref[...], qt_scr[...])
    m_meta = jnp.max(s_meta, axis=0, keepdims=True)
    acc_scr[...] = _dot(vmt_ref[...], jnp.exp2(s_meta - m_meta).astype(BF16))
    carry = lax.fori_loop(0, n_tiles // 2 - 1, two_steps, (max_scr[...], m_meta), unroll=True)
    max_last, m = step(n_tiles - 2, 0, *carry)

    @pl.when(qi < n_q - 1)
    def _():
        max_scr[...] = step(n_tiles - 1, 1, max_last, m, following="block")[0]
        qt_scr[...] = qtn_scr[...]

    @pl.when(qi == n_q - 1)
    def _():
        step(n_tiles - 1, 1, max_last, m, following=None)

    acc = acc_scr[...]
    o_ref[0] = (acc[:V_DIM] / acc[V_DIM:V_DIM + 1]).T.astype(o_ref.dtype)


def _attention(q, k, vt, k_meta, vt_meta, tq, tk):
    b, seq, _ = q.shape
    assert seq % (2 * tk) == 0, (seq, tk)
    n_q = seq // tq
    return pl.pallas_call(
        functools.partial(_attn_kernel, tk=tk, ck=min(tk, MXU_DIM), n_tiles=seq // tk),
        grid=(b, MLA_H, n_q),
        in_specs=[
            pl.BlockSpec((1, tq, QK_PAD), lambda bi, hi, qi: (bi, qi, hi)),
            pl.BlockSpec((1, tq, QK_PAD), lambda bi, hi, qi: (bi, jnp.minimum(qi + 1, n_q - 1), hi)),
            pl.BlockSpec((1, seq, QK_PAD), lambda bi, hi, qi: (bi, 0, hi)),
            pl.BlockSpec((VT_ROWS, seq), lambda bi, hi, qi: (hi, bi)),
            pl.BlockSpec((N_META, QK_PAD), lambda bi, hi, qi: (0, hi)),
            pl.BlockSpec((VT_ROWS, N_META), lambda bi, hi, qi: (hi, 0)),
        ],
        out_specs=pl.BlockSpec((1, tq, V_DIM), lambda bi, hi, qi: (bi, qi, hi)),
        out_shape=jax.ShapeDtypeStruct((b, seq, MLA_H * V_DIM), BF16),
        scratch_shapes=[
            pltpu.VMEM((tk, tq), F32),
            pltpu.VMEM((tk, tq), F32),
            pltpu.VMEM((VT_ROWS, tq), F32),
            pltpu.VMEM((QK_PAD, tq), BF16),
            pltpu.VMEM((QK_PAD, tq), BF16),
            pltpu.VMEM((1, tq), F32),
        ],
        compiler_params=_params("arbitrary", "arbitrary", "arbitrary"),
        name="attention",
    )(q, q, k, vt, k_meta, vt_meta)


def _gla_scan_kernel(q_ref, k_ref, v_ref, lg_ref, s0_ref, o_ref, sfin_ref, state, u_scr, dec_scr, qe_scr,
                     *, reverse, n_chunks):
    blk = pl.program_id(1)

    @pl.when(blk == 0)
    def _():
        state[...] = s0_ref[...]

    ri = lax.broadcasted_iota(jnp.int32, (GLA_CHUNK, GLA_CHUNK), 0)
    ci = lax.broadcasted_iota(jnp.int32, (GLA_CHUNK, GLA_CHUNK), 1)
    keep = (ri <= ci) if reverse else (ri >= ci)
    csum = jnp.where(keep, 1.0, 0.0).astype(BF16)
    csum2 = jnp.concatenate([csum, csum], axis=1)
    mid = GLA_CHUNK // 2 if reverse else GLA_CHUNK // 2 - 1
    end = 0 if reverse else GLA_CHUNK - 1
    group = math.gcd(n_chunks, 8)

    heads = [(slice(hd * GLA_DK, (hd + 1) * GLA_DK), slice(hd * GLA_DV, (hd + 1) * GLA_DV)) for hd in range(GLA_H)]

    def local(g, carry):
        cs = [g * group + j for j in range(group)]
        rows = [pl.ds(pl.multiple_of(c * GLA_CHUNK, GLA_CHUNK), GLA_CHUNK) for c in cs]
        bcums = []
        for r in rows:
            lg = lg_ref[0, r, :]
            lg_hi = lg.astype(BF16)
            lg_lo = (lg - lg_hi.astype(F32)).astype(BF16)
            bcums.append(_dot(csum2, jnp.concatenate([lg_hi, lg_lo], axis=0)))
        qss, kss, kds = [], [], []
        for c, r, bcum in zip(cs, rows, bcums):
            b_mid = bcum[mid:mid + 1, :]
            b_end = bcum[end:end + 1, :]
            q = q_ref[0, r, :]
            k = k_ref[0, r, :]
            qss.append((q * jnp.exp(bcum - b_mid)).astype(BF16))
            kss.append((k * jnp.exp(b_mid - bcum)).astype(BF16))
            kds.append((k * jnp.exp(b_end - bcum)).astype(BF16))
            qe_scr[c] = (q * jnp.exp(bcum)).astype(BF16)
            dec_scr[c] = jnp.broadcast_to(jnp.exp(b_end), (8, GLA_H * GLA_DK))
        attn = [[_dot_nt(qs[:, kc], ks[:, kc]) for kc, _ in heads] for qs, ks in zip(qss, kss)]
        for c, r, kd in zip(cs, rows, kds):
            for hd, (kc, vc) in enumerate(heads):
                u_scr[c, hd] = _dot_tn(v_ref[0, r, vc], kd[:, kc])
        for r, a_c in zip(rows, attn):
            for (_, vc), a in zip(heads, a_c):
                o_ref[0, r, vc] = _dot(jnp.where(keep, a, 0.0).astype(BF16), v_ref[0, r, vc])
        return carry

    lax.fori_loop(0, n_chunks // group, local, 0, unroll=True)

    def recur(i, carry):
        c = (n_chunks - 1 - i) if reverse else i
        rows = pl.ds(pl.multiple_of(c * GLA_CHUNK, GLA_CHUNK), GLA_CHUNK)
        for hd, (kc, vc) in enumerate(heads):
            st = state[hd]
            o_ref[0, rows, vc] += _dot_nt(qe_scr[c, :, kc], st.astype(BF16))
            state[hd] = st * dec_scr[c, 0:1, kc] + u_scr[c, hd]
        return carry

    lax.fori_loop(0, n_chunks, recur, 0, unroll=True)

    @pl.when(blk == pl.num_programs(1) - 1)
    def _():
        sfin_ref[0] = state[...]


def _gla_scan(q, k, v, lg, s0, reverse, t_blk):
    b, t, _ = q.shape
    nb = t // t_blk
    lg_col = 1 if reverse else 0
    blk = (lambda bi, i: (bi, nb - 1 - i, 0)) if reverse else (lambda bi, i: (bi, i, 0))
    lg_blk = (lambda bi, i: (bi, nb - 1 - i, lg_col)) if reverse else (lambda bi, i: (bi, i, lg_col))
    st_shape = (GLA_H, GLA_DV, GLA_DK)
    n_chunks = t_blk // GLA_CHUNK
    return pl.pallas_call(
        functools.partial(_gla_scan_kernel, reverse=reverse, n_chunks=n_chunks),
        grid=(b, nb),
        in_specs=[
            pl.BlockSpec((1, t_blk, GLA_H * GLA_DK), blk),
            pl.BlockSpec((1, t_blk, GLA_H * GLA_DK), blk),
            pl.BlockSpec((1, t_blk, GLA_H * GLA_DV), blk),
            pl.BlockSpec((1, t_blk, GLA_H * GLA_DK), lg_blk),
            pl.BlockSpec(st_shape, lambda bi, i: (0, 0, 0)),
        ],
        out_specs=[
            pl.BlockSpec((1, t_blk, GLA_H * GLA_DV), blk),
            pl.BlockSpec((1,) + st_shape, lambda bi, i: (bi, 0, 0, 0)),
        ],
        out_shape=[
            jax.ShapeDtypeStruct((b, t, GLA_H * GLA_DV), F32),
            jax.ShapeDtypeStruct((b,) + st_shape, F32),
        ],
        scratch_shapes=[
            pltpu.VMEM(st_shape, F32),
            pltpu.VMEM((n_chunks,) + st_shape, F32),
            pltpu.VMEM((n_chunks, 8, GLA_H * GLA_DK), F32),
            pltpu.VMEM((n_chunks, GLA_CHUNK, GLA_H * GLA_DK), BF16),
        ],
        compiler_params=_params("parallel", "arbitrary"),
        name="gla_scan_bwd" if reverse else "gla_scan_fwd",
    )(q, k, v, lg, s0)


def _mix_ffn_kernel(x_ref, oa_ref, of_ref, ob_ref, og_ref, gate_ref, womla_ref, onorm_ref, wogla_ref,
                    wout_ref, fn_ref, wgate_ref, wup_ref, wdown_ref, y_ref):
    y_a = _dot(oa_ref[...], womla_ref[...])
    o = of_ref[...] + ob_ref[...]
    og = og_ref[...]
    parts = []
    for hd in range(GLA_H):
        cols = slice(hd * GLA_DV, (hd + 1) * GLA_DV)
        parts.append((_rms(o[:, cols]) * onorm_ref[...] * og[:, cols]).astype(BF16))
    y_b = _dot(jnp.concatenate(parts, axis=-1), wogla_ref[...])
    gates = gate_ref[...]
    mixed = (gates[:, :D_MODEL] * y_a + gates[:, D_MODEL:] * y_b).astype(BF16)
    x1 = x_ref[...] + _dot(mixed, wout_ref[...])
    h = (_rms(x1) * fn_ref[...]).astype(BF16)
    g = _dot(h, wgate_ref[...])
    u = _dot(h, wup_ref[...])
    act = (g * jax.nn.sigmoid(g) * u).astype(BF16)
    y_ref[...] = x1 + _dot(act, wdown_ref[...])


def _mix_ffn(x2d, o_attn, o_f, o_b, og, gates, w, tm):
    n = x2d.shape[0]
    row = lambda i: (i, 0)
    return pl.pallas_call(
        _mix_ffn_kernel,
        grid=(n // tm,),
        in_specs=[
            pl.BlockSpec((tm, D_MODEL), row),
            pl.BlockSpec((tm, MLA_H * V_DIM), row),
            pl.BlockSpec((tm, GLA_H * GLA_DV), row),
            pl.BlockSpec((tm, GLA_H * GLA_DV), row),
            pl.BlockSpec((tm, GLA_H * GLA_DV), row),
            pl.BlockSpec((tm, 2 * D_MODEL), row),
            _resident(w["w_o_mla"].shape),
            _resident((1, GLA_DV)),
            _resident(w["w_o_gla"].shape),
            _resident(w["w_out"].shape),
            _resident((1, D_MODEL)),
            _resident(w["w_ffn_gate"].shape),
            _resident(w["w_ffn_up"].shape),
            _resident(w["w_ffn_down"].shape),
        ],
        out_specs=pl.BlockSpec((tm, D_MODEL), row),
        out_shape=jax.ShapeDtypeStruct((n, D_MODEL), F32),
        compiler_params=_params("parallel"),
        name="mix_ffn",
    )(x2d, o_attn, o_f, o_b, og, gates, w["w_o_mla"], w["gla_o_norm"], w["w_o_gla"], w["w_out"],
      w["ffn_norm"], w["w_ffn_gate"], w["w_ffn_up"], w["w_ffn_down"])


def _prep_weights(attn_norm, w_in, q_a_norm, w_uq, kv_a_norm, w_ukv, q_norm, k_norm, w_o_mla,
                  w_a2_fwd, b_a2_fwd, w_a2_bwd, b_a2_bwd, gla_o_norm, w_o_gla, w_out, ffn_norm,
                  w_ffn_gate, w_ffn_up, w_ffn_down):
    half = ROPE_DIM // 2
    w_in, w_uq, w_ukv = w_in.astype(BF16), w_uq.astype(BF16), w_ukv.astype(BF16)
    o = 0
    cols = {}
    for name, size in (("cq", Q_LORA), ("ckv", KV_LORA), ("kr", ROPE_DIM), ("gq", GLA_H * GLA_DK),
                       ("gk", GLA_H * GLA_DK), ("gv", GLA_H * GLA_DV), ("gg", GLA_H * GLA_DV),
                       ("af", GATE_RANK), ("ab", GATE_RANK), ("ga", D_MODEL), ("gb", D_MODEL)):
        cols[name] = w_in[:, o:o + size]
        o += size
    kr1, kr2 = cols["kr"][:, :half], cols["kr"][:, half:]
    w_ckvr = jnp.concatenate([cols["ckv"], kr1, kr2, kr1, kr2, kr2, kr1, kr2, kr1], axis=1)

    uq = w_uq.reshape(Q_LORA, MLA_H, QK_DIM)
    uq_nope = uq[:, :, :NOPE_DIM].reshape(Q_LORA, MLA_H * NOPE_DIM)
    uq_r = uq[:, :, NOPE_DIM:]
    uq_a = uq_r.reshape(Q_LORA, MLA_H * ROPE_DIM)
    uq_b = jnp.concatenate([uq_r[:, :, half:], uq_r[:, :, :half]], axis=-1).reshape(Q_LORA, MLA_H * ROPE_DIM)
    ukv = w_ukv.reshape(KV_LORA, MLA_H, NOPE_DIM + V_DIM)
    w_ukv2 = jnp.concatenate([ukv[:, :, :NOPE_DIM].reshape(KV_LORA, MLA_H * NOPE_DIM),
                              ukv[:, :, NOPE_DIM:].reshape(KV_LORA, MLA_H * V_DIM)], axis=1)

    def gain_rows(g):
        z = jnp.zeros((ROPE_DIM,), F32)
        ga = g[NOPE_DIM:]
        gb = jnp.concatenate([g[NOPE_DIM + half:], g[NOPE_DIM:NOPE_DIM + half]])
        rows = [g[:NOPE_DIM], jnp.concatenate([ga, z]), jnp.concatenate([z, ga]),
                jnp.concatenate([gb, z]), jnp.concatenate([z, gb])]
        rows += [jnp.zeros((LANES,), F32)] * 3
        return jnp.stack(rows)

    zpad = jnp.zeros((D_MODEL, LANES - 2 * GATE_RANK), BF16)
    w_g = jnp.concatenate([cols["gq"], cols["gk"], cols["gv"], cols["gg"], cols["af"], cols["ab"], zpad,
                           cols["ga"], cols["gb"]], axis=1)
    nk = GLA_H * GLA_DK
    w_a2 = jnp.zeros((LANES, 2 * nk), F32)
    w_a2 = w_a2.at[:GATE_RANK, :nk].set(w_a2_fwd).at[GATE_RANK:2 * GATE_RANK, nk:].set(w_a2_bwd)
    return {
        "attn_norm": attn_norm[None], "w_cq": cols["cq"], "w_ckvr": w_ckvr,
        "q_a_norm": q_a_norm[None], "w_uq": jnp.concatenate([uq_nope, uq_a, uq_b], axis=1),
        "kv_a_norm": kv_a_norm[None], "w_ukv": w_ukv2,
        "g_q": gain_rows(q_norm), "g_k": gain_rows(k_norm),
        "w_g": w_g, "w_a2": w_a2.astype(BF16),
        "b_a2": jnp.concatenate([b_a2_fwd, b_a2_bwd])[None],
        "w_o_mla": w_o_mla.astype(BF16), "gla_o_norm": gla_o_norm[None], "w_o_gla": w_o_gla.astype(BF16),
        "w_out": w_out.astype(BF16), "ffn_norm": ffn_norm[None], "w_ffn_gate": w_ffn_gate.astype(BF16),
        "w_ffn_up": w_ffn_up.astype(BF16), "w_ffn_down": w_ffn_down.astype(BF16),
    }


def _rope_tables(length):
    inv = 1.0 / (ROPE_THETA ** (np.arange(0, ROPE_DIM, 2, dtype=np.float64) / ROPE_DIM))
    ang = np.arange(length, dtype=np.float64)[:, None] * inv[None, :]
    cos, sin = np.cos(ang), np.sin(ang)
    return (jnp.asarray(np.tile(cos, (1, 4)), F32),
            jnp.asarray(np.concatenate([-sin, sin, -sin, sin], axis=1), F32))


def _encode_group(x, w, meta, cos_t, sin_t):
    b, seq, _ = x.shape
    n = b * seq
    x2d = x.reshape(n, D_MODEL)
    tm = _row_tile(seq, 256)
    q, k, vt, gq, gk, gv, og, lg, gates = _project(
        x2d, w, cos_t[N_META:N_META + seq], sin_t[N_META:N_META + seq], seq, tm, True)

    o_attn = _attention(q.reshape(b, seq, -1), k.reshape(b, seq, -1), vt,
                        meta["k"], meta["vt"], _row_tile(seq // 2, 1024), _row_tile(seq // 4, 2048))

    t_blk = _row_tile(seq, 1024)
    g3 = lambda a: a.reshape(b, seq, -1)
    o_f, _ = _gla_scan(g3(gq), g3(gk), g3(gv), g3(lg), meta["state"], False, t_blk)
    o_b, _ = _gla_scan(g3(gq), g3(gk), g3(gv), g3(lg), jnp.zeros_like(meta["state"]), True, t_blk)

    y = _mix_ffn(x2d, o_attn.reshape(n, -1), o_f.reshape(n, -1), o_b.reshape(n, -1), og, gates, w, tm)
    return y.reshape(b, seq, D_MODEL)


def kernel(x_prompt, x_sample, meta_tokens, attn_norm, w_in, q_a_norm, w_uq, kv_a_norm, w_ukv, q_norm, k_norm, w_o_mla, w_a2_fwd, b_a2_fwd, w_a2_bwd, b_a2_bwd, gla_o_norm, w_o_gla, w_out, ffn_norm, w_ffn_gate, w_ffn_up, w_ffn_down):
    assert attn_norm.shape[0] == 1, "single-layer encoder"
    w = _prep_weights(attn_norm[0], w_in[0], q_a_norm[0], w_uq[0], kv_a_norm[0], w_ukv[0], q_norm[0],
                      k_norm[0], w_o_mla[0], w_a2_fwd[0], b_a2_fwd[0], w_a2_bwd[0], b_a2_bwd[0],
                      gla_o_norm[0], w_o_gla[0], w_out[0], ffn_norm[0], w_ffn_gate[0], w_ffn_up[0],
                      w_ffn_down[0])
    max_len = N_META + max(x_prompt.shape[1], x_sample.shape[1])
    cos_t, sin_t = _rope_tables(max_len)

    xm = meta_tokens.astype(F32)
    _, k_m, v_m, mq, mk, mv, _, mlg, _ = _project(xm, w, cos_t[:N_META], sin_t[:N_META], N_META, N_META, False)
    pad = lambda a: jnp.pad(a, ((GLA_CHUNK - N_META, 0), (0, 0)))[None]
    _, s_meta = _gla_scan(pad(mq), pad(mk), pad(mv), pad(mlg),
                          jnp.zeros((GLA_H, GLA_DV, GLA_DK), F32), False, GLA_CHUNK)
    vt_m = jnp.concatenate([v_m.T.reshape(MLA_H, V_DIM, N_META), jnp.ones((MLA_H, ONES_ROWS, N_META), BF16)],
                           axis=1).reshape(MLA_H * VT_ROWS, N_META)
    meta = {"k": k_m, "vt": vt_m, "state": s_meta[0]}

    return (_encode_group(x_prompt, w, meta, cos_t, sin_t),
            _encode_group(x_sample, w, meta, cos_t, sin_t))
```

```python
import functools
import math

import jax
import jax.numpy as jnp
import numpy as np
from jax import lax
from jax.experimental import pallas as pl
from jax.experimental.pallas import tpu as pltpu

D_MODEL = 1024
N_META = 16
MLA_H = 8
NOPE_DIM = 128
ROPE_DIM = 64
QK_DIM = NOPE_DIM + ROPE_DIM
V_DIM = 128
Q_LORA = 768
KV_LORA = 256
ROPE_THETA = 10000.0
GLA_H = 4
GLA_DK = 128
GLA_DV = 256
GATE_RANK = 16
GATE_TEMP = 16.0
GLA_CHUNK = 64
NORM_EPS = 1e-6

QK_PAD = 256
ONES_ROWS = 16
VT_ROWS = V_DIM + ONES_ROWS
LANES = 128
MXU_DIM = 256
LOG2E = 1.4426950408889634
VMEM_LIMIT = 56 * 1024 * 1024

F32 = jnp.float32
BF16 = jnp.bfloat16


def _dot(a, b):
    return jnp.dot(a, b, preferred_element_type=F32)


def _dot_nt(a, b):
    return lax.dot_general(a, b, (((1,), (1,)), ((), ())), preferred_element_type=F32)


def _dot_tn(a, b):
    return lax.dot_general(a, b, (((0,), (0,)), ((), ())), preferred_element_type=F32)


def _rms(x):
    return x * lax.rsqrt(jnp.mean(x * x, axis=-1, keepdims=True) + NORM_EPS)


def _resident(shape):
    return pl.BlockSpec(shape, lambda *_: (0,) * len(shape), pipeline_mode=pl.Buffered(1))


def _params(*sem):
    return pltpu.CompilerParams(dimension_semantics=sem, vmem_limit_bytes=VMEM_LIMIT)


def _row_tile(n, want):
    t = min(n, want)
    assert n % t == 0, (n, t)
    return t


G_Q0 = 0
G_K0 = G_Q0 + GLA_H * GLA_DK
G_V0 = G_K0 + GLA_H * GLA_DK
G_G0 = G_V0 + GLA_H * GLA_DV
G_A0 = G_G0 + GLA_H * GLA_DV
G_S0 = G_A0 + LANES
G_END = G_S0 + 2 * D_MODEL


def _proj_kernel(x_ref, an_ref, wcq_ref, wckv_ref, qan_ref, wuq_ref, kvan_ref, wukv_ref, gq_ref, gk_ref,
                 cos_ref, sin_ref, wg_ref, wa2_ref, ba2_ref,
                 q_out, k_out, v_out, gla_q_out, gla_k_out, gla_v_out, og_out, lg_out, gate_out, *, transpose_v):
    h = (_rms(x_ref[...]) * an_ref[...]).astype(BF16)
    cq = _dot(h, wcq_ref[...])
    ckvr = _dot(h, wckv_ref[...])
    gla_q_out[...] = _dot(h, wg_ref[:, G_Q0:G_K0]) * (GLA_DK ** -0.5)
    gla_k_out[...] = _dot(h, wg_ref[:, G_K0:G_V0])
    cqn = (_rms(cq) * qan_ref[...]).astype(BF16)
    ckvn = (_rms(ckvr[:, :KV_LORA]) * kvan_ref[...]).astype(BF16)
    qall = _dot(cqn, wuq_ref[...])
    kv = _dot(ckvn, wukv_ref[...])
    gla_v_out[...] = _dot(h, wg_ref[:, G_V0:G_G0]).astype(BF16)
    g = _dot(h, wg_ref[:, G_G0:G_A0])
    og_out[...] = (g * jax.nn.sigmoid(g)).astype(og_out.dtype)
    a = _dot(h, wg_ref[:, G_A0:G_S0]).astype(BF16)
    pre = _dot(a, wa2_ref[...]) + ba2_ref[...]
    lg_out[...] = jax.nn.log_sigmoid(pre) * (1.0 / GATE_TEMP)
    gate_out[...] = jax.nn.sigmoid(_dot(h, wg_ref[:, G_S0:G_END])).astype(gate_out.dtype)

    v = kv[:, MLA_H * NOPE_DIM:]
    if transpose_v:
        vt = v.T.astype(BF16)
        ones = jnp.ones((ONES_ROWS, v.shape[0]), BF16)
        for hd in range(MLA_H):
            v_out[hd * VT_ROWS:hd * VT_ROWS + V_DIM, :] = vt[hd * V_DIM:(hd + 1) * V_DIM, :]
            v_out[hd * VT_ROWS + V_DIM:(hd + 1) * VT_ROWS, :] = ones
    else:
        v_out[...] = v.astype(BF16)

    cos = cos_ref[...]
    sin = sin_ref[...]
    lane = lax.broadcasted_iota(jnp.int32, (1, LANES), 1)
    ka = ckvr[:, KV_LORA:KV_LORA + LANES]
    kb = ckvr[:, KV_LORA + LANES:KV_LORA + 2 * LANES]
    head_lanes = [((lane >= p * ROPE_DIM) & (lane < (p + 1) * ROPE_DIM)).astype(F32) for p in range(2)]
    ssq_kr = jnp.sum(ka * ka * head_lanes[0], axis=-1, keepdims=True)
    k_rot = [(ka * gk_ref[1 + p:2 + p, :]) * cos + (kb * gk_ref[3 + p:4 + p, :]) * sin for p in range(2)]
    q_scale = QK_DIM ** -0.5 * LOG2E
    rope0 = MLA_H * NOPE_DIM
    for hd in range(MLA_H):
        j, p = divmod(hd, 2)
        qn = qall[:, hd * NOPE_DIM:(hd + 1) * NOPE_DIM]
        qa = qall[:, rope0 + j * LANES:rope0 + (j + 1) * LANES]
        qb = qall[:, rope0 + 4 * LANES + j * LANES:rope0 + 4 * LANES + (j + 1) * LANES]
        ssq = (jnp.sum(qn * qn, axis=-1, keepdims=True)
               + jnp.sum(qa * qa * head_lanes[p], axis=-1, keepdims=True))
        rs = lax.rsqrt(ssq * (1.0 / QK_DIM) + NORM_EPS) * q_scale
        q_rot = (qa * gq_ref[1 + p:2 + p, :]) * cos + (qb * gq_ref[3 + p:4 + p, :]) * sin
        q_out[:, hd * QK_PAD:hd * QK_PAD + LANES] = (qn * gq_ref[0:1, :] * rs).astype(BF16)
        q_out[:, hd * QK_PAD + LANES:(hd + 1) * QK_PAD] = (q_rot * rs).astype(BF16)
        kn = kv[:, hd * NOPE_DIM:(hd + 1) * NOPE_DIM]
        rsk = lax.rsqrt((jnp.sum(kn * kn, axis=-1, keepdims=True) + ssq_kr) * (1.0 / QK_DIM) + NORM_EPS)
        k_out[:, hd * QK_PAD:hd * QK_PAD + LANES] = (kn * gk_ref[0:1, :] * rsk).astype(BF16)
        k_out[:, hd * QK_PAD + LANES:(hd + 1) * QK_PAD] = (k_rot[p] * rsk).astype(BF16)


def _project(x2d, w, cos_t, sin_t, seq_len, tm, transpose_v):
    n = x2d.shape[0]
    nt = seq_len // tm
    row = lambda i: (i, 0)
    pos = lambda i: (i % nt, 0)
    v_dims = MLA_H * V_DIM
    vt_rows = MLA_H * VT_ROWS
    nk, nv = GLA_H * GLA_DK, GLA_H * GLA_DV
    gla_outs = ((nk, F32), (nk, F32), (nv, BF16), (nv, BF16), (2 * nk, F32), (2 * D_MODEL, BF16))
    return pl.pallas_call(
        functools.partial(_proj_kernel, transpose_v=transpose_v),
        grid=(n // tm,),
        in_specs=[
            pl.BlockSpec((tm, D_MODEL), row),
            _resident((1, D_MODEL)),
            _resident(w["w_cq"].shape),
            _resident(w["w_ckvr"].shape),
            _resident((1, Q_LORA)),
            _resident(w["w_uq"].shape),
            _resident((1, KV_LORA)),
            _resident(w["w_ukv"].shape),
            _resident((8, LANES)),
            _resident((8, LANES)),
            pl.BlockSpec((tm, LANES), pos),
            pl.BlockSpec((tm, LANES), pos),
            _resident(w["w_g"].shape),
            _resident(w["w_a2"].shape),
            _resident((1, 2 * nk)),
        ],
        out_specs=[
            pl.BlockSpec((tm, MLA_H * QK_PAD), row),
            pl.BlockSpec((tm, MLA_H * QK_PAD), row),
            pl.BlockSpec((vt_rows, tm), lambda i: (0, i)) if transpose_v else pl.BlockSpec((tm, v_dims), row),
        ] + [pl.BlockSpec((tm, width), row) for width, _ in gla_outs],
        out_shape=[
            jax.ShapeDtypeStruct((n, MLA_H * QK_PAD), BF16),
            jax.ShapeDtypeStruct((n, MLA_H * QK_PAD), BF16),
            jax.ShapeDtypeStruct((vt_rows, n) if transpose_v else (n, v_dims), BF16),
        ] + [jax.ShapeDtypeStruct((n, width), dtype) for width, dtype in gla_outs],
        compiler_params=_params("parallel"),
        name="project",
    )(x2d, w["attn_norm"], w["w_cq"], w["w_ckvr"], w["q_a_norm"], w["w_uq"], w["kv_a_norm"],
      w["w_ukv"], w["g_q"], w["g_k"], cos_t, sin_t, w["w_g"], w["w_a2"], w["b_a2"])


def _attn_kernel(q_ref, qn_ref, k_ref, vt_ref, km_ref, vmt_ref, o_ref,
                 s_even, s_odd, acc_scr, qt_scr, qtn_scr, max_scr, *, tk, ck, n_tiles):
    qi = pl.program_id(2)
    n_q = pl.num_programs(2)
    s_scr = (s_even, s_odd)
    n_chunks = tk // ck

    def transposed(ref):
        return ref[0].astype(F32).T.astype(BF16)

    def score_chunk(i, c, slot, running_max, qt_ref):
        off = pl.multiple_of(i * tk + c * ck, ck)
        s = _dot(k_ref[0, pl.ds(off, ck), :], qt_ref[...])
        s_scr[slot][c * ck:(c + 1) * ck, :] = s
        s_max = jnp.max(s, axis=0, keepdims=True)
        return s_max if running_max is None else jnp.maximum(running_max, s_max)

    def gather_chunk(i, c, slot, m, partial):
        off = pl.multiple_of(i * tk + c * ck, ck)
        p = jnp.exp2(s_scr[slot][c * ck:(c + 1) * ck, :] - m).astype(BF16)
        d = _dot(vt_ref[:, pl.ds(off, ck)], p)
        return d if partial is None else partial + d

    def step(i, slot, s_max, m, following="tile"):
        m_new = jnp.maximum(m, s_max)
        next_max, partial = None, None
        for c in range(n_chunks):
            if following == "tile":
                next_max = score_chunk(i + 1, c, 1 - slot, next_max, qt_scr)
            elif following == "block":
                next_max = score_chunk(0, c, 1 - slot, next_max, qtn_scr)
            partial = gather_chunk(i, c, slot, m_new, partial)
        acc_scr[...] = jnp.exp2(m - m_new) * acc_scr[...] + partial
        return next_max, m_new

    def two_steps(j, carry):
        carry = step(2 * j, 0, *carry)
        return step(2 * j + 1, 1, *carry)

    @pl.when(qi == 0)
    def _():
        qt_scr[...] = transposed(q_ref)
        max0 = None
        for c in range(n_chunks):
            max0 = score_chunk(0, c, 0, max0, qt_scr)
        max_scr[...] = max0

    qtn_scr[...] = transposed(qn_ref)
    s_meta = _dot(km_ref[...], qt_scr[...])
    m_meta = jnp.max(s_meta, axis=0, keepdims=True)
    acc_scr[...] = _dot(vmt_ref[...], jnp.exp2(s_meta - m_meta).astype(BF16))
    carry = lax.fori_loop(0, n_tiles // 2 - 1, two_steps, (max_scr[...], m_meta), unroll=True)
    max_last, m = step(n_tiles - 2, 0, *carry)

    @pl.when(qi < n_q - 1)
    def _():
        max_scr[...] = step(n_tiles - 1, 1, max_last, m, following="block")[0]
        qt_scr[...] = qtn_scr[...]

    @pl.when(qi == n_q - 1)
    def _():
        step(n_tiles - 1, 1, max_last, m, following=None)

    acc = acc_scr[...]
    o_ref[0] = (acc[:V_DIM] / acc[V_DIM:V_DIM + 1]).T.astype(o_ref.dtype)


def _attention(q, k, vt, k_meta, vt_meta, tq, tk):
    b, seq, _ = q.shape
    assert seq % (2 * tk) == 0, (seq, tk)
    n_q = seq // tq
    return pl.pallas_call(
        functools.partial(_attn_kernel, tk=tk, ck=min(tk, MXU_DIM), n_tiles=seq // tk),
        grid=(b, MLA_H, n_q),
        in_specs=[
            pl.BlockSpec((1, tq, QK_PAD), lambda bi, hi, qi: (bi, qi, hi)),
            pl.BlockSpec((1, tq, QK_PAD), lambda bi, hi, qi: (bi, jnp.minimum(qi + 1, n_q - 1), hi)),
            pl.BlockSpec((1, seq, QK_PAD), lambda bi, hi, qi: (bi, 0, hi)),
            pl.BlockSpec((VT_ROWS, seq), lambda bi, hi, qi: (hi, bi)),
            pl.BlockSpec((N_META, QK_PAD), lambda bi, hi, qi: (0, hi)),
            pl.BlockSpec((VT_ROWS, N_META), lambda bi, hi, qi: (hi, 0)),
        ],
        out_specs=pl.BlockSpec((1, tq, V_DIM), lambda bi, hi, qi: (bi, qi, hi)),
        out_shape=jax.ShapeDtypeStruct((b, seq, MLA_H * V_DIM), BF16),
        scratch_shapes=[
            pltpu.VMEM((tk, tq), F32),
            pltpu.VMEM((tk, tq), F32),
            pltpu.VMEM((VT_ROWS, tq), F32),
            pltpu.VMEM((QK_PAD, tq), BF16),
            pltpu.VMEM((QK_PAD, tq), BF16),
            pltpu.VMEM((1, tq), F32),
        ],
        compiler_params=_params("arbitrary", "arbitrary", "arbitrary"),
        name="attention",
    )(q, q, k, vt, k_meta, vt_meta)


def _scan_phases(q_ref, k_ref, v_ref, lg_ref, o_ref, state, u_scr, dec_scr, qe_scr, reverse, n_chunks):
    ri = lax.broadcasted_iota(jnp.int32, (GLA_CHUNK, GLA_CHUNK), 0)
    ci = lax.broadcasted_iota(jnp.int32, (GLA_CHUNK, GLA_CHUNK), 1)
    keep = (ri <= ci) if reverse else (ri >= ci)
    csum = jnp.where(keep, 1.0, 0.0).astype(BF16)
    csum2 = jnp.concatenate([csum, csum], axis=1)
    mid = GLA_CHUNK // 2 if reverse else GLA_CHUNK // 2 - 1
    end = 0 if reverse else GLA_CHUNK - 1
    group = math.gcd(n_chunks, 8)

    heads = [(slice(hd * GLA_DK, (hd + 1) * GLA_DK), slice(hd * GLA_DV, (hd + 1) * GLA_DV)) for hd in range(GLA_H)]

    def local(g):
        cs = [g * group + j for j in range(group)]
        rows = [pl.ds(c * GLA_CHUNK, GLA_CHUNK) for c in cs]
        bcums = []
        for r in rows:
            lg = lg_ref[0, r, :]
            lg_hi = lg.astype(BF16)
            lg_lo = (lg - lg_hi.astype(F32)).astype(BF16)
            bcums.append(_dot(csum2, jnp.concatenate([lg_hi, lg_lo], axis=0)))
        qss, kss, kds = [], [], []
        for c, r, bcum in zip(cs, rows, bcums):
            b_mid = bcum[mid:mid + 1, :]
            b_end = bcum[end:end + 1, :]
            q = q_ref[0, r, :]
            k = k_ref[0, r, :]
            qss.append((q * jnp.exp(bcum - b_mid)).astype(BF16))
            kss.append((k * jnp.exp(b_mid - bcum)).astype(BF16))
            kds.append((k * jnp.exp(b_end - bcum)).astype(BF16))
            qe_scr[c] = (q * jnp.exp(bcum)).astype(BF16)
            dec_scr[c] = jnp.broadcast_to(jnp.exp(b_end), (8, GLA_H * GLA_DK))
        attn = [[_dot_nt(qs[:, kc], ks[:, kc]) for kc, _ in heads] for qs, ks in zip(qss, kss)]
        for c, r, kd in zip(cs, rows, kds):
            for hd, (kc, vc) in enumerate(heads):
                u_scr[c, hd] = _dot_tn(v_ref[0, r, vc], kd[:, kc])
        for r, a_c in zip(rows, attn):
            for (_, vc), a in zip(heads, a_c):
                o_ref[0, r, vc] = _dot(jnp.where(keep, a, 0.0).astype(BF16), v_ref[0, r, vc])

    def recur(i):
        c = (n_chunks - 1 - i) if reverse else i
        rows = pl.ds(c * GLA_CHUNK, GLA_CHUNK)
        for hd, (kc, vc) in enumerate(heads):
            st = state[hd]
            o_ref[0, rows, vc] += _dot_nt(qe_scr[c, :, kc], st.astype(BF16))
            state[hd] = st * dec_scr[c, 0:1, kc] + u_scr[c, hd]

    return local, recur, n_chunks // group


def _gla_scan_kernel(q_ref, k_ref, v_ref, lg_ref, s0_ref, o_ref, sfin_ref, state, u_scr, dec_scr, qe_scr,
                     *, reverse, n_chunks):
    blk = pl.program_id(1)

    @pl.when(blk == 0)
    def _():
        state[...] = s0_ref[...]

    local, recur, n_groups = _scan_phases(q_ref, k_ref, v_ref, lg_ref, o_ref, state, u_scr, dec_scr, qe_scr,
                                          reverse, n_chunks)
    for g in range(n_groups):
        local(g)
    for i in range(n_chunks):
        recur(i)

    @pl.when(blk == pl.num_programs(1) - 1)
    def _():
        sfin_ref[0] = state[...]


def _gla_both_kernel(qf_ref, kf_ref, vf_ref, lgf_ref, qb_ref, kb_ref, vb_ref, lgb_ref, s0_ref, of_ref, ob_ref,
                     state_f, u_f, dec_f, qe_f, state_b, u_b, dec_b, qe_b, *, n_chunks):
    @pl.when(pl.program_id(1) == 0)
    def _():
        state_f[...] = s0_ref[...]
        state_b[...] = jnp.zeros_like(state_b)

    local_f, recur_f, n_groups = _scan_phases(qf_ref, kf_ref, vf_ref, lgf_ref, of_ref, state_f, u_f, dec_f, qe_f,
                                              False, n_chunks)
    local_b, recur_b, _ = _scan_phases(qb_ref, kb_ref, vb_ref, lgb_ref, ob_ref, state_b, u_b, dec_b, qe_b,
                                       True, n_chunks)
    for g in range(n_groups):
        local_f(g)
        local_b(g)
    for i in range(n_chunks):
        recur_f(i)
        recur_b(i)


def _gla_scan(q, k, v, lg, s0, reverse, t_blk):
    b, t, _ = q.shape
    nb = t // t_blk
    lg_col = 1 if reverse else 0
    blk = (lambda bi, i: (bi, nb - 1 - i, 0)) if reverse else (lambda bi, i: (bi, i, 0))
    lg_blk = (lambda bi, i: (bi, nb - 1 - i, lg_col)) if reverse else (lambda bi, i: (bi, i, lg_col))
    st_shape = (GLA_H, GLA_DV, GLA_DK)
    n_chunks = t_blk // GLA_CHUNK
    return pl.pallas_call(
        functools.partial(_gla_scan_kernel, reverse=reverse, n_chunks=n_chunks),
        grid=(b, nb),
        in_specs=[
            pl.BlockSpec((1, t_blk, GLA_H * GLA_DK), blk),
            pl.BlockSpec((1, t_blk, GLA_H * GLA_DK), blk),
            pl.BlockSpec((1, t_blk, GLA_H * GLA_DV), blk),
            pl.BlockSpec((1, t_blk, GLA_H * GLA_DK), lg_blk),
            pl.BlockSpec(st_shape, lambda bi, i: (0, 0, 0)),
        ],
        out_specs=[
            pl.BlockSpec((1, t_blk, GLA_H * GLA_DV), blk),
            pl.BlockSpec((1,) + st_shape, lambda bi, i: (bi, 0, 0, 0)),
        ],
        out_shape=[
            jax.ShapeDtypeStruct((b, t, GLA_H * GLA_DV), F32),
            jax.ShapeDtypeStruct((b,) + st_shape, F32),
        ],
        scratch_shapes=[
            pltpu.VMEM(st_shape, F32),
            pltpu.VMEM((n_chunks,) + st_shape, F32),
            pltpu.VMEM((n_chunks, 8, GLA_H * GLA_DK), F32),
            pltpu.VMEM((n_chunks, GLA_CHUNK, GLA_H * GLA_DK), BF16),
        ],
        compiler_params=_params("parallel", "arbitrary"),
        name="gla_scan_bwd" if reverse else "gla_scan_fwd",
    )(q, k, v, lg, s0)


def _gla_scan_both(q, k, v, lg, s0, t_blk):
    b, t, _ = q.shape
    nb = t // t_blk
    nk, nv = GLA_H * GLA_DK, GLA_H * GLA_DV
    st_shape = (GLA_H, GLA_DV, GLA_DK)
    n_chunks = t_blk // GLA_CHUNK
    fwd = lambda bi, i: (bi, i, 0)
    bwd = lambda bi, i: (bi, nb - 1 - i, 0)
    bwd_lg = lambda bi, i: (bi, nb - 1 - i, 1)
    per_direction = [
        pltpu.VMEM(st_shape, F32),
        pltpu.VMEM((n_chunks,) + st_shape, F32),
        pltpu.VMEM((n_chunks, 8, nk), F32),
        pltpu.VMEM((n_chunks, GLA_CHUNK, nk), BF16),
    ]
    return pl.pallas_call(
        functools.partial(_gla_both_kernel, n_chunks=n_chunks),
        grid=(b, nb),
        in_specs=[
            pl.BlockSpec((1, t_blk, nk), fwd), pl.BlockSpec((1, t_blk, nk), fwd),
            pl.BlockSpec((1, t_blk, nv), fwd), pl.BlockSpec((1, t_blk, nk), fwd),
            pl.BlockSpec((1, t_blk, nk), bwd), pl.BlockSpec((1, t_blk, nk), bwd),
            pl.BlockSpec((1, t_blk, nv), bwd), pl.BlockSpec((1, t_blk, nk), bwd_lg),
            pl.BlockSpec(st_shape, lambda bi, i: (0, 0, 0)),
        ],
        out_specs=[pl.BlockSpec((1, t_blk, nv), fwd), pl.BlockSpec((1, t_blk, nv), bwd)],
        out_shape=[jax.ShapeDtypeStruct((b, t, nv), F32), jax.ShapeDtypeStruct((b, t, nv), F32)],
        scratch_shapes=per_direction + per_direction,
        compiler_params=_params("parallel", "arbitrary"),
        name="gla_scan_both",
    )(q, k, v, lg, q, k, v, lg, s0)


def _mix_ffn_kernel(x_ref, oa_ref, of_ref, ob_ref, og_ref, gate_ref, womla_ref, onorm_ref, wogla_ref,
                    wout_ref, fn_ref, wgate_ref, wup_ref, wdown_ref, y_ref):
    y_a = _dot(oa_ref[...], womla_ref[...])
    o = of_ref[...] + ob_ref[...]
    og = og_ref[...]
    parts = []
    for hd in range(GLA_H):
        cols = slice(hd * GLA_DV, (hd + 1) * GLA_DV)
        parts.append((_rms(o[:, cols]) * onorm_ref[...] * og[:, cols]).astype(BF16))
    y_b = _dot(jnp.concatenate(parts, axis=-1), wogla_ref[...])
    gates = gate_ref[...]
    mixed = (gates[:, :D_MODEL] * y_a + gates[:, D_MODEL:] * y_b).astype(BF16)
    x1 = x_ref[...] + _dot(mixed, wout_ref[...])
    h = (_rms(x1) * fn_ref[...]).astype(BF16)
    g = _dot(h, wgate_ref[...])
    u = _dot(h, wup_ref[...])
    act = (g * jax.nn.sigmoid(g) * u).astype(BF16)
    y_ref[...] = x1 + _dot(act, wdown_ref[...])


def _mix_ffn(x2d, o_attn, o_f, o_b, og, gates, w, tm):
    n = x2d.shape[0]
    row = lambda i: (i, 0)
    return pl.pallas_call(
        _mix_ffn_kernel,
        grid=(n // tm,),
        in_specs=[
            pl.BlockSpec((tm, D_MODEL), row),
            pl.BlockSpec((tm, MLA_H * V_DIM), row),
            pl.BlockSpec((tm, GLA_H * GLA_DV), row),
            pl.BlockSpec((tm, GLA_H * GLA_DV), row),
            pl.BlockSpec((tm, GLA_H * GLA_DV), row),
            pl.BlockSpec((tm, 2 * D_MODEL), row),
            _resident(w["w_o_mla"].shape),
            _resident((1, GLA_DV)),
            _resident(w["w_o_gla"].shape),
            _resident(w["w_out"].shape),
            _resident((1, D_MODEL)),
            _resident(w["w_ffn_gate"].shape),
            _resident(w["w_ffn_up"].shape),
            _resident(w["w_ffn_down"].shape),
        ],
        out_specs=pl.BlockSpec((tm, D_MODEL), row),
        out_shape=jax.ShapeDtypeStruct((n, D_MODEL), F32),
        compiler_params=_params("parallel"),
        name="mix_ffn",
    )(x2d, o_attn, o_f, o_b, og, gates, w["w_o_mla"], w["gla_o_norm"], w["w_o_gla"], w["w_out"],
      w["ffn_norm"], w["w_ffn_gate"], w["w_ffn_up"], w["w_ffn_down"])


def _prep_weights(attn_norm, w_in, q_a_norm, w_uq, kv_a_norm, w_ukv, q_norm, k_norm, w_o_mla,
                  w_a2_fwd, b_a2_fwd, w_a2_bwd, b_a2_bwd, gla_o_norm, w_o_gla, w_out, ffn_norm,
                  w_ffn_gate, w_ffn_up, w_ffn_down):
    half = ROPE_DIM // 2
    w_in, w_uq, w_ukv = w_in.astype(BF16), w_uq.astype(BF16), w_ukv.astype(BF16)
    o = 0
    cols = {}
    for name, size in (("cq", Q_LORA), ("ckv", KV_LORA), ("kr", ROPE_DIM), ("gq", GLA_H * GLA_DK),
                       ("gk", GLA_H * GLA_DK), ("gv", GLA_H * GLA_DV), ("gg", GLA_H * GLA_DV),
                       ("af", GATE_RANK), ("ab", GATE_RANK), ("ga", D_MODEL), ("gb", D_MODEL)):
        cols[name] = w_in[:, o:o + size]
        o += size
    kr1, kr2 = cols["kr"][:, :half], cols["kr"][:, half:]
    w_ckvr = jnp.concatenate([cols["ckv"], kr1, kr2, kr1, kr2, kr2, kr1, kr2, kr1], axis=1)

    uq = w_uq.reshape(Q_LORA, MLA_H, QK_DIM)
    uq_nope = uq[:, :, :NOPE_DIM].reshape(Q_LORA, MLA_H * NOPE_DIM)
    uq_r = uq[:, :, NOPE_DIM:]
    uq_a = uq_r.reshape(Q_LORA, MLA_H * ROPE_DIM)
    uq_b = jnp.concatenate([uq_r[:, :, half:], uq_r[:, :, :half]], axis=-1).reshape(Q_LORA, MLA_H * ROPE_DIM)
    ukv = w_ukv.reshape(KV_LORA, MLA_H, NOPE_DIM + V_DIM)
    w_ukv2 = jnp.concatenate([ukv[:, :, :NOPE_DIM].reshape(KV_LORA, MLA_H * NOPE_DIM),
                              ukv[:, :, NOPE_DIM:].reshape(KV_LORA, MLA_H * V_DIM)], axis=1)

    def gain_rows(g):
        z = jnp.zeros((ROPE_DIM,), F32)
        ga = g[NOPE_DIM:]
        gb = jnp.concatenate([g[NOPE_DIM + half:], g[NOPE_DIM:NOPE_DIM + half]])
        rows = [g[:NOPE_DIM], jnp.concatenate([ga, z]), jnp.concatenate([z, ga]),
                jnp.concatenate([gb, z]), jnp.concatenate([z, gb])]
        rows += [jnp.zeros((LANES,), F32)] * 3
        return jnp.stack(rows)

    zpad = jnp.zeros((D_MODEL, LANES - 2 * GATE_RANK), BF16)
    w_g = jnp.concatenate([cols["gq"], cols["gk"], cols["gv"], cols["gg"], cols["af"], cols["ab"], zpad,
                           cols["ga"], cols["gb"]], axis=1)
    nk = GLA_H * GLA_DK
    w_a2 = jnp.zeros((LANES, 2 * nk), F32)
    w_a2 = w_a2.at[:GATE_RANK, :nk].set(w_a2_fwd).at[GATE_RANK:2 * GATE_RANK, nk:].set(w_a2_bwd)
    return {
        "attn_norm": attn_norm[None], "w_cq": cols["cq"], "w_ckvr": w_ckvr,
        "q_a_norm": q_a_norm[None], "w_uq": jnp.concatenate([uq_nope, uq_a, uq_b], axis=1),
        "kv_a_norm": kv_a_norm[None], "w_ukv": w_ukv2,
        "g_q": gain_rows(q_norm), "g_k": gain_rows(k_norm),
        "w_g": w_g, "w_a2": w_a2.astype(BF16),
        "b_a2": jnp.concatenate([b_a2_fwd, b_a2_bwd])[None],
        "w_o_mla": w_o_mla.astype(BF16), "gla_o_norm": gla_o_norm[None], "w_o_gla": w_o_gla.astype(BF16),
        "w_out": w_out.astype(BF16), "ffn_norm": ffn_norm[None], "w_ffn_gate": w_ffn_gate.astype(BF16),
        "w_ffn_up": w_ffn_up.astype(BF16), "w_ffn_down": w_ffn_down.astype(BF16),
    }


def _rope_tables(length):
    inv = 1.0 / (ROPE_THETA ** (np.arange(0, ROPE_DIM, 2, dtype=np.float64) / ROPE_DIM))
    ang = np.arange(length, dtype=np.float64)[:, None] * inv[None, :]
    cos, sin = np.cos(ang), np.sin(ang)
    return (jnp.asarray(np.tile(cos, (1, 4)), F32),
            jnp.asarray(np.concatenate([-sin, sin, -sin, sin], axis=1), F32))


def _encode_group(x, w, meta, cos_t, sin_t):
    b, seq, _ = x.shape
    n = b * seq
    x2d = x.reshape(n, D_MODEL)
    tm = _row_tile(seq, 256)
    q, k, vt, gq, gk, gv, og, lg, gates = _project(
        x2d, w, cos_t[N_META:N_META + seq], sin_t[N_META:N_META + seq], seq, tm, True)

    o_attn = _attention(q.reshape(b, seq, -1), k.reshape(b, seq, -1), vt,
                        meta["k"], meta["vt"], _row_tile(seq // 2, 1024), _row_tile(seq // 4, 2048))

    t_blk = _row_tile(seq, 512)
    g3 = lambda a: a.reshape(b, seq, -1)
    o_f, o_b = _gla_scan_both(g3(gq), g3(gk), g3(gv), g3(lg), meta["state"], t_blk)

    y = _mix_ffn(x2d, o_attn.reshape(n, -1), o_f.reshape(n, -1), o_b.reshape(n, -1), og, gates, w, tm)
    return y.reshape(b, seq, D_MODEL)


def kernel(x_prompt, x_sample, meta_tokens, attn_norm, w_in, q_a_norm, w_uq, kv_a_norm, w_ukv, q_norm, k_norm, w_o_mla, w_a2_fwd, b_a2_fwd, w_a2_bwd, b_a2_bwd, gla_o_norm, w_o_gla, w_out, ffn_norm, w_ffn_gate, w_ffn_up, w_ffn_down):
    assert attn_norm.shape[0] == 1, "single-layer encoder"
    w = _prep_weights(attn_norm[0], w_in[0], q_a_norm[0], w_uq[0], kv_a_norm[0], w_ukv[0], q_norm[0],
                      k_norm[0], w_o_mla[0], w_a2_fwd[0], b_a2_fwd[0], w_a2_bwd[0], b_a2_bwd[0],
                      gla_o_norm[0], w_o_gla[0], w_out[0], ffn_norm[0], w_ffn_gate[0], w_ffn_up[0],
                      w_ffn_down[0])
    max_len = N_META + max(x_prompt.shape[1], x_sample.shape[1])
    cos_t, sin_t = _rope_tables(max_len)

    xm = meta_tokens.astype(F32)
    _, k_m, v_m, mq, mk, mv, _, mlg, _ = _project(xm, w, cos_t[:N_META], sin_t[:N_META], N_META, N_META, False)
    pad = lambda a: jnp.pad(a, ((GLA_CHUNK - N_META, 0), (0, 0)))[None]
    _, s_meta = _gla_scan(pad(mq), pad(mk), pad(mv), pad(mlg),
                          jnp.zeros((GLA_H, GLA_DV, GLA_DK), F32), False, GLA_CHUNK)
    vt_m = jnp.concatenate([v_m.T.reshape(MLA_H, V_DIM, N_META), jnp.ones((MLA_H, ONES_ROWS, N_META), BF16)],
                           axis=1).reshape(MLA_H * VT_ROWS, N_META)
    meta = {"k": k_m, "vt": vt_m, "state": s_meta[0]}

    return (_encode_group(x_prompt, w, meta, cos_t, sin_t),
            _encode_group(x_sample, w, meta, cos_t, sin_t))
```
